```python
import math
import jax, jax.numpy as jnp
from jax import lax
import numpy as np


D_MODEL = 1024
BATCH = 8
SEQ = 2048
DEPTH = 2
DEC_BATCH = 128
DEC_SEQ = 1
PAST_LEN = 16384
PAGE_SIZE = 128

MIX_W = D_MODEL
POOL_W = MIX_W // 2
POOL_GROUPS = 4
POOL_WINDOWS = (2, 4, 8, 16)
POOL_GW = POOL_W // POOL_GROUPS
POOL_BUF = max(POOL_WINDOWS) - 1
DN_W = MIX_W - POOL_W
DN_HEADS = 4
DN_DK = DN_W // DN_HEADS
DN_DV = DN_W // DN_HEADS
CONV_W = 4
CONV_CH = 3 * DN_W
CHUNK = 64
D_IN = POOL_W + CONV_CH + DN_W + 2 * DN_HEADS
SPLITS = (POOL_W, POOL_W + CONV_CH, POOL_W + CONV_CH + DN_W, POOL_W + CONV_CH + DN_W + DN_HEADS)
N_EXPERTS = 256
TOP_K = 8
N_GROUPS = 8
TOPK_GROUPS = 4
D_EXPERT = D_MODEL // 4
D_SHARED = D_MODEL // 4
ROUTED_SCALE = 2.5
EPS = 1e-6

kernel_name = 'hybrid_pool_gdn_moe_adaln_step'


def rms_norm(x, w):
    xf = x.astype(jnp.float32)
    y = xf * lax.rsqrt(jnp.mean(xf * xf, axis=-1, keepdims=True) + EPS)
    return (y * w.astype(jnp.float32)).astype(x.dtype)


def l2_normalize(x):
    xf = x.astype(jnp.float32)
    return xf * lax.rsqrt(jnp.sum(xf * xf, axis=-1, keepdims=True) + EPS)


def pool_mixer(u, buf, start_pos, pool_w, pool_scale):
    bsz, t_len, _ = u.shape
    u_ext = jnp.concatenate([buf.astype(u.dtype), u], axis=1)
    cs = jnp.cumsum(u_ext.astype(jnp.float32), axis=1)
    cs = jnp.pad(cs, ((0, 0), (1, 0), (0, 0)))
    hi = cs[:, POOL_BUF + 1:POOL_BUF + 1 + t_len]
    pos = start_pos + jnp.arange(t_len, dtype=jnp.int32)
    parts = []
    for g, win in enumerate(POOL_WINDOWS):
        sl = slice(g * POOL_GW, (g + 1) * POOL_GW)
        lo = cs[:, POOL_BUF + 1 - win:POOL_BUF + 1 - win + t_len, sl]
        cnt = jnp.minimum(pos + 1, win).astype(jnp.float32)[None, :, None]
        parts.append((hi[..., sl] - lo) / cnt)
    pooled = jnp.concatenate(parts, axis=-1) - u.astype(jnp.float32)
    pooled = pooled.astype(u.dtype).reshape(bsz, t_len, POOL_GROUPS, POOL_GW)
    mixed = jnp.einsum('btgc,gcd->btgd', pooled, pool_w).reshape(bsz, t_len, POOL_W)
    return mixed * pool_scale, u_ext[:, -POOL_BUF:]


def short_conv(u, buf, conv_w):
    t_len = u.shape[1]
    u_ext = jnp.concatenate([buf.astype(u.dtype), u], axis=1)
    out = u_ext[:, 0:t_len] * conv_w[0]
    for j in range(1, CONV_W):
        out = out + u_ext[:, j:j + t_len] * conv_w[j]
    return jax.nn.silu(out), u_ext[:, -(CONV_W - 1):]


def _to_chunks(a, n, c):
    bsz, t_len = a.shape[:2]
    pad = n * c - t_len
    a = jnp.pad(a, [(0, 0), (0, pad)] + [(0, 0)] * (a.ndim - 2))
    a = a.reshape((bsz, n, c) + a.shape[2:])
    perm = (1, 0, 3, 2) + tuple(range(4, a.ndim))
    return a.transpose(perm)


def gated_delta_rule(q, k, v, g, beta, s0):
    bsz, t_len, nh, dk = q.shape
    c = CHUNK if t_len >= CHUNK else t_len
    n = -(-t_len // c)
    qc = _to_chunks(q, n, c)
    kc = _to_chunks(k, n, c)
    vc = _to_chunks(v, n, c)
    gc = _to_chunks(g, n, c)
    bc = _to_chunks(beta, n, c)
    gam = jnp.cumsum(gc, axis=-1)
    idx = jnp.arange(c)
    strict = idx[:, None] > idx[None, :]
    incl = idx[:, None] >= idx[None, :]
    diff = gam[..., :, None] - gam[..., None, :]
    dec_strict = jnp.where(strict, jnp.exp(jnp.where(strict, diff, 0.0)), 0.0)
    dec_incl = jnp.where(incl, jnp.exp(jnp.where(incl, diff, 0.0)), 0.0)
    a_mat = bc[..., :, None] * jnp.einsum('nbhid,nbhjd->nbhij', kc, kc) * dec_strict
    rhs = jnp.concatenate([(bc * jnp.exp(gam))[..., None] * kc, bc[..., None] * vc], axis=-1)
    sol = lax.linalg.triangular_solve(a_mat + jnp.eye(c, dtype=a_mat.dtype), rhs,
                                      left_side=True, lower=True)
    w_c, uv_c = sol[..., :dk], sol[..., dk:]
    qk = jnp.einsum('nbhid,nbhjd->nbhij', qc, kc) * dec_incl

    def step(s, xs):
        w_i, uv_i, q_i, k_i, qk_i, gam_i = xs
        u_i = uv_i - jnp.einsum('bhck,bhkv->bhcv', w_i, s)
        o_i = (jnp.exp(gam_i)[..., None] * jnp.einsum('bhck,bhkv->bhcv', q_i, s)
               + jnp.einsum('bhij,bhjv->bhiv', qk_i, u_i))
        g_last = gam_i[..., -1:]
        k_dec = k_i * jnp.exp(g_last - gam_i)[..., None]
        s_new = jnp.exp(g_last)[..., None] * s + jnp.einsum('bhck,bhcv->bhkv', k_dec, u_i)
        return s_new, o_i

    s_fin, o = lax.scan(step, s0.astype(jnp.float32), (w_c, uv_c, qc, kc, qk, gam))
    o = o.transpose(1, 0, 3, 2, 4).reshape(bsz, n * c, nh, -1)[:, :t_len]
    return o, s_fin


def delta_mixer(qkv, z, b_raw, a_raw, conv_buf, s0, conv_w, a_log, dt_bias, dn_norm):
    bsz, t_len, _ = qkv.shape
    qkv_c, new_conv = short_conv(qkv, conv_buf, conv_w)
    q, k, v = jnp.split(qkv_c, 3, axis=-1)
    q = l2_normalize(q.reshape(bsz, t_len, DN_HEADS, DN_DK)) * (DN_DK ** -0.5)
    k = l2_normalize(k.reshape(bsz, t_len, DN_HEADS, DN_DK))
    v = v.reshape(bsz, t_len, DN_HEADS, DN_DV).astype(jnp.float32)
    beta = jax.nn.sigmoid(b_raw.astype(jnp.float32))
    g = -jnp.exp(a_log.astype(jnp.float32)) * jax.nn.softplus(a_raw.astype(jnp.float32) + dt_bias.astype(jnp.float32))
    o, s_fin = gated_delta_rule(q, k, v, g, beta, s0)
    o = rms_norm(o, dn_norm) * jax.nn.silu(z.reshape(bsz, t_len, DN_HEADS, DN_DV).astype(jnp.float32))
    return o.reshape(bsz, t_len, DN_W).astype(qkv.dtype), new_conv, s_fin.astype(s0.dtype)


def route(h2, router_w, router_bias):
    n_tok = h2.shape[0]
    s = jax.nn.sigmoid(h2.astype(jnp.float32) @ router_w.astype(jnp.float32))
    sel = s + router_bias.astype(jnp.float32)
    grp = sel.reshape(n_tok, N_GROUPS, N_EXPERTS // N_GROUPS)
    grp_score = jnp.sum(lax.top_k(grp, 2)[0], axis=-1)
    _, gidx = lax.top_k(grp_score, TOPK_GROUPS)
    gsel = jnp.any(gidx[:, :, None] == jnp.arange(N_GROUPS)[None, None, :], axis=1)
    emask = jnp.repeat(gsel, N_EXPERTS // N_GROUPS, axis=1)
    sel = jnp.where(emask, sel, -jnp.inf)
    _, eidx = lax.top_k(sel, TOP_K)
    wts = jnp.take_along_axis(s, eidx, axis=1)
    wts = wts / jnp.sum(wts, axis=-1, keepdims=True) * ROUTED_SCALE
    return eidx, wts


def routed_experts(h2, eidx, wts, exp_gate, exp_up, exp_down):
    n_tok, d = h2.shape
    nk = n_tok * TOP_K
    m = 128 if nk >= 128 * N_EXPERTS else 16
    p_rows = -(-(nk + N_EXPERTS * (m - 1)) // m) * m
    nb = p_rows // m
    flat_e = eidx.reshape(-1)
    flat_tok = jnp.repeat(jnp.arange(n_tok, dtype=jnp.int32), TOP_K)
    flat_w = wts.reshape(-1)
    order = jnp.argsort(flat_e)
    se = flat_e[order]
    counts = jnp.zeros((N_EXPERTS,), jnp.int32).at[flat_e].add(1)
    starts = jnp.cumsum(counts) - counts
    pcounts = (counts + m - 1) // m * m
    pends = jnp.cumsum(pcounts)
    pstarts = pends - pcounts
    dest = pstarts[se] + (jnp.arange(nk, dtype=jnp.int32) - starts[se])
    row_tok = jnp.full((p_rows,), n_tok, jnp.int32).at[dest].set(flat_tok[order])
    row_w = jnp.zeros((p_rows,), jnp.float32).at[dest].set(flat_w[order])
    block_e = jnp.minimum(jnp.searchsorted(pends, jnp.arange(nb, dtype=jnp.int32) * m, side='right'),
                          N_EXPERTS - 1)
    h_pad = jnp.concatenate([h2, jnp.zeros((1, d), h2.dtype)], axis=0)

    def block(args):
        e, tok, w = args
        xb = h_pad[tok]
        hid = jax.nn.silu(xb @ exp_gate[e]) * (xb @ exp_up[e])
        return (hid @ exp_down[e]) * w[:, None].astype(h2.dtype)

    y = lax.map(block, (block_e, row_tok.reshape(nb, m), row_w.reshape(nb, m)))
    out = jnp.zeros((n_tok + 1, d), h2.dtype).at[row_tok].add(y.reshape(p_rows, d))
    return out[:n_tok]


def moe_ffn(h, p):
    bsz, t_len, d = h.shape
    h2 = h.reshape(-1, d)
    eidx, wts = route(h2, p['router_w'], p['router_bias'])
    routed = routed_experts(h2, eidx, wts, p['exp_gate'], p['exp_up'], p['exp_down'])
    shared = (jax.nn.silu(h2 @ p['sh_gate']) * (h2 @ p['sh_up'])) @ p['sh_down']
    return (routed + shared).reshape(bsz, t_len, d)


def decoder_layer(x, c, pool_buf, conv_buf, s0, start_pos, p):
    mod = jax.nn.silu(c) @ p['w_ada'] + p['b_ada']
    sh_a, sc_a, gt_a, sh_f, sc_f, gt_f = jnp.split(mod[:, None, :], 6, axis=-1)
    h = rms_norm(x, p['norm_mix']) * (1.0 + sc_a) + sh_a
    proj = h @ p['w_in']
    u_pool, qkv, z, b_raw, a_raw = jnp.split(proj, SPLITS, axis=-1)
    pool_out, new_pool = pool_mixer(u_pool, pool_buf, start_pos, p['pool_w'], p['pool_scale'])
    dn_out, new_conv, s_new = delta_mixer(qkv, z, b_raw, a_raw, conv_buf, s0, p['conv_w'],
                                          p['a_log'], p['dt_bias'], p['dn_norm'])
    x = x + gt_a * (jnp.concatenate([pool_out, dn_out], axis=-1) @ p['w_out'])
    h = rms_norm(x, p['norm_ffn']) * (1.0 + sc_f) + sh_f
    x = x + gt_f * moe_ffn(h, p)
    return x, new_pool, new_conv, s_new


def setup_inputs(seed: int = 0) -> dict:
    key = jax.random.key(seed)
    ks = jax.random.split(key, 32)
    f32 = jnp.float32
    D = D_MODEL

    def nrm(k, shape, scale):
        return jax.random.normal(k, shape, f32) * scale

    dt = jnp.exp(jax.random.uniform(ks[16], (DEPTH, DN_HEADS), f32, math.log(1e-3), math.log(1e-1)))
    return {
        'x_prompt': nrm(ks[0], (BATCH, SEQ, D), 1.0),
        'x_sample': nrm(ks[1], (DEC_BATCH, DEC_SEQ, D), 1.0),
        'state_pool': nrm(ks[2], (DEPTH, DEC_BATCH, POOL_BUF, POOL_W), 1.0),
        'state_conv': nrm(ks[3], (DEPTH, DEC_BATCH, CONV_W - 1, CONV_CH), 1.0),
        'state_delta': nrm(ks[4], (DEPTH, DEC_BATCH, DN_HEADS, DN_DK, DN_DV), 0.1),
        'c_prompt': nrm(ks[5], (BATCH, D), 1.0),
        'c_sample': nrm(ks[6], (DEC_BATCH, D), 1.0),
        'norm_mix': 1.0 + nrm(ks[7], (DEPTH, D), 0.05),
        'norm_ffn': 1.0 + nrm(ks[8], (DEPTH, D), 0.05),
        'w_ada': nrm(ks[9], (DEPTH, D, 6 * D), 0.3 * D ** -0.5),
        'b_ada': nrm(ks[10], (DEPTH, 6 * D), 0.02),
        'w_in': nrm(ks[11], (DEPTH, D, D_IN), D ** -0.5),
        'pool_w': nrm(ks[12], (DEPTH, POOL_GROUPS, POOL_GW, POOL_GW), POOL_GW ** -0.5),
        'pool_scale': 1.0 + nrm(ks[13], (DEPTH, POOL_W), 0.1),
        'conv_w': nrm(ks[14], (DEPTH, CONV_W, CONV_CH), CONV_W ** -0.5),
        'a_log': jnp.log(jax.random.uniform(ks[15], (DEPTH, DN_HEADS), f32, 1.0, 16.0)),
        'dt_bias': dt + jnp.log(-jnp.expm1(-dt)),
        'dn_norm': 1.0 + nrm(ks[17], (DEPTH, DN_DV), 0.05),
        'w_out': nrm(ks[18], (DEPTH, MIX_W, D), MIX_W ** -0.5),
        'router_w': nrm(ks[19], (DEPTH, D, N_EXPERTS), D ** -0.5),
        'router_bias': nrm(ks[20], (DEPTH, N_EXPERTS), 0.01),
        'exp_gate': nrm(ks[21], (DEPTH, N_EXPERTS, D, D_EXPERT), D ** -0.5),
        'exp_up': nrm(ks[22], (DEPTH, N_EXPERTS, D, D_EXPERT), D ** -0.5),
        'exp_down': nrm(ks[23], (DEPTH, N_EXPERTS, D_EXPERT, D), D_EXPERT ** -0.5),
        'sh_gate': nrm(ks[24], (DEPTH, D, D_SHARED), D ** -0.5),
        'sh_up': nrm(ks[25], (DEPTH, D, D_SHARED), D ** -0.5),
        'sh_down': nrm(ks[26], (DEPTH, D_SHARED, D), D_SHARED ** -0.5),
        'norm_final': 1.0 + nrm(ks[27], (D,), 0.05),
    }


def reference(x_prompt, x_sample, state_pool, state_conv, state_delta, c_prompt, c_sample,
              norm_mix, norm_ffn, w_ada, b_ada, w_in, pool_w, pool_scale, conv_w, a_log, dt_bias,
              dn_norm, w_out, router_w, router_bias, exp_gate, exp_up, exp_down, sh_gate, sh_up,
              sh_down, norm_final):
    bsz = x_prompt.shape[0]
    pool0 = jnp.zeros((bsz, POOL_BUF, POOL_W), x_prompt.dtype)
    conv0 = jnp.zeros((bsz, CONV_W - 1, CONV_CH), x_prompt.dtype)
    s_zero = jnp.zeros((bsz, DN_HEADS, DN_DK, DN_DV), state_delta.dtype)
    xp, xs = x_prompt, x_sample
    pool_p, conv_p, delta_p = [], [], []
    pool_s, conv_s, delta_s = [], [], []
    for l in range(DEPTH):
        p = {'norm_mix': norm_mix[l], 'norm_ffn': norm_ffn[l], 'w_ada': w_ada[l], 'b_ada': b_ada[l],
             'w_in': w_in[l], 'pool_w': pool_w[l], 'pool_scale': pool_scale[l], 'conv_w': conv_w[l],
             'a_log': a_log[l], 'dt_bias': dt_bias[l], 'dn_norm': dn_norm[l], 'w_out': w_out[l],
             'router_w': router_w[l], 'router_bias': router_bias[l], 'exp_gate': exp_gate[l],
             'exp_up': exp_up[l], 'exp_down': exp_down[l], 'sh_gate': sh_gate[l], 'sh_up': sh_up[l],
             'sh_down': sh_down[l]}
        xp, np_l, nc_l, ns_l = decoder_layer(xp, c_prompt, pool0, conv0, s_zero, 0, p)
        pool_p.append(np_l)
        conv_p.append(nc_l)
        delta_p.append(ns_l)
        xs, np_l, nc_l, ns_l = decoder_layer(xs, c_sample, state_pool[l], state_conv[l], state_delta[l], PAST_LEN, p)
        pool_s.append(np_l)
        conv_s.append(nc_l)
        delta_s.append(ns_l)
    y_prompt = rms_norm(xp, norm_final)
    y_sample = rms_norm(xs, norm_final)
    return (y_prompt, y_sample, jnp.stack(pool_p), jnp.stack(conv_p), jnp.stack(delta_p),
            jnp.stack(pool_s), jnp.stack(conv_s), jnp.stack(delta_s))
```

```python
import functools

import jax
import jax.numpy as jnp
from jax import lax
from jax.experimental import pallas as pl
from jax.experimental.pallas import tpu as pltpu

F32, BF16, I32 = jnp.float32, jnp.bfloat16, jnp.int32
HI = lax.Precision.HIGHEST
EPS = 1e-6

LANES = 128
POOL_WINDOWS = (2, 4, 8, 16)
POOL_GW = 128
POOL_W = 512
POOL_BUF = 15
POOL_HALO = 16
DN_HEADS = 4
DN_D = 128
DN_W = 512
CONV_W = 4
CONV_CH = 3 * DN_W
CONV_HALO = 8
CHUNK = 64
INV_BLOCK = 16
N_EXPERTS = 256
TOP_K = 8
N_GROUPS = 8
TOPK_GROUPS = 4
GROUP_SHIFT = 5
ROUTED_SCALE = 2.5
PAST_LEN = 16384
ROW_BLOCK = 128
ROUTER_TM = 256
BA_W = 2 * LANES
D_IN_PAD = POOL_W + CONV_CH + DN_W + BA_W
VMEM_LIMIT = 48 * 2 ** 20


def _cparams(n_axes):
    return pltpu.CompilerParams(dimension_semantics=("arbitrary",) * n_axes,
                                vmem_limit_bytes=VMEM_LIMIT)


def _sigmoid(x):
    return 1.0 / (1.0 + jnp.exp(-x))


def _silu(x):
    return x * _sigmoid(x)


def _softplus(x):
    return jnp.maximum(x, 0.0) + jnp.log(1.0 + jnp.exp(-jnp.abs(x)))


def _rms(x, w):
    return x * lax.rsqrt(jnp.mean(x * x, axis=-1, keepdims=True) + EPS) * w


def _dot(a, b, precision=None):
    return jnp.dot(a, b, preferred_element_type=F32, precision=precision)


def _dot_nt(a, b, precision=None):
    return lax.dot_general(a, b, (((1,), (1,)), ((), ())), preferred_element_type=F32,
                           precision=precision)


def _ada_kernel(c_ref, w_ref, b_ref, o_ref):
    o_ref[0] = _dot(_silu(c_ref[...]), w_ref[0], HI) + b_ref[0]


def _ada(c_all, w_ada, b_ada):
    depth, d, d6 = w_ada.shape
    m = c_all.shape[0]
    tn = 1024
    return pl.pallas_call(
        _ada_kernel, grid=(depth, d6 // tn),
        in_specs=[pl.BlockSpec((m, d), lambda l, j: (0, 0)),
                  pl.BlockSpec((1, d, tn), lambda l, j: (l, 0, j)),
                  pl.BlockSpec((1, 1, tn), lambda l, j: (l, 0, j))],
        out_specs=pl.BlockSpec((1, m, tn), lambda l, j: (l, 0, j)),
        out_shape=jax.ShapeDtypeStruct((depth, m, d6), F32),
        compiler_params=_cparams(2), name="ada")(c_all, w_ada, b_ada.reshape(depth, 1, d6))


def _mod_spec(rows, d, tiles_per_batch, chunk):
    return pl.BlockSpec((1, rows, d), lambda i: (i // tiles_per_batch, 0, chunk))


def _norm_proj_kernel(x_ref, sh_ref, sc_ref, nw_ref, w_ref, u_ref, qkv_ref, z_ref, ba_ref):
    h = _rms(x_ref[...], nw_ref[...]) * (1.0 + sc_ref[0]) + sh_ref[0]
    hb = h.astype(BF16)
    c0, c1, c2 = POOL_W, POOL_W + CONV_CH, POOL_W + CONV_CH + DN_W
    u_ref[...] = _dot(hb, w_ref[:, :c0])
    qkv_ref[...] = _dot(hb, w_ref[:, c0:c1])
    z_ref[...] = _dot(hb, w_ref[:, c1:c2])
    ba_ref[...] = _dot(hb, w_ref[:, c2:])


def _norm_proj(x2d, mod3, nw, w_in_p, tm, tiles_per_batch):
    n, d = x2d.shape
    rows = mod3.shape[1]
    widths = (POOL_W, CONV_CH, DN_W, BA_W)
    return pl.pallas_call(
        _norm_proj_kernel, grid=(n // tm,),
        in_specs=[pl.BlockSpec((tm, d), lambda i: (i, 0)),
                  _mod_spec(rows, d, tiles_per_batch, 0),
                  _mod_spec(rows, d, tiles_per_batch, 1),
                  pl.BlockSpec((1, d), lambda i: (0, 0)),
                  pl.BlockSpec((d, D_IN_PAD), lambda i: (0, 0))],
        out_specs=[pl.BlockSpec((tm, w), lambda i: (i, 0)) for w in widths],
        out_shape=[jax.ShapeDtypeStruct((n, w), F32) for w in widths],
        compiler_params=_cparams(1), name="norm_proj")(x2d, mod3, mod3, nw, w_in_p)


def _pool_kernel(u_ref, buf_ref, pw_ref, ps_ref, o_ref, ext, *, tt, start_pos):
    t = pl.program_id(1)

    @pl.when(t == 0)
    def _():
        ext[0:POOL_HALO, :] = buf_ref[0]

    @pl.when(t > 0)
    def _():
        ext[0:POOL_HALO, :] = ext[tt:tt + POOL_HALO, :]

    u = u_ref[0]
    ext[POOL_HALO:POOL_HALO + tt, :] = u
    pos = start_pos + t * tt + lax.broadcasted_iota(I32, (tt, 1), 0)
    for g, win in enumerate(POOL_WINDOWS):
        sl = slice(g * POOL_GW, (g + 1) * POOL_GW)
        s = u[:, sl]
        for j in range(1, win):
            s = s + ext[POOL_HALO - j:POOL_HALO - j + tt, sl]
        cnt = jnp.minimum(pos + 1, win).astype(F32)
        pooled = s / cnt - u[:, sl]
        o_ref[0, :, sl] = _dot(pooled.astype(BF16), pw_ref[g]) * ps_ref[:, sl]


def _pool(u3, buf16, pool_w_b, pool_scale, start_pos):
    b, t, w = u3.shape
    tt = min(t, 256)
    assert t % tt == 0 and tt % 8 == 0
    return pl.pallas_call(
        functools.partial(_pool_kernel, tt=tt, start_pos=start_pos), grid=(b, t // tt),
        in_specs=[pl.BlockSpec((1, tt, w), lambda i, j: (i, j, 0)),
                  pl.BlockSpec((1, POOL_HALO, w), lambda i, j: (i, 0, 0)),
                  pl.BlockSpec((len(POOL_WINDOWS), POOL_GW, POOL_GW), lambda i, j: (0, 0, 0)),
                  pl.BlockSpec((1, w), lambda i, j: (0, 0))],
        out_specs=pl.BlockSpec((1, tt, w), lambda i, j: (i, j, 0)),
        out_shape=jax.ShapeDtypeStruct((b, t, w), F32),
        scratch_shapes=[pltpu.VMEM((POOL_HALO + tt, w), F32)],
        compiler_params=_cparams(2), name="pool")(u3, buf16, pool_w_b, pool_scale)


def _unit_lower_inverse(a, row, col, eye):
    n = -a
    same = (row // INV_BLOCK) == (col // INV_BLOCK)
    dm = jnp.where(same, n, 0.0)
    lm = n - dm
    d2 = _dot(dm, dm, HI)
    d4 = _dot(d2, d2, HI)
    d8 = _dot(d4, d4, HI)
    p = _dot(eye + dm, eye + d2, HI)
    p = _dot(p, eye + d4, HI)
    p = _dot(p, eye + d8, HI)
    m = _dot(p, lm, HI)
    m2 = _dot(m, m, HI)
    q = _dot(eye + m, eye + m2, HI)
    return _dot(q, p, HI)


def _gdn_kernel(qkv_ref, z_ref, ba_ref, cbuf_ref, s0_ref, cw_ref, par_ref, nw_ref,
                o_ref, sfin_ref, state, xext, *, t_len):
    c = pl.program_id(1)

    @pl.when(c == 0)
    def _():
        state[...] = s0_ref[0]
        xext[0:CONV_HALO, :] = cbuf_ref[0]

    @pl.when(c > 0)
    def _():
        xext[0:CONV_HALO, :] = xext[CHUNK:CHUNK + CONV_HALO, :]

    xext[CONV_HALO:CONV_HALO + CHUNK, :] = qkv_ref[0]
    base = CONV_HALO - (CONV_W - 1)
    acc = xext[base:base + CHUNK, :] * cw_ref[0:1, :]
    for j in range(1, CONV_W):
        acc = acc + xext[base + j:base + j + CHUNK, :] * cw_ref[j:j + 1, :]
    qkvc = _silu(acc)

    valid = (c * CHUNK + lax.broadcasted_iota(I32, (CHUNK, 1), 0)) < t_len
    vf = valid.astype(F32)
    ba = ba_ref[0]
    head_lane = lax.broadcasted_iota(I32, (1, LANES), 1) < DN_HEADS
    beta_all = _sigmoid(ba[:, :LANES]) * vf
    decay = jnp.where(head_lane, -jnp.exp(par_ref[0:1, :]), 0.0)
    g_all = decay * _softplus(ba[:, LANES:] + par_ref[1:2, :]) * vf

    row = lax.broadcasted_iota(I32, (CHUNK, CHUNK), 0)
    col = lax.broadcasted_iota(I32, (CHUNK, CHUNK), 1)
    incl = row >= col
    strict = row > col
    eye = (row == col).astype(F32)
    gam_all = _dot(incl.astype(F32), g_all, HI)
    pick = (lax.broadcasted_iota(I32, (8, LANES), 0) == lax.broadcasted_iota(I32, (8, LANES), 1)).astype(F32)
    gam_rows = _dot_nt(pick, gam_all, HI)

    for h in range(DN_HEADS):
        sl = slice(h * DN_D, (h + 1) * DN_D)
        q = qkvc[:, sl]
        k = qkvc[:, DN_W + h * DN_D:DN_W + (h + 1) * DN_D]
        v = qkvc[:, 2 * DN_W + h * DN_D:2 * DN_W + (h + 1) * DN_D] * vf
        q = q * lax.rsqrt(jnp.sum(q * q, axis=-1, keepdims=True) + EPS) * (DN_D ** -0.5) * vf
        k = k * lax.rsqrt(jnp.sum(k * k, axis=-1, keepdims=True) + EPS) * vf
        beta = beta_all[:, h:h + 1]
        gam = gam_all[:, h:h + 1]
        diff = gam - gam_rows[h:h + 1, :]
        dec_s = jnp.where(strict, jnp.exp(jnp.where(strict, diff, 0.0)), 0.0)
        dec_i = jnp.where(incl, jnp.exp(jnp.where(incl, diff, 0.0)), 0.0)
        kb = k.astype(BF16)
        qb = q.astype(BF16)
        a_mat = beta * _dot_nt(kb, kb) * dec_s
        qk = _dot_nt(qb, kb) * dec_i
        egam = jnp.exp(gam)
        rhs = jnp.concatenate([(beta * egam) * k, beta * v], axis=-1)
        sol = _dot(_unit_lower_inverse(a_mat, row, col, eye), rhs, HI)
        w_c = sol[:, :DN_D]
        uv_c = sol[:, DN_D:]
        s_h = state[h]
        s_b = s_h.astype(BF16)
        u = uv_c - _dot(w_c.astype(BF16), s_b)
        ub = u.astype(BF16)
        o = egam * _dot(qb, s_b) + _dot(qk.astype(BF16), ub)
        g_last = gam[CHUNK - 1:CHUNK, :]
        k_dec = k * jnp.exp(g_last - gam)
        state[h] = jnp.exp(g_last) * s_h + _dot(k_dec.T.astype(BF16), ub)
        zz = z_ref[0, :, sl]
        o_ref[0, :, sl] = _rms(o, nw_ref[...]) * _silu(zz)

    @pl.when(c == pl.num_programs(1) - 1)
    def _():
        sfin_ref[0] = state[...]


def _gdn(qkv3, z3, ba3, cbuf8, s0, conv_w8, par, dn_norm, t_len):
    b, tp, _ = qkv3.shape
    assert tp % CHUNK == 0
    kern = functools.partial(_gdn_kernel, t_len=t_len)
    return pl.pallas_call(
        kern, grid=(b, tp // CHUNK),
        in_specs=[pl.BlockSpec((1, CHUNK, CONV_CH), lambda i, j: (i, j, 0)),
                  pl.BlockSpec((1, CHUNK, DN_W), lambda i, j: (i, j, 0)),
                  pl.BlockSpec((1, CHUNK, BA_W), lambda i, j: (i, j, 0)),
                  pl.BlockSpec((1, CONV_HALO, CONV_CH), lambda i, j: (i, 0, 0)),
                  pl.BlockSpec((1, DN_HEADS, DN_D, DN_D), lambda i, j: (i, 0, 0, 0)),
                  pl.BlockSpec((8, CONV_CH), lambda i, j: (0, 0)),
                  pl.BlockSpec((8, LANES), lambda i, j: (0, 0)),
                  pl.BlockSpec((1, DN_D), lambda i, j: (0, 0))],
        out_specs=[pl.BlockSpec((1, CHUNK, DN_W), lambda i, j: (i, j, 0)),
                   pl.BlockSpec((1, DN_HEADS, DN_D, DN_D), lambda i, j: (i, 0, 0, 0))],
        out_shape=[jax.ShapeDtypeStruct((b, tp, DN_W), F32),
                   jax.ShapeDtypeStruct((b, DN_HEADS, DN_D, DN_D), F32)],
        scratch_shapes=[pltpu.VMEM((DN_HEADS, DN_D, DN_D), F32),
                        pltpu.VMEM((CONV_HALO + CHUNK, CONV_CH), F32)],
        compiler_params=_cparams(2), name="gdn")(qkv3, z3, ba3, cbuf8, s0, conv_w8, par, dn_norm)


def _out_proj_kernel(pool_ref, dn_ref, x_ref, gt_ref, sh_ref, sc_ref, nw_ref, w_ref, x1_ref, h2_ref):
    mix = _dot(pool_ref[...].astype(BF16), w_ref[:POOL_W, :]) + _dot(dn_ref[...].astype(BF16), w_ref[POOL_W:, :])
    x1 = x_ref[...] + gt_ref[0] * mix
    x1_ref[...] = x1
    h2_ref[...] = _rms(x1, nw_ref[...]) * (1.0 + sc_ref[0]) + sh_ref[0]


def _out_proj(pool2, dn2, x2d, mod3, nw, w_out_b, tm, tiles_per_batch):
    n, d = x2d.shape
    rows = mod3.shape[1]
    tok = lambda w: pl.BlockSpec((tm, w), lambda i: (i, 0))
    return pl.pallas_call(
        _out_proj_kernel, grid=(n // tm,),
        in_specs=[tok(POOL_W), tok(DN_W), tok(d),
                  _mod_spec(rows, d, tiles_per_batch, 2),
                  _mod_spec(rows, d, tiles_per_batch, 3),
                  _mod_spec(rows, d, tiles_per_batch, 4),
                  pl.BlockSpec((1, d), lambda i: (0, 0)),
                  pl.BlockSpec((POOL_W + DN_W, d), lambda i: (0, 0))],
        out_specs=[tok(d), tok(d)],
        out_shape=[jax.ShapeDtypeStruct((n, d), F32)] * 2,
        compiler_params=_cparams(1), name="out_proj")(pool2, dn2, x2d, mod3, mod3, mod3, nw, w_out_b)


def _router_kernel(h_ref, rw_ref, rb_ref, eidx_ref, wts_ref, rank_ref, cnt_ref, run, *, tm, n_valid):
    i = pl.program_id(0)

    @pl.when(i == 0)
    def _():
        run[...] = jnp.zeros_like(run)

    s = _sigmoid(_dot(h_ref[...].astype(BF16), rw_ref[...]))
    sel = s + rb_ref[...]
    lane_i = lax.broadcasted_iota(I32, (tm, N_EXPERTS), 1)
    lane = lane_i.astype(F32)
    grp = lane_i >> GROUP_SHIFT
    neg = -jnp.inf

    def first_max(x):
        m = jnp.max(x, axis=-1, keepdims=True)
        idx = jnp.min(jnp.where(x == m, lane, float(N_EXPERTS)), axis=-1, keepdims=True)
        return m, idx

    gs = []
    for g in range(N_GROUPS):
        xg = jnp.where(grp == g, sel, neg)
        m1, i1 = first_max(xg)
        m2 = jnp.max(jnp.where(lane == i1, neg, xg), axis=-1, keepdims=True)
        gs.append(m1 + m2)
    cur = jnp.full((tm, N_EXPERTS), neg, F32)
    for g in range(N_GROUPS):
        ahead = jnp.zeros((tm, 1), F32)
        for o in range(N_GROUPS):
            if o < g:
                ahead = ahead + jnp.where(gs[o] >= gs[g], 1.0, 0.0)
            elif o > g:
                ahead = ahead + jnp.where(gs[o] > gs[g], 1.0, 0.0)
        cur = jnp.where((grp == g) & (ahead < TOPK_GROUPS), sel, cur)

    row_valid = (i * tm + lax.broadcasted_iota(I32, (tm, 1), 0)) < n_valid
    hits, idxs, ws = [], [], []
    chosen = jnp.zeros((tm, N_EXPERTS), F32)
    for _ in range(TOP_K):
        _, idx = first_max(cur)
        hit = lane == idx
        ws.append(jnp.sum(jnp.where(hit, s, 0.0), axis=-1, keepdims=True))
        cur = jnp.where(hit, neg, cur)
        chosen = jnp.where(hit, 1.0, chosen)
        hits.append(hit)
        idxs.append(idx)
    onehot = jnp.where(row_valid, chosen, 0.0)
    r = lax.broadcasted_iota(I32, (tm, tm), 0)
    c = lax.broadcasted_iota(I32, (tm, tm), 1)
    earlier = jnp.where(r > c, 1.0, 0.0).astype(BF16)
    before = _dot(earlier, onehot.astype(BF16)) + run[...]
    run[...] = run[...] + jnp.sum(onehot, axis=0, keepdims=True)

    wsum = ws[0]
    for w in ws[1:]:
        wsum = wsum + w
    out_lane = lax.broadcasted_iota(I32, (tm, LANES), 1)
    eidx = jnp.zeros((tm, LANES), F32)
    wts = jnp.zeros((tm, LANES), F32)
    rank = jnp.zeros((tm, LANES), F32)
    for k in range(TOP_K):
        rk = jnp.sum(jnp.where(hits[k], before, 0.0), axis=-1, keepdims=True)
        eidx = jnp.where(out_lane == k, idxs[k], eidx)
        wts = jnp.where(out_lane == k, ws[k] / wsum * ROUTED_SCALE, wts)
        rank = jnp.where(out_lane == k, rk, rank)
    eidx_ref[...] = eidx.astype(I32)
    wts_ref[...] = wts
    rank_ref[...] = rank.astype(I32)

    @pl.when(i == pl.num_programs(0) - 1)
    def _():
        cnt_ref[...] = run[...]


def _router(h_all, router_w_b, router_bias, n_valid):
    n_pad, d = h_all.shape
    tm = ROUTER_TM
    kern = functools.partial(_router_kernel, tm=tm, n_valid=n_valid)
    tok = pl.BlockSpec((tm, LANES), lambda i: (i, 0))
    return pl.pallas_call(
        kern, grid=(n_pad // tm,),
        in_specs=[pl.BlockSpec((tm, d), lambda i: (i, 0)),
                  pl.BlockSpec((d, N_EXPERTS), lambda i: (0, 0)),
                  pl.BlockSpec((1, N_EXPERTS), lambda i: (0, 0))],
        out_specs=[tok, tok, tok, pl.BlockSpec((1, N_EXPERTS), lambda i: (0, 0))],
        out_shape=[jax.ShapeDtypeStruct((n_pad, LANES), I32),
                   jax.ShapeDtypeStruct((n_pad, LANES), F32),
                   jax.ShapeDtypeStruct((n_pad, LANES), I32),
                   jax.ShapeDtypeStruct((1, N_EXPERTS), F32)],
        scratch_shapes=[pltpu.VMEM((1, N_EXPERTS), F32)],
        compiler_params=_cparams(1), name="router")(h_all, router_w_b, router_bias)


def _gather_rows(idx_ref, src_hbm, dst, sem, n_rows):
    def body(r, carry):
        tok = idx_ref[0, 0, r]
        pltpu.make_async_copy(src_hbm.at[pl.ds(tok, 1)], dst.at[pl.ds(r, 1)], sem).start()
        return carry
    lax.fori_loop(0, n_rows, body, 0, unroll=8)


def _wait_rows(src_hbm, dst, sem, n_rows):
    pltpu.make_async_copy(src_hbm.at[pl.ds(0, n_rows)], dst, sem).wait()


def _experts_kernel(be_ref, nact_ref, cur_ref, nxt_ref, h_hbm, g_ref, u_ref, d_ref, y_ref,
                    xbuf, sems, gb, ub, db):
    i = pl.program_id(0)
    nact = nact_ref[0]
    slot = i % 2

    @pl.when(i == 0)
    def _():
        _gather_rows(cur_ref, h_hbm, xbuf.at[0], sems.at[0], ROW_BLOCK)

    @pl.when(i + 1 < nact)
    def _():
        _gather_rows(nxt_ref, h_hbm, xbuf.at[1 - slot], sems.at[1 - slot], ROW_BLOCK)

    changed = jnp.logical_or(i == 0, be_ref[i] != be_ref[jnp.maximum(i - 1, 0)])

    @pl.when(jnp.logical_and(i < nact, changed))
    def _():
        gb[...] = g_ref[0].astype(BF16)
        ub[...] = u_ref[0].astype(BF16)
        db[...] = d_ref[0].astype(BF16)

    @pl.when(jnp.logical_or(i < nact, i == 0))
    def _():
        _wait_rows(h_hbm, xbuf.at[slot], sems.at[slot], ROW_BLOCK)

    @pl.when(i < nact)
    def _():
        x = xbuf[slot].astype(BF16)
        hid = _silu(_dot(x, gb[...])) * _dot(x, ub[...])
        y_ref[...] = _dot(hid.astype(BF16), db[...])

    @pl.when(i >= nact)
    def _():
        y_ref[...] = jnp.zeros_like(y_ref)


def _experts(block_e, nact, row_tok3, h_all, exp_gate, exp_up, exp_down):
    nb = row_tok3.shape[0]
    _, d, de = exp_gate.shape
    grid_spec = pltpu.PrefetchScalarGridSpec(
        num_scalar_prefetch=2, grid=(nb,),
        in_specs=[pl.BlockSpec((1, 1, ROW_BLOCK), lambda i, be, na: (i, 0, 0), memory_space=pltpu.SMEM),
                  pl.BlockSpec((1, 1, ROW_BLOCK), lambda i, be, na: (jnp.minimum(i + 1, nb - 1), 0, 0),
                               memory_space=pltpu.SMEM),
                  pl.BlockSpec(memory_space=pl.ANY),
                  pl.BlockSpec((1, d, de), lambda i, be, na: (be[i], 0, 0)),
                  pl.BlockSpec((1, d, de), lambda i, be, na: (be[i], 0, 0)),
                  pl.BlockSpec((1, de, d), lambda i, be, na: (be[i], 0, 0))],
        out_specs=pl.BlockSpec((ROW_BLOCK, d), lambda i, be, na: (i, 0)),
        scratch_shapes=[pltpu.VMEM((2, ROW_BLOCK, d), F32),
                        pltpu.SemaphoreType.DMA((2,)),
                        pltpu.VMEM((d, de), BF16), pltpu.VMEM((d, de), BF16), pltpu.VMEM((de, d), BF16)])
    return pl.pallas_call(
        _experts_kernel, grid_spec=grid_spec,
        out_shape=jax.ShapeDtypeStruct((nb * ROW_BLOCK, d), F32),
        compiler_params=_cparams(1), name="experts")(
            block_e, nact, row_tok3, row_tok3, h_all, exp_gate, exp_up, exp_down)


def _combine_kernel(cur_ref, nxt_ref, y_hbm, wts_ref, x1_ref, h_ref, gt_ref, sg_ref, su_ref, sd_ref,
                    nf_ref, x2_ref, yn_ref, ybuf, sems, *, tm):
    i = pl.program_id(0)
    n = pl.num_programs(0)
    slot = i % 2
    rows = tm * TOP_K

    @pl.when(i == 0)
    def _():
        _gather_rows(cur_ref, y_hbm, ybuf.at[0], sems.at[0], rows)

    @pl.when(i + 1 < n)
    def _():
        _gather_rows(nxt_ref, y_hbm, ybuf.at[1 - slot], sems.at[1 - slot], rows)

    hb = h_ref[...].astype(BF16)
    hid = _silu(_dot(hb, sg_ref[...])) * _dot(hb, su_ref[...])
    acc = _dot(hid.astype(BF16), sd_ref[...])
    _wait_rows(y_hbm, ybuf.at[slot], sems.at[slot], rows)
    wts = wts_ref[...]
    for k in range(TOP_K):
        acc = acc + ybuf[slot, k * tm:(k + 1) * tm, :] * wts[:, k:k + 1]
    x2 = x1_ref[...] + gt_ref[0] * acc
    x2_ref[...] = x2
    yn_ref[...] = _rms(x2, nf_ref[...])


def _combine(dest3, y_sorted, wts, x1, h2, mod3, sh_gate_b, sh_up_b, sh_down_b, norm_final, tm, tiles_per_batch):
    n, d = x1.shape
    nt = n // tm
    rows = mod3.shape[1]
    ds = sh_gate_b.shape[1]
    tok = lambda w: pl.BlockSpec((tm, w), lambda i: (i, 0))
    kern = functools.partial(_combine_kernel, tm=tm)
    return pl.pallas_call(
        kern, grid=(nt,),
        in_specs=[pl.BlockSpec((1, 1, tm * TOP_K), lambda i: (i, 0, 0), memory_space=pltpu.SMEM),
                  pl.BlockSpec((1, 1, tm * TOP_K), lambda i: (jnp.minimum(i + 1, nt - 1), 0, 0),
                               memory_space=pltpu.SMEM),
                  pl.BlockSpec(memory_space=pl.ANY),
                  tok(LANES), tok(d), tok(d),
                  _mod_spec(rows, d, tiles_per_batch, 5),
                  pl.BlockSpec((d, ds), lambda i: (0, 0)),
                  pl.BlockSpec((d, ds), lambda i: (0, 0)),
                  pl.BlockSpec((ds, d), lambda i: (0, 0)),
                  pl.BlockSpec((1, d), lambda i: (0, 0))],
        out_specs=[tok(d), tok(d)],
        out_shape=[jax.ShapeDtypeStruct((n, d), F32)] * 2,
        scratch_shapes=[pltpu.VMEM((2, tm * TOP_K, d), F32), pltpu.SemaphoreType.DMA((2,))],
        compiler_params=_cparams(1), name="combine")(
            dest3, dest3, y_sorted, wts, x1, h2, mod3, sh_gate_b, sh_up_b, sh_down_b, norm_final)


def _mixers(u, qkv, z, ba, b, t, pool_buf, conv_buf, s0, start_pos, lw):
    t_pool = -(-t // 8) * 8
    t_gdn = -(-t // CHUNK) * CHUNK
    u3 = u.reshape(b, t, POOL_W)
    qkv3 = qkv.reshape(b, t, CONV_CH)
    pad_t = lambda a, tp: a if tp == t else jnp.pad(a, ((0, 0), (0, tp - t), (0, 0)))
    buf16 = jnp.pad(pool_buf, ((0, 0), (POOL_HALO - POOL_BUF, 0), (0, 0)))
    pool_out = _pool(pad_t(u3, t_pool), buf16, lw["pool_w"], lw["pool_scale"], start_pos)[:, :t]
    cbuf8 = jnp.pad(conv_buf, ((0, 0), (CONV_HALO - (CONV_W - 1), 0), (0, 0)))
    dn_out, s_new = _gdn(pad_t(qkv3, t_gdn), pad_t(z.reshape(b, t, DN_W), t_gdn),
                         pad_t(ba.reshape(b, t, BA_W), t_gdn), cbuf8, s0, lw["conv_w"], lw["par"],
                         lw["dn_norm"], t)
    new_pool = jnp.concatenate([pool_buf, u3], axis=1)[:, -POOL_BUF:]
    new_conv = jnp.concatenate([conv_buf, qkv3], axis=1)[:, -(CONV_W - 1):]
    return pool_out.reshape(b * t, POOL_W), dn_out[:, :t].reshape(b * t, DN_W), new_pool, new_conv, s_new


def _dispatch_tables(eidx, rank, counts, n_tok):
    nk = n_tok * TOP_K
    nb = -(-(nk + N_EXPERTS * (ROW_BLOCK - 1)) // ROW_BLOCK)
    pcounts = (counts + ROW_BLOCK - 1) // ROW_BLOCK * ROW_BLOCK
    pends = jnp.cumsum(pcounts)
    pstarts = pends - pcounts
    dest = pstarts[eidx] + rank
    tok = jnp.broadcast_to(jnp.arange(n_tok, dtype=I32)[:, None], (n_tok, TOP_K))
    row_tok = jnp.zeros((nb * ROW_BLOCK,), I32).at[dest.reshape(-1)].set(tok.reshape(-1))
    block_e = jnp.minimum(jnp.searchsorted(pends, jnp.arange(nb, dtype=I32) * ROW_BLOCK, side="right"),
                          N_EXPERTS - 1).astype(I32)
    nact = (pends[-1] // ROW_BLOCK).astype(I32).reshape(1)
    return dest, row_tok.reshape(nb, 1, ROW_BLOCK), block_e, nact


def _dest_tiles(dest, tm):
    n = dest.shape[0]
    return dest.reshape(n // tm, tm, TOP_K).transpose(0, 2, 1).reshape(n // tm, 1, tm * TOP_K)


def kernel(x_prompt, x_sample, state_pool, state_conv, state_delta, c_prompt, c_sample, norm_mix, norm_ffn, w_ada, b_ada, w_in, pool_w, pool_scale, conv_w, a_log, dt_bias, dn_norm, w_out, router_w, router_bias, exp_gate, exp_up, exp_down, sh_gate, sh_up, sh_down, norm_final):
    bp, tp, d = x_prompt.shape
    bs, ts, _ = x_sample.shape
    depth = w_ada.shape[0]
    past_len = PAST_LEN
    n_p, n_s = bp * tp, bs * ts
    n_tok = n_p + n_s
    n_pad = -(-n_tok // ROUTER_TM) * ROUTER_TM
    tm_p = min(512, tp)
    tm_c = min(128, tp)
    assert tp % tm_p == 0 and ts == 1 and bs % 8 == 0

    c_all = jnp.concatenate([c_prompt, c_sample], axis=0)
    m_rows = -(-c_all.shape[0] // 8) * 8
    mod_all = _ada(jnp.pad(c_all, ((0, m_rows - c_all.shape[0]), (0, 0))), w_ada, b_ada)

    c2 = POOL_W + CONV_CH + DN_W
    lane_pad = jnp.zeros((depth, d, LANES - DN_HEADS), F32)
    w_in_p = jnp.concatenate([w_in[:, :, :c2], w_in[:, :, c2:c2 + DN_HEADS], lane_pad,
                              w_in[:, :, c2 + DN_HEADS:], lane_pad], axis=-1).astype(BF16)
    head_pad = lambda a: jnp.pad(a, ((0, 0), (0, LANES - DN_HEADS)))
    par_all = jnp.stack([head_pad(a_log), head_pad(dt_bias)] + [jnp.zeros((depth, LANES), F32)] * 6, axis=1)
    conv_w8 = jnp.pad(conv_w, ((0, 0), (0, 8 - CONV_W), (0, 0)))

    xp = x_prompt.reshape(n_p, d)
    xs = x_sample.reshape(n_s, d)
    zero_pool = jnp.zeros((bp, POOL_BUF, POOL_W), F32)
    zero_conv = jnp.zeros((bp, CONV_W - 1, CONV_CH), F32)
    zero_state = jnp.zeros((bp, DN_HEADS, DN_D, DN_D), F32)
    outs = {k: [] for k in ("pool_p", "conv_p", "delta_p", "pool_s", "conv_s", "delta_s")}
    yp = ys = None
    for l in range(depth):
        lw = {"pool_w": pool_w[l].astype(BF16), "pool_scale": pool_scale[l].reshape(1, POOL_W),
              "conv_w": conv_w8[l], "par": par_all[l], "dn_norm": dn_norm[l].reshape(1, DN_D)}
        mod_p = mod_all[l, :bp].reshape(bp, 1, 6 * d)
        mod_s = mod_all[l, bp:bp + bs].reshape(1, bs, 6 * d)
        nw_mix = norm_mix[l].reshape(1, d)
        nw_ffn = norm_ffn[l].reshape(1, d)
        w_out_b = w_out[l].astype(BF16)

        up, qkvp, zp, bap = _norm_proj(xp, mod_p, nw_mix, w_in_p[l], tm_p, tp // tm_p)
        us, qkvs, zs, bas = _norm_proj(xs, mod_s, nw_mix, w_in_p[l], n_s, 1)
        pool_p, dn_p, npool_p, nconv_p, ns_p = _mixers(up, qkvp, zp, bap, bp, tp, zero_pool, zero_conv,
                                                       zero_state, 0, lw)
        pool_s, dn_s, npool_s, nconv_s, ns_s = _mixers(us, qkvs, zs, bas, bs, ts, state_pool[l], state_conv[l],
                                                       state_delta[l], past_len, lw)
        for key, val in (("pool_p", npool_p), ("conv_p", nconv_p), ("delta_p", ns_p),
                         ("pool_s", npool_s), ("conv_s", nconv_s), ("delta_s", ns_s)):
            outs[key].append(val)

        x1p, h2p = _out_proj(pool_p, dn_p, xp, mod_p, nw_ffn, w_out_b, tm_p, tp // tm_p)
        x1s, h2s = _out_proj(pool_s, dn_s, xs, mod_s, nw_ffn, w_out_b, n_s, 1)

        h_all = jnp.concatenate([h2p, h2s, jnp.zeros((n_pad - n_tok, d), F32)], axis=0)
        eidx, wts, rank, counts = _router(h_all, router_w[l].astype(BF16), router_bias[l].reshape(1, N_EXPERTS),
                                          n_tok)
        eidx = eidx[:n_tok, :TOP_K]
        rank = rank[:n_tok, :TOP_K]
        dest, row_tok3, block_e, nact = _dispatch_tables(eidx, rank, counts.reshape(-1).astype(I32), n_tok)
        y_sorted = _experts(block_e, nact, row_tok3, h_all, exp_gate[l], exp_up[l], exp_down[l])

        shg, shu, shd = sh_gate[l].astype(BF16), sh_up[l].astype(BF16), sh_down[l].astype(BF16)
        nf = norm_final.reshape(1, d)
        xp, yp = _combine(_dest_tiles(dest[:n_p], tm_c), y_sorted, wts[:n_p], x1p, h2p, mod_p, shg, shu, shd,
                          nf, tm_c, tp // tm_c)
        xs, ys = _combine(_dest_tiles(dest[n_p:], n_s), y_sorted, wts[n_p:n_tok], x1s, h2s, mod_s, shg, shu, shd,
                          nf, n_s, 1)

    stack = lambda key: jnp.stack(outs[key])
    return (yp.reshape(bp, tp, d), ys.reshape(bs, ts, d), stack("pool_p"), stack("conv_p"), stack("delta_p"),
            stack("pool_s"), stack("conv_s"), stack("delta_s"))
```

```python
import functools

import jax
import jax.numpy as jnp
from jax import lax
from jax.experimental import pallas as pl
from jax.experimental.pallas import tpu as pltpu

F32, BF16, I32 = jnp.float32, jnp.bfloat16, jnp.int32
HI = lax.Precision.HIGHEST
EPS = 1e-6

LANES = 128
POOL_WINDOWS = (2, 4, 8, 16)
POOL_GW = 128
POOL_W = 512
POOL_BUF = 15
POOL_HALO = 16
DN_HEADS = 4
DN_D = 128
DN_W = 512
CONV_W = 4
CONV_CH = 3 * DN_W
CONV_HALO = 8
CHUNK = 64
INV_BLOCK = 16
N_EXPERTS = 256
TOP_K = 8
N_GROUPS = 8
TOPK_GROUPS = 4
GROUP_SHIFT = 5
ROUTED_SCALE = 2.5
PAST_LEN = 16384
ROW_BLOCK = 128
BA_W = 2 * LANES
D_IN_PAD = POOL_W + CONV_CH + DN_W + BA_W
VMEM_LIMIT = 48 * 2 ** 20


def _cparams(n_axes):
    return pltpu.CompilerParams(dimension_semantics=("arbitrary",) * n_axes,
                                vmem_limit_bytes=VMEM_LIMIT)


def _sigmoid(x):
    return 1.0 / (1.0 + jnp.exp(-x))


def _silu(x):
    return x * _sigmoid(x)


def _softplus(x):
    return jnp.maximum(x, 0.0) + jnp.log(1.0 + jnp.exp(-jnp.abs(x)))


def _rms(x, w):
    return x * lax.rsqrt(jnp.mean(x * x, axis=-1, keepdims=True) + EPS) * w


def _dot(a, b, precision=None):
    return jnp.dot(a, b, preferred_element_type=F32, precision=precision)


def _dot_nt(a, b, precision=None):
    return lax.dot_general(a, b, (((1,), (1,)), ((), ())), preferred_element_type=F32,
                           precision=precision)


def _ada_kernel(c_ref, w_ref, b_ref, o_ref):
    o_ref[0] = _dot(_silu(c_ref[...]), w_ref[0], HI) + b_ref[0]


def _ada(c_all, w_ada, b_ada):
    depth, d, d6 = w_ada.shape
    m = c_all.shape[0]
    tn = 1024
    return pl.pallas_call(
        _ada_kernel, grid=(depth, d6 // tn),
        in_specs=[pl.BlockSpec((m, d), lambda l, j: (0, 0)),
                  pl.BlockSpec((1, d, tn), lambda l, j: (l, 0, j)),
                  pl.BlockSpec((1, 1, tn), lambda l, j: (l, 0, j))],
        out_specs=pl.BlockSpec((1, m, tn), lambda l, j: (l, 0, j)),
        out_shape=jax.ShapeDtypeStruct((depth, m, d6), F32),
        compiler_params=_cparams(2), name="ada")(c_all, w_ada, b_ada.reshape(depth, 1, d6))


def _mod_spec(rows, d, tiles_per_batch, chunk):
    return pl.BlockSpec((1, rows, d), lambda i: (i // tiles_per_batch, 0, chunk))


def _norm_proj_kernel(x_ref, sh_ref, sc_ref, nw_ref, w_ref, u_ref, qkv_ref, z_ref, ba_ref):
    h = _rms(x_ref[...], nw_ref[...]) * (1.0 + sc_ref[0]) + sh_ref[0]
    hb = h.astype(BF16)
    c0, c1, c2 = POOL_W, POOL_W + CONV_CH, POOL_W + CONV_CH + DN_W
    u_ref[...] = _dot(hb, w_ref[:, :c0])
    qkv_ref[...] = _dot(hb, w_ref[:, c0:c1])
    z_ref[...] = _dot(hb, w_ref[:, c1:c2])
    ba_ref[...] = _dot(hb, w_ref[:, c2:])


def _norm_proj(x2d, mod3, nw, w_in_p, tm, tiles_per_batch):
    n, d = x2d.shape
    rows = mod3.shape[1]
    widths = (POOL_W, CONV_CH, DN_W, BA_W)
    return pl.pallas_call(
        _norm_proj_kernel, grid=(n // tm,),
        in_specs=[pl.BlockSpec((tm, d), lambda i: (i, 0)),
                  _mod_spec(rows, d, tiles_per_batch, 0),
                  _mod_spec(rows, d, tiles_per_batch, 1),
                  pl.BlockSpec((1, d), lambda i: (0, 0)),
                  pl.BlockSpec((d, D_IN_PAD), lambda i: (0, 0))],
        out_specs=[pl.BlockSpec((tm, w), lambda i: (i, 0)) for w in widths],
        out_shape=[jax.ShapeDtypeStruct((n, w), F32) for w in widths],
        compiler_params=_cparams(1), name="norm_proj")(x2d, mod3, mod3, nw, w_in_p)


def _pool_kernel(u_ref, buf_ref, pw_ref, ps_ref, o_ref, ext, *, tt, start_pos):
    t = pl.program_id(1)

    @pl.when(t == 0)
    def _():
        ext[0:POOL_HALO, :] = buf_ref[0]

    @pl.when(t > 0)
    def _():
        ext[0:POOL_HALO, :] = ext[tt:tt + POOL_HALO, :]

    u = u_ref[0]
    ext[POOL_HALO:POOL_HALO + tt, :] = u
    pos = start_pos + t * tt + lax.broadcasted_iota(I32, (tt, 1), 0)
    for g, win in enumerate(POOL_WINDOWS):
        sl = slice(g * POOL_GW, (g + 1) * POOL_GW)
        s = u[:, sl]
        for j in range(1, win):
            s = s + ext[POOL_HALO - j:POOL_HALO - j + tt, sl]
        cnt = jnp.minimum(pos + 1, win).astype(F32)
        pooled = s / cnt - u[:, sl]
        o_ref[0, :, sl] = _dot(pooled.astype(BF16), pw_ref[g]) * ps_ref[:, sl]


def _pool(u3, buf16, pool_w_b, pool_scale, start_pos):
    b, t, w = u3.shape
    tt = min(t, 256)
    assert t % tt == 0 and tt % 8 == 0
    return pl.pallas_call(
        functools.partial(_pool_kernel, tt=tt, start_pos=start_pos), grid=(b, t // tt),
        in_specs=[pl.BlockSpec((1, tt, w), lambda i, j: (i, j, 0)),
                  pl.BlockSpec((1, POOL_HALO, w), lambda i, j: (i, 0, 0)),
                  pl.BlockSpec((len(POOL_WINDOWS), POOL_GW, POOL_GW), lambda i, j: (0, 0, 0)),
                  pl.BlockSpec((1, w), lambda i, j: (0, 0))],
        out_specs=pl.BlockSpec((1, tt, w), lambda i, j: (i, j, 0)),
        out_shape=jax.ShapeDtypeStruct((b, t, w), F32),
        scratch_shapes=[pltpu.VMEM((POOL_HALO + tt, w), F32)],
        compiler_params=_cparams(2), name="pool")(u3, buf16, pool_w_b, pool_scale)


def _bf(x):
    return x.astype(BF16)


def _dot3(a, b):
    a_hi, b_hi = _bf(a), _bf(b)
    a_lo, b_lo = _bf(a - a_hi.astype(F32)), _bf(b - b_hi.astype(F32))
    return _dot(a_hi, b_hi) + (_dot(a_hi, b_lo) + _dot(a_lo, b_hi))


def _unit_lower_inverse(a, row, col, eye):
    n = -a
    same = (row // INV_BLOCK) == (col // INV_BLOCK)
    dm = jnp.where(same, n, 0.0)
    lm = n - dm
    d1 = _bf(dm)
    d2 = _dot(d1, d1)
    d2b = _bf(d2)
    d4 = _dot(d2b, d2b)
    d4b = _bf(d4)
    d8 = _dot(d4b, d4b)
    p = _dot(_bf(eye + dm), _bf(eye + d2))
    p2 = _dot(_bf(eye + d4), _bf(eye + d8))
    pb = _bf(_dot(_bf(p), _bf(p2)))
    m = _dot(pb, _bf(lm))
    mb = _bf(m)
    m2 = _dot(mb, mb)
    q = _dot(_bf(eye + m), _bf(eye + m2))
    t0 = _dot(_bf(q), pb)
    resid = (eye - t0) - _dot3(a, t0)
    return t0 + _dot(_bf(t0), _bf(resid))


def _gdn_kernel(qkv_ref, z_ref, ba_ref, cbuf_ref, s0_ref, cw_ref, par_ref, nw_ref,
                o_ref, sfin_ref, state, xext, *, t_len):
    c = pl.program_id(1)

    @pl.when(c == 0)
    def _():
        state[...] = s0_ref[0]
        xext[0:CONV_HALO, :] = cbuf_ref[0]

    @pl.when(c > 0)
    def _():
        xext[0:CONV_HALO, :] = xext[CHUNK:CHUNK + CONV_HALO, :]

    xext[CONV_HALO:CONV_HALO + CHUNK, :] = qkv_ref[0]
    base = CONV_HALO - (CONV_W - 1)
    acc = xext[base:base + CHUNK, :] * cw_ref[0:1, :]
    for j in range(1, CONV_W):
        acc = acc + xext[base + j:base + j + CHUNK, :] * cw_ref[j:j + 1, :]
    qkvc = _silu(acc)

    valid = (c * CHUNK + lax.broadcasted_iota(I32, (CHUNK, 1), 0)) < t_len
    vf = valid.astype(F32)
    ba = ba_ref[0]
    head_lane = lax.broadcasted_iota(I32, (1, LANES), 1) < DN_HEADS
    beta_all = _sigmoid(ba[:, :LANES]) * vf
    decay = jnp.where(head_lane, -jnp.exp(par_ref[0:1, :]), 0.0)
    g_all = decay * _softplus(ba[:, LANES:] + par_ref[1:2, :]) * vf

    r64 = lax.broadcasted_iota(I32, (CHUNK, CHUNK), 0)
    c64 = lax.broadcasted_iota(I32, (CHUNK, CHUNK), 1)
    gam_all = _dot((r64 >= c64).astype(F32), g_all, HI)
    pick = (lax.broadcasted_iota(I32, (8, LANES), 0) == lax.broadcasted_iota(I32, (8, LANES), 1)).astype(F32)
    gam_rows = _dot_nt(pick, gam_all, HI)

    heads = range(DN_HEADS)
    qs, ks, vs = [], [], []
    for h in heads:
        q = qkvc[:, h * DN_D:(h + 1) * DN_D]
        k = qkvc[:, DN_W + h * DN_D:DN_W + (h + 1) * DN_D]
        qs.append(q * lax.rsqrt(jnp.sum(q * q, axis=-1, keepdims=True) + EPS) * (DN_D ** -0.5) * vf)
        ks.append(k * lax.rsqrt(jnp.sum(k * k, axis=-1, keepdims=True) + EPS) * vf)
        vs.append(qkvc[:, 2 * DN_W + h * DN_D:2 * DN_W + (h + 1) * DN_D] * vf)
    q_s = jnp.concatenate(qs, axis=0)
    k_s = jnp.concatenate(ks, axis=0)
    v_s = jnp.concatenate(vs, axis=0)
    beta = jnp.concatenate([beta_all[:, h:h + 1] for h in heads], axis=0)
    gam = jnp.concatenate([gam_all[:, h:h + 1] for h in heads], axis=0)
    gam_r = jnp.concatenate([gam_rows[h:h + 1, :] for h in heads], axis=1)

    hc = DN_HEADS * CHUNK
    row = lax.broadcasted_iota(I32, (hc, hc), 0)
    col = lax.broadcasted_iota(I32, (hc, hc), 1)
    incl = ((row // CHUNK) == (col // CHUNK)) & (row >= col)
    eye = (row == col).astype(F32)
    dec = jnp.where(incl, jnp.exp(jnp.where(incl, gam - gam_r, 0.0)), 0.0)
    kb = _bf(k_s)
    qb = _bf(q_s)
    a_mat = beta * _dot_nt(kb, kb) * jnp.where(row > col, dec, 0.0)
    qk = _dot_nt(qb, kb) * dec
    egam = jnp.exp(gam)
    rhs = jnp.concatenate([(beta * egam) * k_s, beta * v_s], axis=-1)
    sol = _dot3(_unit_lower_inverse(a_mat, row, col, eye), rhs)
    wb = _bf(sol[:, :DN_D])
    uv = sol[:, DN_D:]

    s_f, s_b, us = [], [], []
    for h in heads:
        rows = slice(h * CHUNK, (h + 1) * CHUNK)
        s_f.append(state[h])
        s_b.append(_bf(s_f[h]))
        us.append(uv[rows] - _dot(wb[rows], s_b[h]))
    ub = _bf(jnp.concatenate(us, axis=0))
    o_intra = _dot(_bf(qk), ub)
    for h in heads:
        rows = slice(h * CHUNK, (h + 1) * CHUNK)
        o = egam[rows] * _dot(qb[rows], s_b[h]) + o_intra[rows]
        g_last = gam[(h + 1) * CHUNK - 1:(h + 1) * CHUNK, :]
        k_dec = k_s[rows] * jnp.exp(g_last - gam[rows])
        state[h] = jnp.exp(g_last) * s_f[h] + _dot(_bf(k_dec.T), ub[rows])
        sl = slice(h * DN_D, (h + 1) * DN_D)
        o_ref[0, :, sl] = _rms(o, nw_ref[...]) * _silu(z_ref[0, :, sl])

    @pl.when(c == pl.num_programs(1) - 1)
    def _():
        sfin_ref[0] = state[...]


def _gdn(qkv3, z3, ba3, cbuf8, s0, conv_w8, par, dn_norm, t_len):
    b, tp, _ = qkv3.shape
    assert tp % CHUNK == 0
    kern = functools.partial(_gdn_kernel, t_len=t_len)
    return pl.pallas_call(
        kern, grid=(b, tp // CHUNK),
        in_specs=[pl.BlockSpec((1, CHUNK, CONV_CH), lambda i, j: (i, j, 0)),
                  pl.BlockSpec((1, CHUNK, DN_W), lambda i, j: (i, j, 0)),
                  pl.BlockSpec((1, CHUNK, BA_W), lambda i, j: (i, j, 0)),
                  pl.BlockSpec((1, CONV_HALO, CONV_CH), lambda i, j: (i, 0, 0)),
                  pl.BlockSpec((1, DN_HEADS, DN_D, DN_D), lambda i, j: (i, 0, 0, 0)),
                  pl.BlockSpec((8, CONV_CH), lambda i, j: (0, 0)),
                  pl.BlockSpec((8, LANES), lambda i, j: (0, 0)),
                  pl.BlockSpec((1, DN_D), lambda i, j: (0, 0))],
        out_specs=[pl.BlockSpec((1, CHUNK, DN_W), lambda i, j: (i, j, 0)),
                   pl.BlockSpec((1, DN_HEADS, DN_D, DN_D), lambda i, j: (i, 0, 0, 0))],
        out_shape=[jax.ShapeDtypeStruct((b, tp, DN_W), F32),
                   jax.ShapeDtypeStruct((b, DN_HEADS, DN_D, DN_D), F32)],
        scratch_shapes=[pltpu.VMEM((DN_HEADS, DN_D, DN_D), F32),
                        pltpu.VMEM((CONV_HALO + CHUNK, CONV_CH), F32)],
        compiler_params=_cparams(2), name="gdn")(qkv3, z3, ba3, cbuf8, s0, conv_w8, par, dn_norm)


def _out_proj_kernel(pool_ref, dn_ref, x_ref, gt_ref, sh_ref, sc_ref, nw_ref, w_ref, x1_ref, h2_ref):
    mix = _dot(pool_ref[...].astype(BF16), w_ref[:POOL_W, :]) + _dot(dn_ref[...].astype(BF16), w_ref[POOL_W:, :])
    x1 = x_ref[...] + gt_ref[0] * mix
    x1_ref[...] = x1
    h2_ref[...] = _rms(x1, nw_ref[...]) * (1.0 + sc_ref[0]) + sh_ref[0]


def _out_proj(pool2, dn2, x2d, mod3, nw, w_out_b, tm, tiles_per_batch):
    n, d = x2d.shape
    rows = mod3.shape[1]
    tok = lambda w: pl.BlockSpec((tm, w), lambda i: (i, 0))
    return pl.pallas_call(
        _out_proj_kernel, grid=(n // tm,),
        in_specs=[tok(POOL_W), tok(DN_W), tok(d),
                  _mod_spec(rows, d, tiles_per_batch, 2),
                  _mod_spec(rows, d, tiles_per_batch, 3),
                  _mod_spec(rows, d, tiles_per_batch, 4),
                  pl.BlockSpec((1, d), lambda i: (0, 0)),
                  pl.BlockSpec((POOL_W + DN_W, d), lambda i: (0, 0))],
        out_specs=[tok(d), tok(d)],
        out_shape=[jax.ShapeDtypeStruct((n, d), F32)] * 2,
        compiler_params=_cparams(1), name="out_proj")(pool2, dn2, x2d, mod3, mod3, mod3, nw, w_out_b)


def _router_kernel(h_ref, rw_ref, rb_ref, eidx_ref, wts_ref, rank_ref, cnt_ref, run, *, tm):
    i = pl.program_id(0)

    @pl.when(i == 0)
    def _():
        run[...] = jnp.zeros_like(run)

    s = _sigmoid(_dot(h_ref[...].astype(BF16), rw_ref[...]))
    sel = s + rb_ref[...]
    lane_i = lax.broadcasted_iota(I32, (tm, N_EXPERTS), 1)
    lane = lane_i.astype(F32)
    grp = lane_i >> GROUP_SHIFT
    neg = -jnp.inf

    def first_max(x):
        m = jnp.max(x, axis=-1, keepdims=True)
        idx = jnp.min(jnp.where(x == m, lane, float(N_EXPERTS)), axis=-1, keepdims=True)
        return m, idx

    gs = []
    for g in range(N_GROUPS):
        xg = jnp.where(grp == g, sel, neg)
        m1, i1 = first_max(xg)
        m2 = jnp.max(jnp.where(lane == i1, neg, xg), axis=-1, keepdims=True)
        gs.append(m1 + m2)
    cur = jnp.full((tm, N_EXPERTS), neg, F32)
    for g in range(N_GROUPS):
        ahead = jnp.zeros((tm, 1), F32)
        for o in range(N_GROUPS):
            if o < g:
                ahead = ahead + jnp.where(gs[o] >= gs[g], 1.0, 0.0)
            elif o > g:
                ahead = ahead + jnp.where(gs[o] > gs[g], 1.0, 0.0)
        cur = jnp.where((grp == g) & (ahead < TOPK_GROUPS), sel, cur)

    hits, idxs, ws = [], [], []
    chosen = jnp.zeros((tm, N_EXPERTS), F32)
    for _ in range(TOP_K):
        _, idx = first_max(cur)
        hit = lane == idx
        ws.append(jnp.sum(jnp.where(hit, s, 0.0), axis=-1, keepdims=True))
        cur = jnp.where(hit, neg, cur)
        chosen = jnp.where(hit, 1.0, chosen)
        hits.append(hit)
        idxs.append(idx)
    onehot = chosen
    r = lax.broadcasted_iota(I32, (tm, tm), 0)
    c = lax.broadcasted_iota(I32, (tm, tm), 1)
    earlier = jnp.where(r > c, 1.0, 0.0).astype(BF16)
    before = _dot(earlier, onehot.astype(BF16)) + run[...]
    run[...] = run[...] + jnp.sum(onehot, axis=0, keepdims=True)

    wsum = ws[0]
    for w in ws[1:]:
        wsum = wsum + w
    out_lane = lax.broadcasted_iota(I32, (tm, LANES), 1)
    eidx = jnp.zeros((tm, LANES), F32)
    wts = jnp.zeros((tm, LANES), F32)
    rank = jnp.zeros((tm, LANES), F32)
    for k in range(TOP_K):
        rk = jnp.sum(jnp.where(hits[k], before, 0.0), axis=-1, keepdims=True)
        eidx = jnp.where(out_lane == k, idxs[k], eidx)
        wts = jnp.where(out_lane == k, ws[k] / wsum * ROUTED_SCALE, wts)
        rank = jnp.where(out_lane == k, rk, rank)
    eidx_ref[...] = eidx.astype(I32)
    wts_ref[...] = wts
    rank_ref[...] = rank.astype(I32)

    @pl.when(i == pl.num_programs(0) - 1)
    def _():
        cnt_ref[...] = run[...]


def _router(h_all, router_w_b, router_bias, tm):
    n_pad, d = h_all.shape
    kern = functools.partial(_router_kernel, tm=tm)
    tok = pl.BlockSpec((tm, LANES), lambda i: (i, 0))
    return pl.pallas_call(
        kern, grid=(n_pad // tm,),
        in_specs=[pl.BlockSpec((tm, d), lambda i: (i, 0)),
                  pl.BlockSpec((d, N_EXPERTS), lambda i: (0, 0)),
                  pl.BlockSpec((1, N_EXPERTS), lambda i: (0, 0))],
        out_specs=[tok, tok, tok, pl.BlockSpec((1, N_EXPERTS), lambda i: (0, 0))],
        out_shape=[jax.ShapeDtypeStruct((n_pad, LANES), I32),
                   jax.ShapeDtypeStruct((n_pad, LANES), F32),
                   jax.ShapeDtypeStruct((n_pad, LANES), I32),
                   jax.ShapeDtypeStruct((1, N_EXPERTS), F32)],
        scratch_shapes=[pltpu.VMEM((1, N_EXPERTS), F32)],
        compiler_params=_cparams(1), name="router")(h_all, router_w_b, router_bias)


def _gather_rows(idx_ref, src_hbm, dst, sem, n_rows):
    def body(r, carry):
        tok = idx_ref[0, 0, r]
        pltpu.make_async_copy(src_hbm.at[pl.ds(tok, 1)], dst.at[pl.ds(r, 1)], sem).start()
        return carry
    lax.fori_loop(0, n_rows, body, 0, unroll=8)


def _wait_rows(src_hbm, dst, sem, n_rows):
    pltpu.make_async_copy(src_hbm.at[pl.ds(0, n_rows)], dst, sem).wait()


def _dest_kernel(eidx_ref, rank_ref, ps_ref, dest_ref, *, tm):
    lane = lax.broadcasted_iota(I32, (tm, N_EXPERTS), 1)
    out_lane = lax.broadcasted_iota(I32, (tm, LANES), 1)
    eidx = eidx_ref[...]
    rank = rank_ref[...].astype(F32)
    dest = jnp.zeros((tm, LANES), F32)
    for k in range(TOP_K):
        start = jnp.sum(jnp.where(lane == eidx[:, k:k + 1], ps_ref[...], 0.0), axis=-1, keepdims=True)
        dest = jnp.where(out_lane == k, start + rank[:, k:k + 1], dest)
    dest_ref[...] = dest.astype(I32)


def _dest(eidx, rank, pstarts, tm):
    n = eidx.shape[0]
    tok = pl.BlockSpec((tm, LANES), lambda i: (i, 0))
    return pl.pallas_call(
        functools.partial(_dest_kernel, tm=tm), grid=(n // tm,),
        in_specs=[tok, tok, pl.BlockSpec((1, N_EXPERTS), lambda i: (0, 0))],
        out_specs=tok, out_shape=jax.ShapeDtypeStruct((n, LANES), I32),
        compiler_params=_cparams(1), name="dest")(eidx, rank, pstarts)


def _dispatch_kernel(dest_ref, h_ref, xs_in, xs_hbm, sem, *, tm):
    del xs_in

    def body(r, carry):
        for k in range(TOP_K):
            row = dest_ref[0, 0, r * TOP_K + k]
            pltpu.make_async_copy(h_ref.at[pl.ds(r, 1)], xs_hbm.at[pl.ds(row, 1)], sem).start()
        return carry
    lax.fori_loop(0, tm, body, 0, unroll=2)
    for k in range(TOP_K):
        pltpu.make_async_copy(h_ref, xs_hbm.at[pl.ds(0, tm)], sem).wait()


def _dispatch(dest3, h_all, n_rows, tm):
    n, d = h_all.shape
    return pl.pallas_call(
        functools.partial(_dispatch_kernel, tm=tm), grid=(n // tm,),
        in_specs=[pl.BlockSpec((1, 1, tm * TOP_K), lambda i: (i, 0, 0), memory_space=pltpu.SMEM),
                  pl.BlockSpec((tm, d), lambda i: (i, 0)),
                  pl.BlockSpec(memory_space=pl.ANY)],
        out_specs=pl.BlockSpec(memory_space=pl.ANY),
        out_shape=jax.ShapeDtypeStruct((n_rows, d), F32),
        scratch_shapes=[pltpu.SemaphoreType.DMA(())],
        input_output_aliases={2: 0},
        compiler_params=_cparams(1), name="dispatch")(dest3, h_all, jnp.zeros((n_rows, d), F32))


def _experts_kernel(be_ref, nact_ref, x_ref, g_ref, u_ref, d_ref, y_ref, gb, ub, db):
    i = pl.program_id(0)
    nact = nact_ref[0]
    changed = jnp.logical_or(i == 0, be_ref[i] != be_ref[jnp.maximum(i - 1, 0)])

    @pl.when(jnp.logical_and(i < nact, changed))
    def _():
        gb[...] = g_ref[0].astype(BF16)
        ub[...] = u_ref[0].astype(BF16)
        db[...] = d_ref[0].astype(BF16)

    @pl.when(i < nact)
    def _():
        x = x_ref[...].astype(BF16)
        hid = _silu(_dot(x, gb[...])) * _dot(x, ub[...])
        y_ref[...] = _dot(hid.astype(BF16), db[...])

    @pl.when(i >= nact)
    def _():
        y_ref[...] = jnp.zeros_like(y_ref)


def _experts(block_e, nact, x_sorted, exp_gate, exp_up, exp_down, layer):
    nb = x_sorted.shape[0] // ROW_BLOCK
    _, d, de = exp_gate.shape
    base = layer * N_EXPERTS
    grid_spec = pltpu.PrefetchScalarGridSpec(
        num_scalar_prefetch=2, grid=(nb,),
        in_specs=[pl.BlockSpec((ROW_BLOCK, d), lambda i, be, na: (i, 0)),
                  pl.BlockSpec((1, d, de), lambda i, be, na: (base + be[i], 0, 0)),
                  pl.BlockSpec((1, d, de), lambda i, be, na: (base + be[i], 0, 0)),
                  pl.BlockSpec((1, de, d), lambda i, be, na: (base + be[i], 0, 0))],
        out_specs=pl.BlockSpec((ROW_BLOCK, d), lambda i, be, na: (i, 0)),
        scratch_shapes=[pltpu.VMEM((d, de), BF16), pltpu.VMEM((d, de), BF16), pltpu.VMEM((de, d), BF16)])
    return pl.pallas_call(
        _experts_kernel, grid_spec=grid_spec,
        out_shape=jax.ShapeDtypeStruct((nb * ROW_BLOCK, d), F32),
        compiler_params=_cparams(1), name="experts")(block_e, nact, x_sorted, exp_gate, exp_up, exp_down)


def _combine_kernel(cur_ref, nxt_ref, y_hbm, wts_ref, x1_ref, h_ref, gt_ref, sg_ref, su_ref, sd_ref,
                    nf_ref, x2_ref, yn_ref, ybuf, sems, *, tm):
    i = pl.program_id(0)
    n = pl.num_programs(0)
    slot = i % 2
    rows = tm * TOP_K

    @pl.when(i == 0)
    def _():
        _gather_rows(cur_ref, y_hbm, ybuf.at[0], sems.at[0], rows)

    @pl.when(i + 1 < n)
    def _():
        _gather_rows(nxt_ref, y_hbm, ybuf.at[1 - slot], sems.at[1 - slot], rows)

    hb = h_ref[...].astype(BF16)
    hid = _silu(_dot(hb, sg_ref[...])) * _dot(hb, su_ref[...])
    acc = _dot(hid.astype(BF16), sd_ref[...])
    _wait_rows(y_hbm, ybuf.at[slot], sems.at[slot], rows)
    wts = wts_ref[...]
    for k in range(TOP_K):
        acc = acc + ybuf[slot, k * tm:(k + 1) * tm, :] * wts[:, k:k + 1]
    x2 = x1_ref[...] + gt_ref[0] * acc
    x2_ref[...] = x2
    yn_ref[...] = _rms(x2, nf_ref[...])


def _combine(dest3, y_sorted, wts, x1, h2, mod3, sh_gate_b, sh_up_b, sh_down_b, norm_final, tm, tiles_per_batch):
    n, d = x1.shape
    nt = n // tm
    rows = mod3.shape[1]
    ds = sh_gate_b.shape[1]
    tok = lambda w: pl.BlockSpec((tm, w), lambda i: (i, 0))
    kern = functools.partial(_combine_kernel, tm=tm)
    return pl.pallas_call(
        kern, grid=(nt,),
        in_specs=[pl.BlockSpec((1, 1, tm * TOP_K), lambda i: (i, 0, 0), memory_space=pltpu.SMEM),
                  pl.BlockSpec((1, 1, tm * TOP_K), lambda i: (jnp.minimum(i + 1, nt - 1), 0, 0),
                               memory_space=pltpu.SMEM),
                  pl.BlockSpec(memory_space=pl.ANY),
                  tok(LANES), tok(d), tok(d),
                  _mod_spec(rows, d, tiles_per_batch, 5),
                  pl.BlockSpec((d, ds), lambda i: (0, 0)),
                  pl.BlockSpec((d, ds), lambda i: (0, 0)),
                  pl.BlockSpec((ds, d), lambda i: (0, 0)),
                  pl.BlockSpec((1, d), lambda i: (0, 0))],
        out_specs=[tok(d), tok(d)],
        out_shape=[jax.ShapeDtypeStruct((n, d), F32)] * 2,
        scratch_shapes=[pltpu.VMEM((2, tm * TOP_K, d), F32), pltpu.SemaphoreType.DMA((2,))],
        compiler_params=_cparams(1), name="combine")(
            dest3, dest3, y_sorted, wts, x1, h2, mod3, sh_gate_b, sh_up_b, sh_down_b, norm_final)


def _mixers(u, qkv, z, ba, b, t, pool_buf, conv_buf, s0, start_pos, lw):
    t_pool = -(-t // 8) * 8
    t_gdn = -(-t // CHUNK) * CHUNK
    u3 = u.reshape(b, t, POOL_W)
    qkv3 = qkv.reshape(b, t, CONV_CH)
    pad_t = lambda a, tp: a if tp == t else jnp.pad(a, ((0, 0), (0, tp - t), (0, 0)))
    buf16 = jnp.pad(pool_buf, ((0, 0), (POOL_HALO - POOL_BUF, 0), (0, 0)))
    pool_out = _pool(pad_t(u3, t_pool), buf16, lw["pool_w"], lw["pool_scale"], start_pos)[:, :t]
    cbuf8 = jnp.pad(conv_buf, ((0, 0), (CONV_HALO - (CONV_W - 1), 0), (0, 0)))
    dn_out, s_new = _gdn(pad_t(qkv3, t_gdn), pad_t(z.reshape(b, t, DN_W), t_gdn),
                         pad_t(ba.reshape(b, t, BA_W), t_gdn), cbuf8, s0, lw["conv_w"], lw["par"],
                         lw["dn_norm"], t)
    new_pool = jnp.concatenate([pool_buf, u3], axis=1)[:, -POOL_BUF:]
    new_conv = jnp.concatenate([conv_buf, qkv3], axis=1)[:, -(CONV_W - 1):]
    return pool_out.reshape(b * t, POOL_W), dn_out[:, :t].reshape(b * t, DN_W), new_pool, new_conv, s_new


def _block_tables(counts, n_tok):
    nk = n_tok * TOP_K
    nb = -(-(nk + N_EXPERTS * (ROW_BLOCK - 1)) // ROW_BLOCK)
    pcounts = (counts + ROW_BLOCK - 1) // ROW_BLOCK * ROW_BLOCK
    pends = jnp.cumsum(pcounts)
    pstarts = pends - pcounts
    first_row = jnp.arange(nb, dtype=I32)[:, None] * ROW_BLOCK
    block_e = jnp.minimum(jnp.sum((pends[None, :] <= first_row).astype(I32), axis=1), N_EXPERTS - 1)
    nact = (pends[-1] // ROW_BLOCK).astype(I32).reshape(1)
    return pstarts.astype(F32).reshape(1, N_EXPERTS), block_e, nact, nb


def _dest_tiles(dest, tm, k_major):
    n = dest.shape[0]
    tiles = dest.reshape(n // tm, tm, TOP_K)
    if k_major:
        tiles = tiles.transpose(0, 2, 1)
    return tiles.reshape(n // tm, 1, tm * TOP_K)


def kernel(x_prompt, x_sample, state_pool, state_conv, state_delta, c_prompt, c_sample, norm_mix, norm_ffn, w_ada, b_ada, w_in, pool_w, pool_scale, conv_w, a_log, dt_bias, dn_norm, w_out, router_w, router_bias, exp_gate, exp_up, exp_down, sh_gate, sh_up, sh_down, norm_final):
    bp, tp, d = x_prompt.shape
    bs, ts, _ = x_sample.shape
    depth = w_ada.shape[0]
    past_len = PAST_LEN
    n_p, n_s = bp * tp, bs * ts
    n_tok = n_p + n_s
    tm_p = min(512, tp)
    tm_c = min(128, tp)
    tm_r = next(t for t in (128, 64, 32, 16, 8) if n_tok % t == 0)
    assert tp % tm_p == 0 and ts == 1 and bs % 8 == 0
    n_exp = exp_gate.shape[1]
    gate_w = exp_gate.reshape((depth * n_exp,) + exp_gate.shape[2:])
    up_w = exp_up.reshape((depth * n_exp,) + exp_up.shape[2:])
    down_w = exp_down.reshape((depth * n_exp,) + exp_down.shape[2:])

    c_all = jnp.concatenate([c_prompt, c_sample], axis=0)
    m_rows = -(-c_all.shape[0] // 8) * 8
    mod_all = _ada(jnp.pad(c_all, ((0, m_rows - c_all.shape[0]), (0, 0))), w_ada, b_ada)

    c2 = POOL_W + CONV_CH + DN_W
    lane_pad = jnp.zeros((depth, d, LANES - DN_HEADS), F32)
    w_in_p = jnp.concatenate([w_in[:, :, :c2], w_in[:, :, c2:c2 + DN_HEADS], lane_pad,
                              w_in[:, :, c2 + DN_HEADS:], lane_pad], axis=-1).astype(BF16)
    head_pad = lambda a: jnp.pad(a, ((0, 0), (0, LANES - DN_HEADS)))
    par_all = jnp.stack([head_pad(a_log), head_pad(dt_bias)] + [jnp.zeros((depth, LANES), F32)] * 6, axis=1)
    conv_w8 = jnp.pad(conv_w, ((0, 0), (0, 8 - CONV_W), (0, 0)))

    xp = x_prompt.reshape(n_p, d)
    xs = x_sample.reshape(n_s, d)
    zero_pool = jnp.zeros((bp, POOL_BUF, POOL_W), F32)
    zero_conv = jnp.zeros((bp, CONV_W - 1, CONV_CH), F32)
    zero_state = jnp.zeros((bp, DN_HEADS, DN_D, DN_D), F32)
    outs = {k: [] for k in ("pool_p", "conv_p", "delta_p", "pool_s", "conv_s", "delta_s")}
    yp = ys = None
    for l in range(depth):
        lw = {"pool_w": pool_w[l].astype(BF16), "pool_scale": pool_scale[l].reshape(1, POOL_W),
              "conv_w": conv_w8[l], "par": par_all[l], "dn_norm": dn_norm[l].reshape(1, DN_D)}
        mod_p = mod_all[l, :bp].reshape(bp, 1, 6 * d)
        mod_s = mod_all[l, bp:bp + bs].reshape(1, bs, 6 * d)
        nw_mix = norm_mix[l].reshape(1, d)
        nw_ffn = norm_ffn[l].reshape(1, d)
        w_out_b = w_out[l].astype(BF16)

        up, qkvp, zp, bap = _norm_proj(xp, mod_p, nw_mix, w_in_p[l], tm_p, tp // tm_p)
        us, qkvs, zs, bas = _norm_proj(xs, mod_s, nw_mix, w_in_p[l], n_s, 1)
        pool_p, dn_p, npool_p, nconv_p, ns_p = _mixers(up, qkvp, zp, bap, bp, tp, zero_pool, zero_conv,
                                                       zero_state, 0, lw)
        pool_s, dn_s, npool_s, nconv_s, ns_s = _mixers(us, qkvs, zs, bas, bs, ts, state_pool[l], state_conv[l],
                                                       state_delta[l], past_len, lw)
        for key, val in (("pool_p", npool_p), ("conv_p", nconv_p), ("delta_p", ns_p),
                         ("pool_s", npool_s), ("conv_s", nconv_s), ("delta_s", ns_s)):
            outs[key].append(val)

        x1p, h2p = _out_proj(pool_p, dn_p, xp, mod_p, nw_ffn, w_out_b, tm_p, tp // tm_p)
        x1s, h2s = _out_proj(pool_s, dn_s, xs, mod_s, nw_ffn, w_out_b, n_s, 1)

        h_all = jnp.concatenate([h2p, h2s], axis=0)
        eidx, wts, rank, counts = _router(h_all, router_w[l].astype(BF16), router_bias[l].reshape(1, N_EXPERTS),
                                          tm_r)
        pstarts, block_e, nact, nb = _block_tables(counts.reshape(-1).astype(I32), n_tok)
        dest = _dest(eidx, rank, pstarts, tm_r)[:, :TOP_K]
        x_sorted = _dispatch(_dest_tiles(dest, tm_r, False), h_all, nb * ROW_BLOCK, tm_r)
        y_sorted = _experts(block_e, nact, x_sorted, gate_w, up_w, down_w, l)

        shg, shu, shd = sh_gate[l].astype(BF16), sh_up[l].astype(BF16), sh_down[l].astype(BF16)
        nf = norm_final.reshape(1, d)
        xp, yp = _combine(_dest_tiles(dest[:n_p], tm_c, True), y_sorted, wts[:n_p], x1p, h2p, mod_p, shg, shu, shd,
                          nf, tm_c, tp // tm_c)
        xs, ys = _combine(_dest_tiles(dest[n_p:], n_s, True), y_sorted, wts[n_p:], x1s, h2s, mod_s, shg, shu, shd,
                          nf, n_s, 1)

    stack = lambda key: jnp.stack(outs[key])
    return (yp.reshape(bp, tp, d), ys.reshape(bs, ts, d), stack("pool_p"), stack("conv_p"), stack("delta_p"),
            stack("pool_s"), stack("conv_s"), stack("delta_s"))
```

```python
import functools

import jax
import jax.numpy as jnp
from jax import lax
from jax.experimental import pallas as pl
from jax.experimental.pallas import tpu as pltpu

F32, BF16, I32 = jnp.float32, jnp.bfloat16, jnp.int32
HI = lax.Precision.HIGHEST
EPS = 1e-6

LANES = 128
POOL_WINDOWS = (2, 4, 8, 16)
POOL_GW = 128
POOL_W = 512
POOL_BUF = 15
POOL_HALO = 16
DN_HEADS = 4
DN_D = 128
DN_W = 512
CONV_W = 4
CONV_CH = 3 * DN_W
CONV_HALO = 8
CHUNK = 64
INV_BLOCK = 16
GDN_ROWS = 2
N_EXPERTS = 256
TOP_K = 8
N_GROUPS = 8
TOPK_GROUPS = 4
GROUP_SHIFT = 5
ROUTED_SCALE = 2.5
PAST_LEN = 16384
ROW_BLOCK = 256
BA_W = 2 * LANES
D_IN_PAD = POOL_W + CONV_CH + DN_W + BA_W
VMEM_LIMIT = 48 * 2 ** 20


def _cparams(n_axes):
    return pltpu.CompilerParams(dimension_semantics=("arbitrary",) * n_axes,
                                vmem_limit_bytes=VMEM_LIMIT)


def _sigmoid(x):
    return 1.0 / (1.0 + jnp.exp(-x))


def _silu(x):
    return x * _sigmoid(x)


def _softplus(x):
    return jnp.maximum(x, 0.0) + jnp.log(1.0 + jnp.exp(-jnp.abs(x)))


def _rms(x, w):
    return x * lax.rsqrt(jnp.mean(x * x, axis=-1, keepdims=True) + EPS) * w


def _dot(a, b, precision=None):
    return jnp.dot(a, b, preferred_element_type=F32, precision=precision)


def _dot_nt(a, b, precision=None):
    return lax.dot_general(a, b, (((1,), (1,)), ((), ())), preferred_element_type=F32,
                           precision=precision)


def _ada_kernel(c_ref, w_ref, b_ref, o_ref):
    o_ref[0] = _dot(_silu(c_ref[...]), w_ref[0], HI) + b_ref[0]


def _ada(c_all, w_ada, b_ada):
    depth, d, d6 = w_ada.shape
    m = c_all.shape[0]
    tn = 1024
    return pl.pallas_call(
        _ada_kernel, grid=(depth, d6 // tn),
        in_specs=[pl.BlockSpec((m, d), lambda l, j: (0, 0)),
                  pl.BlockSpec((1, d, tn), lambda l, j: (l, 0, j)),
                  pl.BlockSpec((1, 1, tn), lambda l, j: (l, 0, j))],
        out_specs=pl.BlockSpec((1, m, tn), lambda l, j: (l, 0, j)),
        out_shape=jax.ShapeDtypeStruct((depth, m, d6), F32),
        compiler_params=_cparams(2), name="ada")(c_all, w_ada, b_ada.reshape(depth, 1, d6))


def _mod_spec(rows, d, tiles_per_batch, chunk):
    return pl.BlockSpec((1, rows, d), lambda i: (i // tiles_per_batch, 0, chunk))


def _norm_proj_kernel(x_ref, sh_ref, sc_ref, nw_ref, w_ref, u_ref, qkv_ref, z_ref, ba_ref):
    h = _rms(x_ref[...], nw_ref[...]) * (1.0 + sc_ref[0]) + sh_ref[0]
    hb = h.astype(BF16)
    c0, c1, c2 = POOL_W, POOL_W + CONV_CH, POOL_W + CONV_CH + DN_W
    u_ref[...] = _dot(hb, w_ref[:, :c0])
    qkv_ref[...] = _dot(hb, w_ref[:, c0:c1])
    z_ref[...] = _dot(hb, w_ref[:, c1:c2])
    ba_ref[...] = _dot(hb, w_ref[:, c2:])


def _norm_proj(x2d, mod3, nw, w_in_p, tm, tiles_per_batch):
    n, d = x2d.shape
    rows = mod3.shape[1]
    widths = (POOL_W, CONV_CH, DN_W, BA_W)
    return pl.pallas_call(
        _norm_proj_kernel, grid=(n // tm,),
        in_specs=[pl.BlockSpec((tm, d), lambda i: (i, 0)),
                  _mod_spec(rows, d, tiles_per_batch, 0),
                  _mod_spec(rows, d, tiles_per_batch, 1),
                  pl.BlockSpec((1, d), lambda i: (0, 0)),
                  pl.BlockSpec((d, D_IN_PAD), lambda i: (0, 0))],
        out_specs=[pl.BlockSpec((tm, w), lambda i: (i, 0)) for w in widths],
        out_shape=[jax.ShapeDtypeStruct((n, w), F32) for w in widths],
        compiler_params=_cparams(1), name="norm_proj")(x2d, mod3, mod3, nw, w_in_p)


def _pool_kernel(u_ref, buf_ref, pw_ref, ps_ref, o_ref, ext, *, tt, gb, start_pos):
    t = pl.program_id(1)

    @pl.when(t == 0)
    def _():
        ext[:, 0:POOL_HALO, :] = buf_ref[...]

    @pl.when(t > 0)
    def _():
        ext[:, 0:POOL_HALO, :] = ext[:, tt:tt + POOL_HALO, :]

    ext[:, POOL_HALO:POOL_HALO + tt, :] = u_ref[...]
    pos = start_pos + t * tt + lax.broadcasted_iota(I32, (tt, 1), 0)
    for g, win in enumerate(POOL_WINDOWS):
        sl = slice(g * POOL_GW, (g + 1) * POOL_GW)
        cnt = jnp.minimum(pos + 1, win).astype(F32)
        pooled = []
        for rb in range(gb):
            u = u_ref[rb, :, sl]
            s = u
            for j in range(1, win):
                s = s + ext[rb, POOL_HALO - j:POOL_HALO - j + tt, sl]
            pooled.append(s / cnt - u)
        mixed = _dot(jnp.concatenate(pooled, axis=0).astype(BF16), pw_ref[g]) * ps_ref[:, sl]
        for rb in range(gb):
            o_ref[rb, :, sl] = mixed[rb * tt:(rb + 1) * tt]


def _pool(u3, buf16, pool_w_b, pool_scale, start_pos):
    b, t, w = u3.shape
    tt = min(t, 256)
    assert t % tt == 0 and tt % 8 == 0
    gb = next(g for g in (16, 8, 4, 2, 1) if b % g == 0 and g * tt <= 256)
    return pl.pallas_call(
        functools.partial(_pool_kernel, tt=tt, gb=gb, start_pos=start_pos), grid=(b // gb, t // tt),
        in_specs=[pl.BlockSpec((gb, tt, w), lambda i, j: (i, j, 0)),
                  pl.BlockSpec((gb, POOL_HALO, w), lambda i, j: (i, 0, 0)),
                  pl.BlockSpec((len(POOL_WINDOWS), POOL_GW, POOL_GW), lambda i, j: (0, 0, 0)),
                  pl.BlockSpec((1, w), lambda i, j: (0, 0))],
        out_specs=pl.BlockSpec((gb, tt, w), lambda i, j: (i, j, 0)),
        out_shape=jax.ShapeDtypeStruct((b, t, w), F32),
        scratch_shapes=[pltpu.VMEM((gb, POOL_HALO + tt, w), F32)],
        compiler_params=_cparams(2), name="pool")(u3, buf16, pool_w_b, pool_scale)


def _bf(x):
    return x.astype(BF16)


def _dot3(a, b):
    a_hi, b_hi = _bf(a), _bf(b)
    a_lo, b_lo = _bf(a - a_hi.astype(F32)), _bf(b - b_hi.astype(F32))
    return _dot(a_hi, b_hi) + (_dot(a_hi, b_lo) + _dot(a_lo, b_hi))


def _unit_lower_inverse(a, row, col, eye):
    n = -a
    same = (row // INV_BLOCK) == (col // INV_BLOCK)
    dm = jnp.where(same, n, 0.0)
    lm = n - dm
    d1 = _bf(dm)
    d2 = _dot(d1, d1)
    d2b = _bf(d2)
    d4 = _dot(d2b, d2b)
    d4b = _bf(d4)
    d8 = _dot(d4b, d4b)
    p = _dot(_bf(eye + dm), _bf(eye + d2))
    p2 = _dot(_bf(eye + d4), _bf(eye + d8))
    pb = _bf(_dot(_bf(p), _bf(p2)))
    m = _dot(pb, _bf(lm))
    mb = _bf(m)
    m2 = _dot(mb, mb)
    q = _dot(_bf(eye + m), _bf(eye + m2))
    t0 = _dot(_bf(q), pb)
    resid = (eye - t0) - _dot3(a, t0)
    return t0 + _dot(_bf(t0), _bf(resid))


def _gdn_kernel(qkv_ref, z_ref, ba_ref, cbuf_ref, s0_ref, cw_ref, par_ref, nw_ref,
                o_ref, sfin_ref, state, xext, *, t_len, gb):
    c = pl.program_id(1)

    @pl.when(c == 0)
    def _():
        for rb in range(gb):
            state[rb * DN_HEADS:(rb + 1) * DN_HEADS] = s0_ref[rb]
            xext[rb, 0:CONV_HALO, :] = cbuf_ref[rb]

    @pl.when(c > 0)
    def _():
        for rb in range(gb):
            xext[rb, 0:CONV_HALO, :] = xext[rb, CHUNK:CHUNK + CONV_HALO, :]

    for rb in range(gb):
        one = pl.ds(rb, 1)
        _gdn_chunk(c, qkv_ref.at[one], z_ref.at[one], ba_ref.at[one], cw_ref, par_ref, nw_ref, o_ref.at[one],
                   state.at[pl.ds(rb * DN_HEADS, DN_HEADS)], xext.at[rb], t_len)

    @pl.when(c == pl.num_programs(1) - 1)
    def _():
        for rb in range(gb):
            sfin_ref[rb] = state[rb * DN_HEADS:(rb + 1) * DN_HEADS]


def _gdn_chunk(c, qkv_ref, z_ref, ba_ref, cw_ref, par_ref, nw_ref, o_ref, state, xext, t_len):
    xext[CONV_HALO:CONV_HALO + CHUNK, :] = qkv_ref[0]
    base = CONV_HALO - (CONV_W - 1)
    acc = xext[base:base + CHUNK, :] * cw_ref[0:1, :]
    for j in range(1, CONV_W):
        acc = acc + xext[base + j:base + j + CHUNK, :] * cw_ref[j:j + 1, :]
    qkvc = _silu(acc)

    valid = (c * CHUNK + lax.broadcasted_iota(I32, (CHUNK, 1), 0)) < t_len
    vf = valid.astype(F32)
    ba = ba_ref[0]
    head_lane = lax.broadcasted_iota(I32, (1, LANES), 1) < DN_HEADS
    beta_all = _sigmoid(ba[:, :LANES]) * vf
    decay = jnp.where(head_lane, -jnp.exp(par_ref[0:1, :]), 0.0)
    g_all = decay * _softplus(ba[:, LANES:] + par_ref[1:2, :]) * vf

    r64 = lax.broadcasted_iota(I32, (CHUNK, CHUNK), 0)
    c64 = lax.broadcasted_iota(I32, (CHUNK, CHUNK), 1)
    gam_all = _dot((r64 >= c64).astype(F32), g_all, HI)
    pick = (lax.broadcasted_iota(I32, (8, LANES), 0) == lax.broadcasted_iota(I32, (8, LANES), 1)).astype(F32)
    gam_rows = _dot_nt(pick, gam_all, HI)

    heads = range(DN_HEADS)
    qs, ks, vs = [], [], []
    for h in heads:
        q = qkvc[:, h * DN_D:(h + 1) * DN_D]
        k = qkvc[:, DN_W + h * DN_D:DN_W + (h + 1) * DN_D]
        qs.append(q * lax.rsqrt(jnp.sum(q * q, axis=-1, keepdims=True) + EPS) * (DN_D ** -0.5) * vf)
        ks.append(k * lax.rsqrt(jnp.sum(k * k, axis=-1, keepdims=True) + EPS) * vf)
        vs.append(qkvc[:, 2 * DN_W + h * DN_D:2 * DN_W + (h + 1) * DN_D] * vf)
    q_s = jnp.concatenate(qs, axis=0)
    k_s = jnp.concatenate(ks, axis=0)
    v_s = jnp.concatenate(vs, axis=0)
    beta = jnp.concatenate([beta_all[:, h:h + 1] for h in heads], axis=0)
    gam = jnp.concatenate([gam_all[:, h:h + 1] for h in heads], axis=0)
    gam_r = jnp.concatenate([gam_rows[h:h + 1, :] for h in heads], axis=1)

    hc = DN_HEADS * CHUNK
    row = lax.broadcasted_iota(I32, (hc, hc), 0)
    col = lax.broadcasted_iota(I32, (hc, hc), 1)
    incl = ((row // CHUNK) == (col // CHUNK)) & (row >= col)
    eye = (row == col).astype(F32)
    dec = jnp.where(incl, jnp.exp(jnp.where(incl, gam - gam_r, 0.0)), 0.0)
    kb = _bf(k_s)
    qb = _bf(q_s)
    a_mat = beta * _dot_nt(kb, kb) * jnp.where(row > col, dec, 0.0)
    qk = _dot_nt(qb, kb) * dec
    egam = jnp.exp(gam)
    rhs = jnp.concatenate([(beta * egam) * k_s, beta * v_s], axis=-1)
    sol = _dot3(_unit_lower_inverse(a_mat, row, col, eye), rhs)
    wb = _bf(sol[:, :DN_D])
    uv = sol[:, DN_D:]

    s_f, s_b, us = [], [], []
    for h in heads:
        rows = slice(h * CHUNK, (h + 1) * CHUNK)
        s_f.append(state[h])
        s_b.append(_bf(s_f[h]))
        us.append(uv[rows] - _dot(wb[rows], s_b[h]))
    ub = _bf(jnp.concatenate(us, axis=0))
    o_intra = _dot(_bf(qk), ub)
    for h in heads:
        rows = slice(h * CHUNK, (h + 1) * CHUNK)
        o = egam[rows] * _dot(qb[rows], s_b[h]) + o_intra[rows]
        g_last = gam[(h + 1) * CHUNK - 1:(h + 1) * CHUNK, :]
        k_dec = k_s[rows] * jnp.exp(g_last - gam[rows])
        state[h] = jnp.exp(g_last) * s_f[h] + _dot(_bf(k_dec.T), ub[rows])
        sl = slice(h * DN_D, (h + 1) * DN_D)
        o_ref[0, :, sl] = _rms(o, nw_ref[...]) * _silu(z_ref[0, :, sl])


def _gdn(qkv3, z3, ba3, cbuf8, s0, conv_w8, par, dn_norm, t_len):
    b, tp, _ = qkv3.shape
    assert tp % CHUNK == 0
    gb = GDN_ROWS if b % GDN_ROWS == 0 else 1
    kern = functools.partial(_gdn_kernel, t_len=t_len, gb=gb)
    return pl.pallas_call(
        kern, grid=(b // gb, tp // CHUNK),
        in_specs=[pl.BlockSpec((gb, CHUNK, CONV_CH), lambda i, j: (i, j, 0)),
                  pl.BlockSpec((gb, CHUNK, DN_W), lambda i, j: (i, j, 0)),
                  pl.BlockSpec((gb, CHUNK, BA_W), lambda i, j: (i, j, 0)),
                  pl.BlockSpec((gb, CONV_HALO, CONV_CH), lambda i, j: (i, 0, 0)),
                  pl.BlockSpec((gb, DN_HEADS, DN_D, DN_D), lambda i, j: (i, 0, 0, 0)),
                  pl.BlockSpec((8, CONV_CH), lambda i, j: (0, 0)),
                  pl.BlockSpec((8, LANES), lambda i, j: (0, 0)),
                  pl.BlockSpec((1, DN_D), lambda i, j: (0, 0))],
        out_specs=[pl.BlockSpec((gb, CHUNK, DN_W), lambda i, j: (i, j, 0)),
                   pl.BlockSpec((gb, DN_HEADS, DN_D, DN_D), lambda i, j: (i, 0, 0, 0))],
        out_shape=[jax.ShapeDtypeStruct((b, tp, DN_W), F32),
                   jax.ShapeDtypeStruct((b, DN_HEADS, DN_D, DN_D), F32)],
        scratch_shapes=[pltpu.VMEM((gb * DN_HEADS, DN_D, DN_D), F32),
                        pltpu.VMEM((gb, CONV_HALO + CHUNK, CONV_CH), F32)],
        compiler_params=_cparams(2), name="gdn")(qkv3, z3, ba3, cbuf8, s0, conv_w8, par, dn_norm)


STEP_TILE = 8


def _gdn_step_kernel(qkv_ref, z_ref, ba_ref, cst_ref, s0_ref, cw_ref, par_ref, nw_ref, o_ref, s_ref):
    acc = qkv_ref[...] * cw_ref[CONV_W - 1:CONV_W, :]
    for j in range(CONV_W - 1):
        acc = acc + cst_ref[:, j, :] * cw_ref[j:j + 1, :]
    qkvc = _silu(acc)
    ba = ba_ref[...]
    head_lane = lax.broadcasted_iota(I32, (1, LANES), 1) < DN_HEADS
    beta_all = _sigmoid(ba[:, :LANES])
    decay = jnp.where(head_lane, -jnp.exp(par_ref[0:1, :]), 0.0)
    eg_all = jnp.exp(decay * _softplus(ba[:, LANES:] + par_ref[1:2, :]))
    pad6 = jnp.zeros((6, DN_D), F32)
    pad7 = jnp.zeros((7, DN_D), F32)
    for h in range(DN_HEADS):
        q = qkvc[:, h * DN_D:(h + 1) * DN_D]
        k = qkvc[:, DN_W + h * DN_D:DN_W + (h + 1) * DN_D]
        v = qkvc[:, 2 * DN_W + h * DN_D:2 * DN_W + (h + 1) * DN_D]
        q = q * lax.rsqrt(jnp.sum(q * q, axis=-1, keepdims=True) + EPS) * (DN_D ** -0.5)
        k = k * lax.rsqrt(jnp.sum(k * k, axis=-1, keepdims=True) + EPS)
        qk = _dot_nt(_bf(q), _bf(k))
        o_rows = []
        for b in range(STEP_TILE):
            kb, qb_, vb = k[b:b + 1], q[b:b + 1], v[b:b + 1]
            beta = beta_all[b:b + 1, h:h + 1]
            eg = eg_all[b:b + 1, h:h + 1]
            s_old = s0_ref[b, h]
            ks_qs = _dot(_bf(jnp.concatenate([kb, qb_, pad6], axis=0)), _bf(s_old))
            u = beta * (vb - eg * ks_qs[0:1])
            o_rows.append(eg * ks_qs[1:2] + qk[b:b + 1, b:b + 1] * u)
            k_col = jnp.concatenate([kb, pad7], axis=0).T
            s_ref[b, h] = eg * s_old + _dot(_bf(k_col), _bf(jnp.concatenate([u, pad7], axis=0)))
        o = jnp.concatenate(o_rows, axis=0)
        sl = slice(h * DN_D, (h + 1) * DN_D)
        o_ref[:, sl] = _rms(o, nw_ref[...]) * _silu(z_ref[:, sl])


def _gdn_step(qkv2, z2, ba2, conv_state, s0, conv_w8, par, dn_norm):
    b = qkv2.shape[0]
    assert b % STEP_TILE == 0
    row = lambda w: pl.BlockSpec((STEP_TILE, w), lambda i: (i, 0))
    st = pl.BlockSpec((STEP_TILE, DN_HEADS, DN_D, DN_D), lambda i: (i, 0, 0, 0))
    return pl.pallas_call(
        _gdn_step_kernel, grid=(b // STEP_TILE,),
        in_specs=[row(CONV_CH), row(DN_W), row(BA_W),
                  pl.BlockSpec((STEP_TILE, CONV_W - 1, CONV_CH), lambda i: (i, 0, 0)), st,
                  pl.BlockSpec((8, CONV_CH), lambda i: (0, 0)),
                  pl.BlockSpec((8, LANES), lambda i: (0, 0)),
                  pl.BlockSpec((1, DN_D), lambda i: (0, 0))],
        out_specs=[row(DN_W), st],
        out_shape=[jax.ShapeDtypeStruct((b, DN_W), F32),
                   jax.ShapeDtypeStruct((b, DN_HEADS, DN_D, DN_D), F32)],
        compiler_params=_cparams(1), name="gdn_step")(qkv2, z2, ba2, conv_state, s0, conv_w8, par, dn_norm)


def _out_proj_kernel(pool_ref, dn_ref, x_ref, gt_ref, sh_ref, sc_ref, nw_ref, w_ref, x1_ref, h2_ref):
    mix = _dot(pool_ref[...].astype(BF16), w_ref[:POOL_W, :]) + _dot(dn_ref[...].astype(BF16), w_ref[POOL_W:, :])
    x1 = x_ref[...] + gt_ref[0] * mix
    x1_ref[...] = x1
    h2_ref[...] = _rms(x1, nw_ref[...]) * (1.0 + sc_ref[0]) + sh_ref[0]


def _out_proj(pool2, dn2, x2d, mod3, nw, w_out_b, tm, tiles_per_batch):
    n, d = x2d.shape
    rows = mod3.shape[1]
    tok = lambda w: pl.BlockSpec((tm, w), lambda i: (i, 0))
    return pl.pallas_call(
        _out_proj_kernel, grid=(n // tm,),
        in_specs=[tok(POOL_W), tok(DN_W), tok(d),
                  _mod_spec(rows, d, tiles_per_batch, 2),
                  _mod_spec(rows, d, tiles_per_batch, 3),
                  _mod_spec(rows, d, tiles_per_batch, 4),
                  pl.BlockSpec((1, d), lambda i: (0, 0)),
                  pl.BlockSpec((POOL_W + DN_W, d), lambda i: (0, 0))],
        out_specs=[tok(d), tok(d)],
        out_shape=[jax.ShapeDtypeStruct((n, d), F32)] * 2,
        compiler_params=_cparams(1), name="out_proj")(pool2, dn2, x2d, mod3, mod3, mod3, nw, w_out_b)


def _router_kernel(h_ref, rw_ref, rb_ref, eidx_ref, wts_ref, rank_ref, cnt_ref, run, *, tm):
    i = pl.program_id(0)

    @pl.when(i == 0)
    def _():
        run[...] = jnp.zeros_like(run)

    s = _sigmoid(_dot(h_ref[...].astype(BF16), rw_ref[...]))
    sel = s + rb_ref[...]
    lane_i = lax.broadcasted_iota(I32, (tm, N_EXPERTS), 1)
    lane = lane_i.astype(F32)
    grp = lane_i >> GROUP_SHIFT
    neg = -jnp.inf

    def first_max(x):
        m = jnp.max(x, axis=-1, keepdims=True)
        idx = jnp.min(jnp.where(x == m, lane, float(N_EXPERTS)), axis=-1, keepdims=True)
        return m, idx

    gs = []
    for g in range(N_GROUPS):
        xg = jnp.where(grp == g, sel, neg)
        m1, i1 = first_max(xg)
        m2 = jnp.max(jnp.where(lane == i1, neg, xg), axis=-1, keepdims=True)
        gs.append(m1 + m2)
    cur = jnp.full((tm, N_EXPERTS), neg, F32)
    for g in range(N_GROUPS):
        ahead = jnp.zeros((tm, 1), F32)
        for o in range(N_GROUPS):
            if o < g:
                ahead = ahead + jnp.where(gs[o] >= gs[g], 1.0, 0.0)
            elif o > g:
                ahead = ahead + jnp.where(gs[o] > gs[g], 1.0, 0.0)
        cur = jnp.where((grp == g) & (ahead < TOPK_GROUPS), sel, cur)

    hits, idxs, ws = [], [], []
    chosen = jnp.zeros((tm, N_EXPERTS), F32)
    for _ in range(TOP_K):
        _, idx = first_max(cur)
        hit = lane == idx
        ws.append(jnp.sum(jnp.where(hit, s, 0.0), axis=-1, keepdims=True))
        cur = jnp.where(hit, neg, cur)
        chosen = jnp.where(hit, 1.0, chosen)
        hits.append(hit)
        idxs.append(idx)
    onehot = chosen
    r = lax.broadcasted_iota(I32, (tm, tm), 0)
    c = lax.broadcasted_iota(I32, (tm, tm), 1)
    earlier = jnp.where(r > c, 1.0, 0.0).astype(BF16)
    before = _dot(earlier, onehot.astype(BF16)) + run[...]
    run[...] = run[...] + jnp.sum(onehot, axis=0, keepdims=True)

    wsum = ws[0]
    for w in ws[1:]:
        wsum = wsum + w
    out_lane = lax.broadcasted_iota(I32, (tm, LANES), 1)
    eidx = jnp.zeros((tm, LANES), F32)
    wts = jnp.zeros((tm, LANES), F32)
    rank = jnp.zeros((tm, LANES), F32)
    for k in range(TOP_K):
        rk = jnp.sum(jnp.where(hits[k], before, 0.0), axis=-1, keepdims=True)
        eidx = jnp.where(out_lane == k, idxs[k], eidx)
        wts = jnp.where(out_lane == k, ws[k] / wsum * ROUTED_SCALE, wts)
        rank = jnp.where(out_lane == k, rk, rank)
    eidx_ref[...] = eidx.astype(I32)
    wts_ref[...] = wts
    rank_ref[...] = rank.astype(I32)

    @pl.when(i == pl.num_programs(0) - 1)
    def _():
        cnt_ref[...] = run[...]


def _router(h_all, router_w_b, router_bias, tm):
    n_pad, d = h_all.shape
    kern = functools.partial(_router_kernel, tm=tm)
    tok = pl.BlockSpec((tm, LANES), lambda i: (i, 0))
    return pl.pallas_call(
        kern, grid=(n_pad // tm,),
        in_specs=[pl.BlockSpec((tm, d), lambda i: (i, 0)),
                  pl.BlockSpec((d, N_EXPERTS), lambda i: (0, 0)),
                  pl.BlockSpec((1, N_EXPERTS), lambda i: (0, 0))],
        out_specs=[tok, tok, tok, pl.BlockSpec((1, N_EXPERTS), lambda i: (0, 0))],
        out_shape=[jax.ShapeDtypeStruct((n_pad, LANES), I32),
                   jax.ShapeDtypeStruct((n_pad, LANES), F32),
                   jax.ShapeDtypeStruct((n_pad, LANES), I32),
                   jax.ShapeDtypeStruct((1, N_EXPERTS), F32)],
        scratch_shapes=[pltpu.VMEM((1, N_EXPERTS), F32)],
        compiler_params=_cparams(1), name="router")(h_all, router_w_b, router_bias)


SUBLANES = 8


def _gather_rows(idx_ref, src_hbm, dst, sem):
    def body(g, carry):
        for j in range(SUBLANES):
            tok = idx_ref[0, 0, g * SUBLANES + j]
            pltpu.make_async_copy(src_hbm.at[pl.ds(tok, 1)], dst.at[g, pl.ds(j, 1)], sem).start(priority=j % 2)
        return carry
    lax.fori_loop(0, dst.shape[0], body, 0)


def _wait_rows(dst, sem):
    pltpu.make_async_copy(dst, dst, sem).wait()


def _dest_kernel(eidx_ref, rank_ref, ps_ref, dest_ref, *, tm):
    lane = lax.broadcasted_iota(I32, (tm, N_EXPERTS), 1)
    out_lane = lax.broadcasted_iota(I32, (tm, LANES), 1)
    eidx = eidx_ref[...]
    rank = rank_ref[...].astype(F32)
    dest = jnp.zeros((tm, LANES), F32)
    for k in range(TOP_K):
        start = jnp.sum(jnp.where(lane == eidx[:, k:k + 1], ps_ref[...], 0.0), axis=-1, keepdims=True)
        dest = jnp.where(out_lane == k, start + rank[:, k:k + 1], dest)
    dest_ref[...] = dest.astype(I32)


def _dest(eidx, rank, pstarts, tm):
    n = eidx.shape[0]
    tok = pl.BlockSpec((tm, LANES), lambda i: (i, 0))
    return pl.pallas_call(
        functools.partial(_dest_kernel, tm=tm), grid=(n // tm,),
        in_specs=[tok, tok, pl.BlockSpec((1, N_EXPERTS), lambda i: (0, 0))],
        out_specs=tok, out_shape=jax.ShapeDtypeStruct((n, LANES), I32),
        compiler_params=_cparams(1), name="dest")(eidx, rank, pstarts)


def _dispatch_kernel(clear_ref, dest_ref, h_ref, xs_hbm, pbuf, zbuf, sems, zsem, *, tm, nb):
    i = pl.program_id(0)
    n = pl.num_programs(0)
    slot = i % 2

    def zero_copy(blk):
        return pltpu.make_async_copy(zbuf, xs_hbm.at[pl.ds(blk * ROW_BLOCK, ROW_BLOCK)], zsem)

    def wait_slot(s):
        for _ in range(TOP_K):
            _wait_rows(pbuf.at[s], sems.at[s])

    @pl.when(i == 0)
    def _():
        zbuf[...] = jnp.zeros_like(zbuf)

        def start(blk, carry):
            @pl.when(clear_ref[blk] > 0)
            def _():
                zero_copy(blk).start()
            return carry

        def wait(blk, carry):
            @pl.when(clear_ref[blk] > 0)
            def _():
                zero_copy(blk).wait()
            return carry
        lax.fori_loop(0, nb, start, 0)
        lax.fori_loop(0, nb, wait, 0)

    @pl.when(i >= 2)
    def _():
        wait_slot(slot)

    pbuf[slot] = h_ref[...]

    def body(g, carry):
        for j in range(SUBLANES):
            for k in range(TOP_K):
                row = dest_ref[0, 0, (g * SUBLANES + j) * TOP_K + k]
                pltpu.make_async_copy(pbuf.at[slot, g, pl.ds(j, 1)], xs_hbm.at[pl.ds(row, 1)],
                                      sems.at[slot]).start(priority=k % 2)
        return carry
    lax.fori_loop(0, tm // SUBLANES, body, 0)

    @pl.when(i == n - 1)
    def _():
        wait_slot(slot)

    @pl.when(jnp.logical_and(i == n - 1, n > 1))
    def _():
        wait_slot(1 - slot)


def _dispatch(clear, dest3, h_all, tm):
    n, d = h_all.shape
    nb = clear.shape[0]
    groups = tm // SUBLANES
    grid_spec = pltpu.PrefetchScalarGridSpec(
        num_scalar_prefetch=1, grid=(n // tm,),
        in_specs=[pl.BlockSpec((1, 1, tm * TOP_K), lambda i, cl: (i, 0, 0), memory_space=pltpu.SMEM),
                  pl.BlockSpec((groups, SUBLANES, d), lambda i, cl: (i, 0, 0))],
        out_specs=pl.BlockSpec(memory_space=pl.ANY),
        scratch_shapes=[pltpu.VMEM((2, groups, SUBLANES, d), F32), pltpu.VMEM((ROW_BLOCK, d), F32),
                        pltpu.SemaphoreType.DMA((2,)), pltpu.SemaphoreType.DMA(())])
    return pl.pallas_call(
        functools.partial(_dispatch_kernel, tm=tm, nb=nb), grid_spec=grid_spec,
        out_shape=jax.ShapeDtypeStruct((nb * ROW_BLOCK, d), F32),
        compiler_params=_cparams(1), name="dispatch")(clear, dest3, h_all.reshape(n // SUBLANES, SUBLANES, d))


def _experts_kernel(be_ref, nact_ref, x_ref, g_ref, u_ref, d_ref, y_ref, gb, ub, db):
    i = pl.program_id(0)
    nact = nact_ref[0]
    changed = jnp.logical_or(i == 0, be_ref[i] != be_ref[jnp.maximum(i - 1, 0)])

    @pl.when(jnp.logical_and(i < nact, changed))
    def _():
        gb[...] = g_ref[0].astype(BF16)
        ub[...] = u_ref[0].astype(BF16)
        db[...] = d_ref[0].astype(BF16)

    @pl.when(i < nact)
    def _():
        x = x_ref[...].astype(BF16)
        hid = _silu(_dot(x, gb[...])) * _dot(x, ub[...])
        y_ref[...] = _dot(hid.astype(BF16), db[...])

    @pl.when(i >= nact)
    def _():
        y_ref[...] = jnp.zeros_like(y_ref)


def _experts(block_e, nact, x_sorted, exp_gate, exp_up, exp_down, layer):
    nb = x_sorted.shape[0] // ROW_BLOCK
    _, d, de = exp_gate.shape
    base = layer * N_EXPERTS
    grid_spec = pltpu.PrefetchScalarGridSpec(
        num_scalar_prefetch=2, grid=(nb,),
        in_specs=[pl.BlockSpec((ROW_BLOCK, d),
                               lambda i, be, na: (jnp.minimum(i, jnp.maximum(na[0] - 1, 0)), 0)),
                  pl.BlockSpec((1, d, de), lambda i, be, na: (base + be[i], 0, 0)),
                  pl.BlockSpec((1, d, de), lambda i, be, na: (base + be[i], 0, 0)),
                  pl.BlockSpec((1, de, d), lambda i, be, na: (base + be[i], 0, 0))],
        out_specs=pl.BlockSpec((ROW_BLOCK, d), lambda i, be, na: (i, 0)),
        scratch_shapes=[pltpu.VMEM((d, de), BF16), pltpu.VMEM((d, de), BF16), pltpu.VMEM((de, d), BF16)])
    return pl.pallas_call(
        _experts_kernel, grid_spec=grid_spec,
        out_shape=jax.ShapeDtypeStruct((nb * ROW_BLOCK, d), F32),
        compiler_params=_cparams(1), name="experts")(block_e, nact, x_sorted, exp_gate, exp_up, exp_down)


def _combine_kernel(cur_ref, nxt_ref, y_hbm, wts_ref, x1_ref, h_ref, gt_ref, sg_ref, su_ref, sd_ref,
                    nf_ref, x2_ref, yn_ref, ybuf, sems, *, tm):
    i = pl.program_id(0)
    n = pl.num_programs(0)
    slot = i % 2
    groups = tm // SUBLANES

    @pl.when(i == 0)
    def _():
        _gather_rows(cur_ref, y_hbm, ybuf.at[0], sems.at[0])

    @pl.when(i + 1 < n)
    def _():
        _gather_rows(nxt_ref, y_hbm, ybuf.at[1 - slot], sems.at[1 - slot])

    hb = h_ref[...].astype(BF16)
    hid = _silu(_dot(hb, sg_ref[...])) * _dot(hb, su_ref[...])
    acc = _dot(hid.astype(BF16), sd_ref[...])
    _wait_rows(ybuf.at[slot], sems.at[slot])
    wts = wts_ref[...]
    for k in range(TOP_K):
        y_k = ybuf[slot, k * groups:(k + 1) * groups].reshape(tm, acc.shape[1])
        acc = acc + y_k * wts[:, k:k + 1]
    x2 = x1_ref[...] + gt_ref[0] * acc
    x2_ref[...] = x2
    yn_ref[...] = _rms(x2, nf_ref[...])


def _combine(dest3, y_sorted, wts, x1, h2, mod3, sh_gate_b, sh_up_b, sh_down_b, norm_final, tm, tiles_per_batch):
    n, d = x1.shape
    nt = n // tm
    rows = mod3.shape[1]
    ds = sh_gate_b.shape[1]
    tok = lambda w: pl.BlockSpec((tm, w), lambda i: (i, 0))
    kern = functools.partial(_combine_kernel, tm=tm)
    return pl.pallas_call(
        kern, grid=(nt,),
        in_specs=[pl.BlockSpec((1, 1, tm * TOP_K), lambda i: (i, 0, 0), memory_space=pltpu.SMEM),
                  pl.BlockSpec((1, 1, tm * TOP_K), lambda i: (jnp.minimum(i + 1, nt - 1), 0, 0),
                               memory_space=pltpu.SMEM),
                  pl.BlockSpec(memory_space=pl.ANY),
                  tok(LANES), tok(d), tok(d),
                  _mod_spec(rows, d, tiles_per_batch, 5),
                  pl.BlockSpec((d, ds), lambda i: (0, 0)),
                  pl.BlockSpec((d, ds), lambda i: (0, 0)),
                  pl.BlockSpec((ds, d), lambda i: (0, 0)),
                  pl.BlockSpec((1, d), lambda i: (0, 0))],
        out_specs=[tok(d), tok(d)],
        out_shape=[jax.ShapeDtypeStruct((n, d), F32)] * 2,
        scratch_shapes=[pltpu.VMEM((2, tm * TOP_K // SUBLANES, SUBLANES, d), F32),
                        pltpu.SemaphoreType.DMA((2,))],
        compiler_params=_cparams(1), name="combine")(
            dest3, dest3, y_sorted, wts, x1, h2, mod3, sh_gate_b, sh_up_b, sh_down_b, norm_final)


def _last_rows(buf, new, keep):
    t = new.shape[1]
    if t >= keep:
        return new[:, t - keep:]
    return jnp.concatenate([buf[:, t:], new], axis=1)


def _mixers(u, qkv, z, ba, b, t, pool_buf, conv_buf, s0, start_pos, lw):
    t_pool = -(-t // 8) * 8
    t_gdn = -(-t // CHUNK) * CHUNK
    u3 = u.reshape(b, t, POOL_W)
    qkv3 = qkv.reshape(b, t, CONV_CH)
    pad_t = lambda a, tp: a if tp == t else jnp.pad(a, ((0, 0), (0, tp - t), (0, 0)))
    buf16 = jnp.pad(pool_buf, ((0, 0), (POOL_HALO - POOL_BUF, 0), (0, 0)))
    pool_out = _pool(pad_t(u3, t_pool), buf16, lw["pool_w"], lw["pool_scale"], start_pos)[:, :t]
    if t == 1:
        dn_out, s_new = _gdn_step(qkv, z, ba, conv_buf, s0, lw["conv_w"], lw["par"], lw["dn_norm"])
    else:
        cbuf8 = jnp.pad(conv_buf, ((0, 0), (CONV_HALO - (CONV_W - 1), 0), (0, 0)))
        dn_out, s_new = _gdn(pad_t(qkv3, t_gdn), pad_t(z.reshape(b, t, DN_W), t_gdn),
                             pad_t(ba.reshape(b, t, BA_W), t_gdn), cbuf8, s0, lw["conv_w"], lw["par"],
                             lw["dn_norm"], t)
        dn_out = dn_out[:, :t].reshape(b * t, DN_W)
    new_pool = _last_rows(pool_buf, u3, POOL_BUF)
    new_conv = _last_rows(conv_buf, qkv3, CONV_W - 1)
    return pool_out.reshape(b * t, POOL_W), dn_out, new_pool, new_conv, s_new


def _block_tables(counts, n_tok):
    nk = n_tok * TOP_K
    nb = -(-(nk + N_EXPERTS * (ROW_BLOCK - 1)) // ROW_BLOCK)
    pcounts = (counts + ROW_BLOCK - 1) // ROW_BLOCK * ROW_BLOCK
    pends = jnp.cumsum(pcounts)
    pstarts = pends - pcounts
    first_row = jnp.arange(nb, dtype=I32)[:, None] * ROW_BLOCK
    block_e = jnp.minimum(jnp.sum((pends[None, :] <= first_row).astype(I32), axis=1), N_EXPERTS - 1)
    nact = (pends[-1] // ROW_BLOCK).astype(I32).reshape(1)
    blk = jnp.arange(nb, dtype=I32)
    next_e = jnp.concatenate([block_e[1:], jnp.full((1,), -1, I32)])
    clear = ((blk >= nact[0] - 1) | (next_e != block_e)).astype(I32)
    return pstarts.astype(F32).reshape(1, N_EXPERTS), block_e, nact, clear


def _dest_tiles(dest, tm, k_major):
    n = dest.shape[0]
    tiles = dest.reshape(n // tm, tm, TOP_K)
    if k_major:
        tiles = tiles.transpose(0, 2, 1)
    return tiles.reshape(n // tm, 1, tm * TOP_K)


def kernel(x_prompt, x_sample, state_pool, state_conv, state_delta, c_prompt, c_sample, norm_mix, norm_ffn, w_ada, b_ada, w_in, pool_w, pool_scale, conv_w, a_log, dt_bias, dn_norm, w_out, router_w, router_bias, exp_gate, exp_up, exp_down, sh_gate, sh_up, sh_down, norm_final):
    bp, tp, d = x_prompt.shape
    bs, ts, _ = x_sample.shape
    depth = w_ada.shape[0]
    past_len = PAST_LEN
    n_p, n_s = bp * tp, bs * ts
    n_tok = n_p + n_s
    tm_p = min(512, tp)
    tm_c = min(128, tp)
    tm_r = next(t for t in (128, 64, 32, 16, 8) if n_tok % t == 0)
    assert tp % tm_p == 0 and ts == 1 and bs % 8 == 0
    n_exp = exp_gate.shape[1]
    gate_w = exp_gate.reshape((depth * n_exp,) + exp_gate.shape[2:])
    up_w = exp_up.reshape((depth * n_exp,) + exp_up.shape[2:])
    down_w = exp_down.reshape((depth * n_exp,) + exp_down.shape[2:])

    c_all = jnp.concatenate([c_prompt, c_sample], axis=0)
    m_rows = -(-c_all.shape[0] // 8) * 8
    mod_all = _ada(jnp.pad(c_all, ((0, m_rows - c_all.shape[0]), (0, 0))), w_ada, b_ada)

    c2 = POOL_W + CONV_CH + DN_W
    lane_pad = jnp.zeros((depth, d, LANES - DN_HEADS), F32)
    w_in_p = jnp.concatenate([w_in[:, :, :c2], w_in[:, :, c2:c2 + DN_HEADS], lane_pad,
                              w_in[:, :, c2 + DN_HEADS:], lane_pad], axis=-1).astype(BF16)
    head_pad = lambda a: jnp.pad(a, ((0, 0), (0, LANES - DN_HEADS)))
    par_all = jnp.stack([head_pad(a_log), head_pad(dt_bias)] + [jnp.zeros((depth, LANES), F32)] * 6, axis=1)
    conv_w8 = jnp.pad(conv_w, ((0, 0), (0, 8 - CONV_W), (0, 0)))

    xp = x_prompt.reshape(n_p, d)
    xs = x_sample.reshape(n_s, d)
    zero_pool = jnp.zeros((bp, POOL_BUF, POOL_W), F32)
    zero_conv = jnp.zeros((bp, CONV_W - 1, CONV_CH), F32)
    zero_state = jnp.zeros((bp, DN_HEADS, DN_D, DN_D), F32)
    outs = {k: [] for k in ("pool_p", "conv_p", "delta_p", "pool_s", "conv_s", "delta_s")}
    yp = ys = None
    for l in range(depth):
        lw = {"pool_w": pool_w[l].astype(BF16), "pool_scale": pool_scale[l].reshape(1, POOL_W),
              "conv_w": conv_w8[l], "par": par_all[l], "dn_norm": dn_norm[l].reshape(1, DN_D)}
        mod_p = mod_all[l, :bp].reshape(bp, 1, 6 * d)
        mod_s = mod_all[l, bp:bp + bs].reshape(1, bs, 6 * d)
        nw_mix = norm_mix[l].reshape(1, d)
        nw_ffn = norm_ffn[l].reshape(1, d)
        w_out_b = w_out[l].astype(BF16)

        up, qkvp, zp, bap = _norm_proj(xp, mod_p, nw_mix, w_in_p[l], tm_p, tp // tm_p)
        us, qkvs, zs, bas = _norm_proj(xs, mod_s, nw_mix, w_in_p[l], n_s, 1)
        pool_p, dn_p, npool_p, nconv_p, ns_p = _mixers(up, qkvp, zp, bap, bp, tp, zero_pool, zero_conv,
                                                       zero_state, 0, lw)
        pool_s, dn_s, npool_s, nconv_s, ns_s = _mixers(us, qkvs, zs, bas, bs, ts, state_pool[l], state_conv[l],
                                                       state_delta[l], past_len, lw)
        for key, val in (("pool_p", npool_p), ("conv_p", nconv_p), ("delta_p", ns_p),
                         ("pool_s", npool_s), ("conv_s", nconv_s), ("delta_s", ns_s)):
            outs[key].append(val)

        x1p, h2p = _out_proj(pool_p, dn_p, xp, mod_p, nw_ffn, w_out_b, tm_p, tp // tm_p)
        x1s, h2s = _out_proj(pool_s, dn_s, xs, mod_s, nw_ffn, w_out_b, n_s, 1)

        h_all = jnp.concatenate([h2p, h2s], axis=0)
        eidx, wts, rank, counts = _router(h_all, router_w[l].astype(BF16), router_bias[l].reshape(1, N_EXPERTS),
                                          tm_r)
        pstarts, block_e, nact, clear = _block_tables(counts.reshape(-1).astype(I32), n_tok)
        dest = _dest(eidx, rank, pstarts, tm_r)[:, :TOP_K]
        x_sorted = _dispatch(clear, _dest_tiles(dest, tm_r, False), h_all, tm_r)
        y_sorted = _experts(block_e, nact, x_sorted, gate_w, up_w, down_w, l)

        shg, shu, shd = sh_gate[l].astype(BF16), sh_up[l].astype(BF16), sh_down[l].astype(BF16)
        nf = norm_final.reshape(1, d)
        xp, yp = _combine(_dest_tiles(dest[:n_p], tm_c, True), y_sorted, wts[:n_p], x1p, h2p, mod_p, shg, shu, shd,
                          nf, tm_c, tp // tm_c)
        xs, ys = _combine(_dest_tiles(dest[n_p:], n_s, True), y_sorted, wts[n_p:], x1s, h2s, mod_s, shg, shu, shd,
                          nf, n_s, 1)

    stack = lambda key: jnp.stack(outs[key])
    return (yp.reshape(bp, tp, d), ys.reshape(bs, ts, d), stack("pool_p"), stack("conv_p"), stack("delta_p"),
            stack("pool_s"), stack("conv_s"), stack("delta_s"))
```

```python
import functools

import jax
import jax.numpy as jnp
from jax import lax
from jax.experimental import pallas as pl
from jax.experimental.pallas import tpu as pltpu

F32, BF16, I32 = jnp.float32, jnp.bfloat16, jnp.int32
HI = lax.Precision.HIGHEST
EPS = 1e-6

LANES = 128
POOL_WINDOWS = (2, 4, 8, 16)
POOL_GW = 128
POOL_W = 512
POOL_BUF = 15
POOL_HALO = 16
DN_HEADS = 4
DN_D = 128
DN_W = 512
CONV_W = 4
CONV_CH = 3 * DN_W
CONV_HALO = 8
CHUNK = 64
INV_BLOCK = 16
GDN_ROWS = 2
N_EXPERTS = 256
TOP_K = 8
N_GROUPS = 8
TOPK_GROUPS = 4
GROUP_SHIFT = 5
ROUTED_SCALE = 2.5
PAST_LEN = 16384
ROW_BLOCK = 256
BA_W = 2 * LANES
D_IN_PAD = POOL_W + CONV_CH + DN_W + BA_W
VMEM_LIMIT = 48 * 2 ** 20


def _cparams(n_axes):
    return pltpu.CompilerParams(dimension_semantics=("arbitrary",) * n_axes,
                                vmem_limit_bytes=VMEM_LIMIT)


def _sigmoid(x):
    return 1.0 / (1.0 + jnp.exp(-x))


def _silu(x):
    return x * _sigmoid(x)


def _softplus(x):
    return jnp.maximum(x, 0.0) + jnp.log(1.0 + jnp.exp(-jnp.abs(x)))


def _rms(x, w):
    return x * lax.rsqrt(jnp.mean(x * x, axis=-1, keepdims=True) + EPS) * w


def _dot(a, b, precision=None):
    return jnp.dot(a, b, preferred_element_type=F32, precision=precision)


def _dot_nt(a, b, precision=None):
    return lax.dot_general(a, b, (((1,), (1,)), ((), ())), preferred_element_type=F32,
                           precision=precision)


def _ada_kernel(c_ref, w_ref, b_ref, o_ref):
    o_ref[0] = _dot(_silu(c_ref[...]), w_ref[0], HI) + b_ref[0]


def _ada(c_all, w_ada, b_ada):
    depth, d, d6 = w_ada.shape
    m = c_all.shape[0]
    tn = 1024
    return pl.pallas_call(
        _ada_kernel, grid=(depth, d6 // tn),
        in_specs=[pl.BlockSpec((m, d), lambda l, j: (0, 0)),
                  pl.BlockSpec((1, d, tn), lambda l, j: (l, 0, j)),
                  pl.BlockSpec((1, 1, tn), lambda l, j: (l, 0, j))],
        out_specs=pl.BlockSpec((1, m, tn), lambda l, j: (l, 0, j)),
        out_shape=jax.ShapeDtypeStruct((depth, m, d6), F32),
        compiler_params=_cparams(2), name="ada")(c_all, w_ada, b_ada.reshape(depth, 1, d6))


def _mod_spec(rows, d, tiles_per_batch, chunk):
    return pl.BlockSpec((1, rows, d), lambda i: (i // tiles_per_batch, 0, chunk))


def _norm_proj_kernel(x_ref, sh_ref, sc_ref, nw_ref, w_ref, u_ref, qkv_ref, z_ref, ba_ref):
    h = _rms(x_ref[...], nw_ref[...]) * (1.0 + sc_ref[0]) + sh_ref[0]
    hb = h.astype(BF16)
    c0, c1, c2 = POOL_W, POOL_W + CONV_CH, POOL_W + CONV_CH + DN_W
    u_ref[...] = _dot(hb, w_ref[:, :c0])
    qkv_ref[...] = _dot(hb, w_ref[:, c0:c1])
    z_ref[...] = _dot(hb, w_ref[:, c1:c2])
    ba_ref[...] = _dot(hb, w_ref[:, c2:])


def _norm_proj(x2d, mod3, nw, w_in_p, tm, tiles_per_batch):
    n, d = x2d.shape
    rows = mod3.shape[1]
    widths = (POOL_W, CONV_CH, DN_W, BA_W)
    return pl.pallas_call(
        _norm_proj_kernel, grid=(n // tm,),
        in_specs=[pl.BlockSpec((tm, d), lambda i: (i, 0)),
                  _mod_spec(rows, d, tiles_per_batch, 0),
                  _mod_spec(rows, d, tiles_per_batch, 1),
                  pl.BlockSpec((1, d), lambda i: (0, 0)),
                  pl.BlockSpec((d, D_IN_PAD), lambda i: (0, 0))],
        out_specs=[pl.BlockSpec((tm, w), lambda i: (i, 0)) for w in widths],
        out_shape=[jax.ShapeDtypeStruct((n, w), F32) for w in widths],
        compiler_params=_cparams(1), name="norm_proj")(x2d, mod3, mod3, nw, w_in_p)


def _pool_kernel(u_ref, buf_ref, pw_ref, ps_ref, o_ref, ext, *, tt, gb, start_pos):
    t = pl.program_id(1)

    @pl.when(t == 0)
    def _():
        ext[:, 0:POOL_HALO, :] = buf_ref[...]

    @pl.when(t > 0)
    def _():
        ext[:, 0:POOL_HALO, :] = ext[:, tt:tt + POOL_HALO, :]

    ext[:, POOL_HALO:POOL_HALO + tt, :] = u_ref[...]
    pos = start_pos + t * tt + lax.broadcasted_iota(I32, (tt, 1), 0)
    for g, win in enumerate(POOL_WINDOWS):
        sl = slice(g * POOL_GW, (g + 1) * POOL_GW)
        cnt = jnp.minimum(pos + 1, win).astype(F32)
        pooled = []
        for rb in range(gb):
            u = u_ref[rb, :, sl]
            s = u
            for j in range(1, win):
                s = s + ext[rb, POOL_HALO - j:POOL_HALO - j + tt, sl]
            pooled.append(s / cnt - u)
        mixed = _dot(jnp.concatenate(pooled, axis=0).astype(BF16), pw_ref[g]) * ps_ref[:, sl]
        for rb in range(gb):
            o_ref[rb, :, sl] = mixed[rb * tt:(rb + 1) * tt]


def _pool(u3, buf16, pool_w_b, pool_scale, start_pos):
    b, t, w = u3.shape
    tt = min(t, 256)
    assert t % tt == 0 and tt % 8 == 0
    gb = next(g for g in (16, 8, 4, 2, 1) if b % g == 0 and g * tt <= 256)
    return pl.pallas_call(
        functools.partial(_pool_kernel, tt=tt, gb=gb, start_pos=start_pos), grid=(b // gb, t // tt),
        in_specs=[pl.BlockSpec((gb, tt, w), lambda i, j: (i, j, 0)),
                  pl.BlockSpec((gb, POOL_HALO, w), lambda i, j: (i, 0, 0)),
                  pl.BlockSpec((len(POOL_WINDOWS), POOL_GW, POOL_GW), lambda i, j: (0, 0, 0)),
                  pl.BlockSpec((1, w), lambda i, j: (0, 0))],
        out_specs=pl.BlockSpec((gb, tt, w), lambda i, j: (i, j, 0)),
        out_shape=jax.ShapeDtypeStruct((b, t, w), F32),
        scratch_shapes=[pltpu.VMEM((gb, POOL_HALO + tt, w), F32)],
        compiler_params=_cparams(2), name="pool")(u3, buf16, pool_w_b, pool_scale)


def _bf(x):
    return x.astype(BF16)


def _dot3(a, b):
    a_hi, b_hi = _bf(a), _bf(b)
    a_lo, b_lo = _bf(a - a_hi.astype(F32)), _bf(b - b_hi.astype(F32))
    return _dot(a_hi, b_hi) + (_dot(a_hi, b_lo) + _dot(a_lo, b_hi))


def _unit_lower_inverse(a, row, col, eye):
    n = -a
    same = (row // INV_BLOCK) == (col // INV_BLOCK)
    dm = jnp.where(same, n, 0.0)
    lm = n - dm
    d1 = _bf(dm)
    d2 = _dot(d1, d1)
    d2b = _bf(d2)
    d4 = _dot(d2b, d2b)
    d4b = _bf(d4)
    d8 = _dot(d4b, d4b)
    p = _dot(_bf(eye + dm), _bf(eye + d2))
    p2 = _dot(_bf(eye + d4), _bf(eye + d8))
    pb = _bf(_dot(_bf(p), _bf(p2)))
    m = _dot(pb, _bf(lm))
    mb = _bf(m)
    m2 = _dot(mb, mb)
    q = _dot(_bf(eye + m), _bf(eye + m2))
    t0 = _dot(_bf(q), pb)
    resid = (eye - t0) - _dot3(a, t0)
    return t0 + _dot(_bf(t0), _bf(resid))


def _gdn_kernel(qkv_ref, z_ref, ba_ref, cbuf_ref, s0_ref, cw_ref, par_ref, nw_ref,
                o_ref, sfin_ref, state, xext, *, t_len, gb):
    c = pl.program_id(1)

    @pl.when(c == 0)
    def _():
        for rb in range(gb):
            state[rb * DN_HEADS:(rb + 1) * DN_HEADS] = s0_ref[rb]
            xext[rb, 0:CONV_HALO, :] = cbuf_ref[rb]

    @pl.when(c > 0)
    def _():
        for rb in range(gb):
            xext[rb, 0:CONV_HALO, :] = xext[rb, CHUNK:CHUNK + CONV_HALO, :]

    for rb in range(gb):
        one = pl.ds(rb, 1)
        _gdn_chunk(c, qkv_ref.at[one], z_ref.at[one], ba_ref.at[one], cw_ref, par_ref, nw_ref, o_ref.at[one],
                   state.at[pl.ds(rb * DN_HEADS, DN_HEADS)], xext.at[rb], t_len)

    @pl.when(c == pl.num_programs(1) - 1)
    def _():
        for rb in range(gb):
            sfin_ref[rb] = state[rb * DN_HEADS:(rb + 1) * DN_HEADS]


def _gdn_chunk(c, qkv_ref, z_ref, ba_ref, cw_ref, par_ref, nw_ref, o_ref, state, xext, t_len):
    xext[CONV_HALO:CONV_HALO + CHUNK, :] = qkv_ref[0]
    base = CONV_HALO - (CONV_W - 1)
    acc = xext[base:base + CHUNK, :] * cw_ref[0:1, :]
    for j in range(1, CONV_W):
        acc = acc + xext[base + j:base + j + CHUNK, :] * cw_ref[j:j + 1, :]
    qkvc = _silu(acc)

    valid = (c * CHUNK + lax.broadcasted_iota(I32, (CHUNK, 1), 0)) < t_len
    vf = valid.astype(F32)
    ba = ba_ref[0]
    head_lane = lax.broadcasted_iota(I32, (1, LANES), 1) < DN_HEADS
    beta_all = _sigmoid(ba[:, :LANES]) * vf
    decay = jnp.where(head_lane, -jnp.exp(par_ref[0:1, :]), 0.0)
    g_all = decay * _softplus(ba[:, LANES:] + par_ref[1:2, :]) * vf

    r64 = lax.broadcasted_iota(I32, (CHUNK, CHUNK), 0)
    c64 = lax.broadcasted_iota(I32, (CHUNK, CHUNK), 1)
    gam_all = _dot((r64 >= c64).astype(F32), g_all, HI)
    pick = (lax.broadcasted_iota(I32, (8, LANES), 0) == lax.broadcasted_iota(I32, (8, LANES), 1)).astype(F32)
    gam_rows = _dot_nt(pick, gam_all, HI)

    heads = range(DN_HEADS)
    qs, ks, vs = [], [], []
    for h in heads:
        q = qkvc[:, h * DN_D:(h + 1) * DN_D]
        k = qkvc[:, DN_W + h * DN_D:DN_W + (h + 1) * DN_D]
        qs.append(q * lax.rsqrt(jnp.sum(q * q, axis=-1, keepdims=True) + EPS) * (DN_D ** -0.5) * vf)
        ks.append(k * lax.rsqrt(jnp.sum(k * k, axis=-1, keepdims=True) + EPS) * vf)
        vs.append(qkvc[:, 2 * DN_W + h * DN_D:2 * DN_W + (h + 1) * DN_D] * vf)
    q_s = jnp.concatenate(qs, axis=0)
    k_s = jnp.concatenate(ks, axis=0)
    v_s = jnp.concatenate(vs, axis=0)
    beta = jnp.concatenate([beta_all[:, h:h + 1] for h in heads], axis=0)
    gam = jnp.concatenate([gam_all[:, h:h + 1] for h in heads], axis=0)
    gam_r = jnp.concatenate([gam_rows[h:h + 1, :] for h in heads], axis=1)

    hc = DN_HEADS * CHUNK
    row = lax.broadcasted_iota(I32, (hc, hc), 0)
    col = lax.broadcasted_iota(I32, (hc, hc), 1)
    incl = ((row // CHUNK) == (col // CHUNK)) & (row >= col)
    eye = (row == col).astype(F32)
    dec = jnp.where(incl, jnp.exp(jnp.where(incl, gam - gam_r, 0.0)), 0.0)
    kb = _bf(k_s)
    qb = _bf(q_s)
    a_mat = beta * _dot_nt(kb, kb) * jnp.where(row > col, dec, 0.0)
    qk = _dot_nt(qb, kb) * dec
    egam = jnp.exp(gam)
    rhs = jnp.concatenate([(beta * egam) * k_s, beta * v_s], axis=-1)
    sol = _dot3(_unit_lower_inverse(a_mat, row, col, eye), rhs)
    wb = _bf(sol[:, :DN_D])
    uv = sol[:, DN_D:]

    s_f, s_b, us = [], [], []
    for h in heads:
        rows = slice(h * CHUNK, (h + 1) * CHUNK)
        s_f.append(state[h])
        s_b.append(_bf(s_f[h]))
        us.append(uv[rows] - _dot(wb[rows], s_b[h]))
    ub = _bf(jnp.concatenate(us, axis=0))
    o_intra = _dot(_bf(qk), ub)
    for h in heads:
        rows = slice(h * CHUNK, (h + 1) * CHUNK)
        o = egam[rows] * _dot(qb[rows], s_b[h]) + o_intra[rows]
        g_last = gam[(h + 1) * CHUNK - 1:(h + 1) * CHUNK, :]
        k_dec = k_s[rows] * jnp.exp(g_last - gam[rows])
        state[h] = jnp.exp(g_last) * s_f[h] + _dot(_bf(k_dec.T), ub[rows])
        sl = slice(h * DN_D, (h + 1) * DN_D)
        o_ref[0, :, sl] = _rms(o, nw_ref[...]) * _silu(z_ref[0, :, sl])


def _gdn(qkv3, z3, ba3, cbuf8, s0, conv_w8, par, dn_norm, t_len):
    b, tp, _ = qkv3.shape
    assert tp % CHUNK == 0
    gb = GDN_ROWS if b % GDN_ROWS == 0 else 1
    kern = functools.partial(_gdn_kernel, t_len=t_len, gb=gb)
    return pl.pallas_call(
        kern, grid=(b // gb, tp // CHUNK),
        in_specs=[pl.BlockSpec((gb, CHUNK, CONV_CH), lambda i, j: (i, j, 0)),
                  pl.BlockSpec((gb, CHUNK, DN_W), lambda i, j: (i, j, 0)),
                  pl.BlockSpec((gb, CHUNK, BA_W), lambda i, j: (i, j, 0)),
                  pl.BlockSpec((gb, CONV_HALO, CONV_CH), lambda i, j: (i, 0, 0)),
                  pl.BlockSpec((gb, DN_HEADS, DN_D, DN_D), lambda i, j: (i, 0, 0, 0)),
                  pl.BlockSpec((8, CONV_CH), lambda i, j: (0, 0)),
                  pl.BlockSpec((8, LANES), lambda i, j: (0, 0)),
                  pl.BlockSpec((1, DN_D), lambda i, j: (0, 0))],
        out_specs=[pl.BlockSpec((gb, CHUNK, DN_W), lambda i, j: (i, j, 0)),
                   pl.BlockSpec((gb, DN_HEADS, DN_D, DN_D), lambda i, j: (i, 0, 0, 0))],
        out_shape=[jax.ShapeDtypeStruct((b, tp, DN_W), F32),
                   jax.ShapeDtypeStruct((b, DN_HEADS, DN_D, DN_D), F32)],
        scratch_shapes=[pltpu.VMEM((gb * DN_HEADS, DN_D, DN_D), F32),
                        pltpu.VMEM((gb, CONV_HALO + CHUNK, CONV_CH), F32)],
        compiler_params=_cparams(2), name="gdn")(qkv3, z3, ba3, cbuf8, s0, conv_w8, par, dn_norm)


STEP_TILE = 8


def _gdn_step_kernel(qkv_ref, z_ref, ba_ref, cst_ref, s0_ref, cw_ref, par_ref, nw_ref, o_ref, s_ref):
    acc = qkv_ref[...] * cw_ref[CONV_W - 1:CONV_W, :]
    for j in range(CONV_W - 1):
        acc = acc + cst_ref[:, j, :] * cw_ref[j:j + 1, :]
    qkvc = _silu(acc)
    ba = ba_ref[...]
    head_lane = lax.broadcasted_iota(I32, (1, LANES), 1) < DN_HEADS
    beta_all = _sigmoid(ba[:, :LANES])
    decay = jnp.where(head_lane, -jnp.exp(par_ref[0:1, :]), 0.0)
    eg_all = jnp.exp(decay * _softplus(ba[:, LANES:] + par_ref[1:2, :]))
    pad6 = jnp.zeros((6, DN_D), F32)
    pad7 = jnp.zeros((7, DN_D), F32)
    for h in range(DN_HEADS):
        q = qkvc[:, h * DN_D:(h + 1) * DN_D]
        k = qkvc[:, DN_W + h * DN_D:DN_W + (h + 1) * DN_D]
        v = qkvc[:, 2 * DN_W + h * DN_D:2 * DN_W + (h + 1) * DN_D]
        q = q * lax.rsqrt(jnp.sum(q * q, axis=-1, keepdims=True) + EPS) * (DN_D ** -0.5)
        k = k * lax.rsqrt(jnp.sum(k * k, axis=-1, keepdims=True) + EPS)
        qk = _dot_nt(_bf(q), _bf(k))
        o_rows = []
        for b in range(STEP_TILE):
            kb, qb_, vb = k[b:b + 1], q[b:b + 1], v[b:b + 1]
            beta = beta_all[b:b + 1, h:h + 1]
            eg = eg_all[b:b + 1, h:h + 1]
            s_old = s0_ref[b, h]
            ks_qs = _dot(_bf(jnp.concatenate([kb, qb_, pad6], axis=0)), _bf(s_old))
            u = beta * (vb - eg * ks_qs[0:1])
            o_rows.append(eg * ks_qs[1:2] + qk[b:b + 1, b:b + 1] * u)
            k_col = jnp.concatenate([kb, pad7], axis=0).T
            s_ref[b, h] = eg * s_old + _dot(_bf(k_col), _bf(jnp.concatenate([u, pad7], axis=0)))
        o = jnp.concatenate(o_rows, axis=0)
        sl = slice(h * DN_D, (h + 1) * DN_D)
        o_ref[:, sl] = _rms(o, nw_ref[...]) * _silu(z_ref[:, sl])


def _gdn_step(qkv2, z2, ba2, conv_state, s0, conv_w8, par, dn_norm):
    b = qkv2.shape[0]
    assert b % STEP_TILE == 0
    row = lambda w: pl.BlockSpec((STEP_TILE, w), lambda i: (i, 0))
    st = pl.BlockSpec((STEP_TILE, DN_HEADS, DN_D, DN_D), lambda i: (i, 0, 0, 0))
    return pl.pallas_call(
        _gdn_step_kernel, grid=(b // STEP_TILE,),
        in_specs=[row(CONV_CH), row(DN_W), row(BA_W),
                  pl.BlockSpec((STEP_TILE, CONV_W - 1, CONV_CH), lambda i: (i, 0, 0)), st,
                  pl.BlockSpec((8, CONV_CH), lambda i: (0, 0)),
                  pl.BlockSpec((8, LANES), lambda i: (0, 0)),
                  pl.BlockSpec((1, DN_D), lambda i: (0, 0))],
        out_specs=[row(DN_W), st],
        out_shape=[jax.ShapeDtypeStruct((b, DN_W), F32),
                   jax.ShapeDtypeStruct((b, DN_HEADS, DN_D, DN_D), F32)],
        compiler_params=_cparams(1), name="gdn_step")(qkv2, z2, ba2, conv_state, s0, conv_w8, par, dn_norm)


def _out_proj_kernel(pool_ref, dn_ref, x_ref, gt_ref, sh_ref, sc_ref, nw_ref, w_ref, x1_ref, h2_ref):
    mix = _dot(pool_ref[...].astype(BF16), w_ref[:POOL_W, :]) + _dot(dn_ref[...].astype(BF16), w_ref[POOL_W:, :])
    x1 = x_ref[...] + gt_ref[0] * mix
    x1_ref[...] = x1
    h2_ref[...] = _rms(x1, nw_ref[...]) * (1.0 + sc_ref[0]) + sh_ref[0]


def _out_proj(pool2, dn2, x2d, mod3, nw, w_out_b, tm, tiles_per_batch):
    n, d = x2d.shape
    rows = mod3.shape[1]
    tok = lambda w: pl.BlockSpec((tm, w), lambda i: (i, 0))
    return pl.pallas_call(
        _out_proj_kernel, grid=(n // tm,),
        in_specs=[tok(POOL_W), tok(DN_W), tok(d),
                  _mod_spec(rows, d, tiles_per_batch, 2),
                  _mod_spec(rows, d, tiles_per_batch, 3),
                  _mod_spec(rows, d, tiles_per_batch, 4),
                  pl.BlockSpec((1, d), lambda i: (0, 0)),
                  pl.BlockSpec((POOL_W + DN_W, d), lambda i: (0, 0))],
        out_specs=[tok(d), tok(d)],
        out_shape=[jax.ShapeDtypeStruct((n, d), F32)] * 2,
        compiler_params=_cparams(1), name="out_proj")(pool2, dn2, x2d, mod3, mod3, mod3, nw, w_out_b)


def _router_kernel(hp_ref, hs_ref, rw_ref, rb_ref, eidx_ref, wts_ref, rank_ref, cnt_ref, run, *, tm, np_tiles):
    i = pl.program_id(0)

    @pl.when(i == 0)
    def _():
        run[...] = jnp.zeros_like(run)

    h = jnp.where(i < np_tiles, hp_ref[...], hs_ref[...])
    s = _sigmoid(_dot(h.astype(BF16), rw_ref[...]))
    sel = s + rb_ref[...]
    lane_i = lax.broadcasted_iota(I32, (tm, N_EXPERTS), 1)
    lane = lane_i.astype(F32)
    grp = lane_i >> GROUP_SHIFT
    neg = -jnp.inf

    def first_max(x):
        m = jnp.max(x, axis=-1, keepdims=True)
        idx = jnp.min(jnp.where(x == m, lane, float(N_EXPERTS)), axis=-1, keepdims=True)
        return m, idx

    gs = []
    for g in range(N_GROUPS):
        xg = jnp.where(grp == g, sel, neg)
        m1, i1 = first_max(xg)
        m2 = jnp.max(jnp.where(lane == i1, neg, xg), axis=-1, keepdims=True)
        gs.append(m1 + m2)
    cur = jnp.full((tm, N_EXPERTS), neg, F32)
    for g in range(N_GROUPS):
        ahead = jnp.zeros((tm, 1), F32)
        for o in range(N_GROUPS):
            if o < g:
                ahead = ahead + jnp.where(gs[o] >= gs[g], 1.0, 0.0)
            elif o > g:
                ahead = ahead + jnp.where(gs[o] > gs[g], 1.0, 0.0)
        cur = jnp.where((grp == g) & (ahead < TOPK_GROUPS), sel, cur)

    hits, idxs, ws = [], [], []
    chosen = jnp.zeros((tm, N_EXPERTS), F32)
    for _ in range(TOP_K):
        _, idx = first_max(cur)
        hit = lane == idx
        ws.append(jnp.sum(jnp.where(hit, s, 0.0), axis=-1, keepdims=True))
        cur = jnp.where(hit, neg, cur)
        chosen = jnp.where(hit, 1.0, chosen)
        hits.append(hit)
        idxs.append(idx)
    onehot = chosen
    r = lax.broadcasted_iota(I32, (tm, tm), 0)
    c = lax.broadcasted_iota(I32, (tm, tm), 1)
    earlier = jnp.where(r > c, 1.0, 0.0).astype(BF16)
    before = _dot(earlier, onehot.astype(BF16)) + run[...]
    run[...] = run[...] + jnp.sum(onehot, axis=0, keepdims=True)

    wsum = ws[0]
    for w in ws[1:]:
        wsum = wsum + w
    out_lane = lax.broadcasted_iota(I32, (tm, LANES), 1)
    eidx = jnp.zeros((tm, LANES), F32)
    wts = jnp.zeros((tm, LANES), F32)
    rank = jnp.zeros((tm, LANES), F32)
    for k in range(TOP_K):
        rk = jnp.sum(jnp.where(hits[k], before, 0.0), axis=-1, keepdims=True)
        eidx = jnp.where(out_lane == k, idxs[k], eidx)
        wts = jnp.where(out_lane == k, ws[k] / wsum * ROUTED_SCALE, wts)
        rank = jnp.where(out_lane == k, rk, rank)
    eidx_ref[...] = eidx.astype(I32)
    wts_ref[...] = wts
    rank_ref[...] = rank.astype(I32)

    @pl.when(i == pl.num_programs(0) - 1)
    def _():
        cnt_ref[...] = run[...]


def _token_specs(n_p, tm, d_block, index_rest):
    np_tiles = n_p // tm
    return (pl.BlockSpec((tm,) + d_block, lambda i, *_: (jnp.minimum(i, np_tiles - 1),) + index_rest),
            pl.BlockSpec((tm,) + d_block, lambda i, *_: (jnp.maximum(i - np_tiles, 0),) + index_rest))


def _router(h_p, h_s, router_w_b, router_bias, tm):
    (n_p, d), n_s = h_p.shape, h_s.shape[0]
    n_pad = n_p + n_s
    kern = functools.partial(_router_kernel, tm=tm, np_tiles=n_p // tm)
    tok = pl.BlockSpec((tm, LANES), lambda i: (i, 0))
    return pl.pallas_call(
        kern, grid=(n_pad // tm,),
        in_specs=[*_token_specs(n_p, tm, (d,), (0,)),
                  pl.BlockSpec((d, N_EXPERTS), lambda i: (0, 0)),
                  pl.BlockSpec((1, N_EXPERTS), lambda i: (0, 0))],
        out_specs=[tok, tok, tok, pl.BlockSpec((1, N_EXPERTS), lambda i: (0, 0))],
        out_shape=[jax.ShapeDtypeStruct((n_pad, LANES), I32),
                   jax.ShapeDtypeStruct((n_pad, LANES), F32),
                   jax.ShapeDtypeStruct((n_pad, LANES), I32),
                   jax.ShapeDtypeStruct((1, N_EXPERTS), F32)],
        scratch_shapes=[pltpu.VMEM((1, N_EXPERTS), F32)],
        compiler_params=_cparams(1), name="router")(h_p, h_s, router_w_b, router_bias)


SUBLANES = 8


def _gather_rows(idx_ref, src_hbm, dst, sem):
    def body(g, carry):
        for j in range(SUBLANES):
            tok = idx_ref[0, 0, g * SUBLANES + j]
            pltpu.make_async_copy(src_hbm.at[pl.ds(tok, 1)], dst.at[g, pl.ds(j, 1)], sem).start(priority=j % 2)
        return carry
    lax.fori_loop(0, dst.shape[0], body, 0)


def _wait_rows(dst, sem):
    pltpu.make_async_copy(dst, dst, sem).wait()


def _dest_kernel(eidx_ref, rank_ref, ps_ref, dest_ref, *, tm):
    lane = lax.broadcasted_iota(I32, (tm, N_EXPERTS), 1)
    out_lane = lax.broadcasted_iota(I32, (tm, LANES), 1)
    eidx = eidx_ref[...]
    rank = rank_ref[...].astype(F32)
    dest = jnp.zeros((tm, LANES), F32)
    for k in range(TOP_K):
        start = jnp.sum(jnp.where(lane == eidx[:, k:k + 1], ps_ref[...], 0.0), axis=-1, keepdims=True)
        dest = jnp.where(out_lane == k, start + rank[:, k:k + 1], dest)
    dest_ref[...] = dest.astype(I32)


def _dest(eidx, rank, pstarts, tm):
    n = eidx.shape[0]
    tok = pl.BlockSpec((tm, LANES), lambda i: (i, 0))
    return pl.pallas_call(
        functools.partial(_dest_kernel, tm=tm), grid=(n // tm,),
        in_specs=[tok, tok, pl.BlockSpec((1, N_EXPERTS), lambda i: (0, 0))],
        out_specs=tok, out_shape=jax.ShapeDtypeStruct((n, LANES), I32),
        compiler_params=_cparams(1), name="dest")(eidx, rank, pstarts)


def _dispatch_kernel(clear_ref, dest_ref, hp_ref, hs_ref, xs_hbm, pbuf, zbuf, sems, zsem, *, tm, nb, np_tiles):
    i = pl.program_id(0)
    n = pl.num_programs(0)
    slot = i % 2

    def zero_copy(blk):
        return pltpu.make_async_copy(zbuf, xs_hbm.at[pl.ds(blk * ROW_BLOCK, ROW_BLOCK)], zsem)

    def wait_slot(s):
        for _ in range(TOP_K):
            _wait_rows(pbuf.at[s], sems.at[s])

    @pl.when(i == 0)
    def _():
        zbuf[...] = jnp.zeros_like(zbuf)

        def start(blk, carry):
            @pl.when(clear_ref[blk] > 0)
            def _():
                zero_copy(blk).start()
            return carry

        def wait(blk, carry):
            @pl.when(clear_ref[blk] > 0)
            def _():
                zero_copy(blk).wait()
            return carry
        lax.fori_loop(0, nb, start, 0)
        lax.fori_loop(0, nb, wait, 0)

    @pl.when(i >= 2)
    def _():
        wait_slot(slot)

    pbuf[slot] = jnp.where(i < np_tiles, hp_ref[...], hs_ref[...])

    def body(g, carry):
        for j in range(SUBLANES):
            for k in range(TOP_K):
                row = dest_ref[0, 0, (g * SUBLANES + j) * TOP_K + k]
                pltpu.make_async_copy(pbuf.at[slot, g, pl.ds(j, 1)], xs_hbm.at[pl.ds(row, 1)],
                                      sems.at[slot]).start(priority=k % 2)
        return carry
    lax.fori_loop(0, tm // SUBLANES, body, 0)

    @pl.when(i == n - 1)
    def _():
        wait_slot(slot)

    @pl.when(jnp.logical_and(i == n - 1, n > 1))
    def _():
        wait_slot(1 - slot)


def _dispatch(clear, dest3, h_p, h_s, tm):
    (n_p, d), n_s = h_p.shape, h_s.shape[0]
    nb = clear.shape[0]
    groups = tm // SUBLANES
    rows3 = lambda h: h.reshape(h.shape[0] // SUBLANES, SUBLANES, d)
    grid_spec = pltpu.PrefetchScalarGridSpec(
        num_scalar_prefetch=1, grid=((n_p + n_s) // tm,),
        in_specs=[pl.BlockSpec((1, 1, tm * TOP_K), lambda i, cl: (i, 0, 0), memory_space=pltpu.SMEM),
                  *_token_specs(n_p // SUBLANES, groups, (SUBLANES, d), (0, 0))],
        out_specs=pl.BlockSpec(memory_space=pl.ANY),
        scratch_shapes=[pltpu.VMEM((2, groups, SUBLANES, d), F32), pltpu.VMEM((ROW_BLOCK, d), F32),
                        pltpu.SemaphoreType.DMA((2,)), pltpu.SemaphoreType.DMA(())])
    return pl.pallas_call(
        functools.partial(_dispatch_kernel, tm=tm, nb=nb, np_tiles=n_p // tm), grid_spec=grid_spec,
        out_shape=jax.ShapeDtypeStruct((nb * ROW_BLOCK, d), F32),
        compiler_params=_cparams(1), name="dispatch")(clear, dest3, rows3(h_p), rows3(h_s))


def _experts_kernel(be_ref, nact_ref, x_ref, g_hbm, u_hbm, d_hbm, y_ref, gf, uf, df, gb, ub, db, sems, slot_ref,
                    *, base, nb):
    i = pl.program_id(0)
    nact = nact_ref[0]

    def weight_copies(e, s):
        return (pltpu.make_async_copy(g_hbm.at[base + e], gf.at[s], sems.at[s]),
                pltpu.make_async_copy(u_hbm.at[base + e], uf.at[s], sems.at[s]),
                pltpu.make_async_copy(d_hbm.at[base + e], df.at[s], sems.at[s]))

    @pl.when(jnp.logical_and(i == 0, nact > 0))
    def _():
        slot_ref[0] = 0
        for c in weight_copies(be_ref[0], 0):
            c.start()

    first = jnp.logical_or(i == 0, be_ref[i] != be_ref[jnp.maximum(i - 1, 0)])

    @pl.when(jnp.logical_and(i < nact, first))
    def _():
        s = slot_ref[0]
        here = be_ref[i]
        nxt = lax.while_loop(lambda j: jnp.logical_and(j < nact, be_ref[jnp.minimum(j, nb - 1)] == here),
                             lambda j: j + 1, i + 1)

        @pl.when(nxt < nact)
        def _():
            for c in weight_copies(be_ref[jnp.minimum(nxt, nb - 1)], 1 - s):
                c.start()
        for c in weight_copies(here, s):
            c.wait()
        gb[...] = gf[s].astype(BF16)
        ub[...] = uf[s].astype(BF16)
        db[...] = df[s].astype(BF16)
        slot_ref[0] = 1 - s

    @pl.when(i < nact)
    def _():
        x = x_ref[...].astype(BF16)
        hid = _silu(_dot(x, gb[...])) * _dot(x, ub[...])
        y_ref[...] = _dot(hid.astype(BF16), db[...])

    @pl.when(i >= nact)
    def _():
        y_ref[...] = jnp.zeros_like(y_ref)


def _experts(block_e, nact, x_sorted, exp_gate, exp_up, exp_down, layer):
    nb = x_sorted.shape[0] // ROW_BLOCK
    _, d, de = exp_gate.shape
    base = layer * N_EXPERTS
    grid_spec = pltpu.PrefetchScalarGridSpec(
        num_scalar_prefetch=2, grid=(nb,),
        in_specs=[pl.BlockSpec((ROW_BLOCK, d),
                               lambda i, be, na: (jnp.minimum(i, jnp.maximum(na[0] - 1, 0)), 0)),
                  pl.BlockSpec(memory_space=pl.ANY), pl.BlockSpec(memory_space=pl.ANY),
                  pl.BlockSpec(memory_space=pl.ANY)],
        out_specs=pl.BlockSpec((ROW_BLOCK, d), lambda i, be, na: (i, 0)),
        scratch_shapes=[pltpu.VMEM((2, d, de), F32), pltpu.VMEM((2, d, de), F32), pltpu.VMEM((2, de, d), F32),
                        pltpu.VMEM((d, de), BF16), pltpu.VMEM((d, de), BF16), pltpu.VMEM((de, d), BF16),
                        pltpu.SemaphoreType.DMA((2,)), pltpu.SMEM((1,), I32)])
    return pl.pallas_call(
        functools.partial(_experts_kernel, base=base, nb=nb), grid_spec=grid_spec,
        out_shape=jax.ShapeDtypeStruct((nb * ROW_BLOCK, d), F32),
        compiler_params=_cparams(1), name="experts")(block_e, nact, x_sorted, exp_gate, exp_up, exp_down)


def _combine_kernel(cur_ref, nxt_ref, y_hbm, wts_ref, x1_ref, h_ref, gt_ref, sg_ref, su_ref, sd_ref,
                    nf_ref, x2_ref, yn_ref, ybuf, sems, *, tm):
    i = pl.program_id(0)
    n = pl.num_programs(0)
    slot = i % 2
    groups = tm // SUBLANES

    @pl.when(i == 0)
    def _():
        _gather_rows(cur_ref, y_hbm, ybuf.at[0], sems.at[0])

    @pl.when(i + 1 < n)
    def _():
        _gather_rows(nxt_ref, y_hbm, ybuf.at[1 - slot], sems.at[1 - slot])

    hb = h_ref[...].astype(BF16)
    hid = _silu(_dot(hb, sg_ref[...])) * _dot(hb, su_ref[...])
    acc = _dot(hid.astype(BF16), sd_ref[...])
    _wait_rows(ybuf.at[slot], sems.at[slot])
    wts = wts_ref[...]
    for k in range(TOP_K):
        y_k = ybuf[slot, k * groups:(k + 1) * groups].reshape(tm, acc.shape[1])
        acc = acc + y_k * wts[:, k:k + 1]
    x2 = x1_ref[...] + gt_ref[0] * acc
    x2_ref[...] = x2
    yn_ref[...] = _rms(x2, nf_ref[...])


def _combine(dest3, y_sorted, wts, x1, h2, mod3, sh_gate_b, sh_up_b, sh_down_b, norm_final, tm, tiles_per_batch):
    n, d = x1.shape
    nt = n // tm
    rows = mod3.shape[1]
    ds = sh_gate_b.shape[1]
    tok = lambda w: pl.BlockSpec((tm, w), lambda i: (i, 0))
    kern = functools.partial(_combine_kernel, tm=tm)
    return pl.pallas_call(
        kern, grid=(nt,),
        in_specs=[pl.BlockSpec((1, 1, tm * TOP_K), lambda i: (i, 0, 0), memory_space=pltpu.SMEM),
                  pl.BlockSpec((1, 1, tm * TOP_K), lambda i: (jnp.minimum(i + 1, nt - 1), 0, 0),
                               memory_space=pltpu.SMEM),
                  pl.BlockSpec(memory_space=pl.ANY),
                  tok(LANES), tok(d), tok(d),
                  _mod_spec(rows, d, tiles_per_batch, 5),
                  pl.BlockSpec((d, ds), lambda i: (0, 0)),
                  pl.BlockSpec((d, ds), lambda i: (0, 0)),
                  pl.BlockSpec((ds, d), lambda i: (0, 0)),
                  pl.BlockSpec((1, d), lambda i: (0, 0))],
        out_specs=[tok(d), tok(d)],
        out_shape=[jax.ShapeDtypeStruct((n, d), F32)] * 2,
        scratch_shapes=[pltpu.VMEM((2, tm * TOP_K // SUBLANES, SUBLANES, d), F32),
                        pltpu.SemaphoreType.DMA((2,))],
        compiler_params=_cparams(1), name="combine")(
            dest3, dest3, y_sorted, wts, x1, h2, mod3, sh_gate_b, sh_up_b, sh_down_b, norm_final)


def _last_rows(buf, new, keep):
    t = new.shape[1]
    if t >= keep:
        return new[:, t - keep:]
    return jnp.concatenate([buf[:, t:], new], axis=1)


def _mixers(u, qkv, z, ba, b, t, pool_buf, conv_buf, s0, start_pos, lw):
    t_pool = -(-t // 8) * 8
    t_gdn = -(-t // CHUNK) * CHUNK
    u3 = u.reshape(b, t, POOL_W)
    qkv3 = qkv.reshape(b, t, CONV_CH)
    pad_t = lambda a, tp: a if tp == t else jnp.pad(a, ((0, 0), (0, tp - t), (0, 0)))
    buf16 = jnp.pad(pool_buf, ((0, 0), (POOL_HALO - POOL_BUF, 0), (0, 0)))
    pool_out = _pool(pad_t(u3, t_pool), buf16, lw["pool_w"], lw["pool_scale"], start_pos)[:, :t]
    if t == 1:
        dn_out, s_new = _gdn_step(qkv, z, ba, conv_buf, s0, lw["conv_w"], lw["par"], lw["dn_norm"])
    else:
        cbuf8 = jnp.pad(conv_buf, ((0, 0), (CONV_HALO - (CONV_W - 1), 0), (0, 0)))
        dn_out, s_new = _gdn(pad_t(qkv3, t_gdn), pad_t(z.reshape(b, t, DN_W), t_gdn),
                             pad_t(ba.reshape(b, t, BA_W), t_gdn), cbuf8, s0, lw["conv_w"], lw["par"],
                             lw["dn_norm"], t)
        dn_out = dn_out[:, :t].reshape(b * t, DN_W)
    new_pool = _last_rows(pool_buf, u3, POOL_BUF)
    new_conv = _last_rows(conv_buf, qkv3, CONV_W - 1)
    return pool_out.reshape(b * t, POOL_W), dn_out, new_pool, new_conv, s_new


def _block_tables(counts, n_tok):
    nk = n_tok * TOP_K
    nb = -(-(nk + N_EXPERTS * (ROW_BLOCK - 1)) // ROW_BLOCK)
    pcounts = (counts + ROW_BLOCK - 1) // ROW_BLOCK * ROW_BLOCK
    pends = jnp.cumsum(pcounts)
    pstarts = pends - pcounts
    first_row = jnp.arange(nb, dtype=I32)[:, None] * ROW_BLOCK
    block_e = jnp.minimum(jnp.sum((pends[None, :] <= first_row).astype(I32), axis=1), N_EXPERTS - 1)
    nact = (pends[-1] // ROW_BLOCK).astype(I32).reshape(1)
    blk = jnp.arange(nb, dtype=I32)
    next_e = jnp.concatenate([block_e[1:], jnp.full((1,), -1, I32)])
    clear = ((blk >= nact[0] - 1) | (next_e != block_e)).astype(I32)
    return pstarts.astype(F32).reshape(1, N_EXPERTS), block_e, nact, clear


def _dest_tiles(dest, tm, k_major):
    n = dest.shape[0]
    tiles = dest.reshape(n // tm, tm, TOP_K)
    if k_major:
        tiles = tiles.transpose(0, 2, 1)
    return tiles.reshape(n // tm, 1, tm * TOP_K)


def kernel(x_prompt, x_sample, state_pool, state_conv, state_delta, c_prompt, c_sample, norm_mix, norm_ffn, w_ada, b_ada, w_in, pool_w, pool_scale, conv_w, a_log, dt_bias, dn_norm, w_out, router_w, router_bias, exp_gate, exp_up, exp_down, sh_gate, sh_up, sh_down, norm_final):
    bp, tp, d = x_prompt.shape
    bs, ts, _ = x_sample.shape
    depth = w_ada.shape[0]
    past_len = PAST_LEN
    n_p, n_s = bp * tp, bs * ts
    n_tok = n_p + n_s
    tm_p = min(512, tp)
    tm_c = min(128, tp)
    tm_r = next(t for t in (128, 64, 32, 16, 8) if n_p % t == 0 and n_s % t == 0)
    assert tp % tm_p == 0 and ts == 1 and bs % 8 == 0
    n_exp = exp_gate.shape[1]
    gate_w = exp_gate.reshape((depth * n_exp,) + exp_gate.shape[2:])
    up_w = exp_up.reshape((depth * n_exp,) + exp_up.shape[2:])
    down_w = exp_down.reshape((depth * n_exp,) + exp_down.shape[2:])

    c_all = jnp.concatenate([c_prompt, c_sample], axis=0)
    m_rows = -(-c_all.shape[0] // 8) * 8
    mod_all = _ada(jnp.pad(c_all, ((0, m_rows - c_all.shape[0]), (0, 0))), w_ada, b_ada)

    c2 = POOL_W + CONV_CH + DN_W
    lane_pad = jnp.zeros((depth, d, LANES - DN_HEADS), F32)
    w_in_p = jnp.concatenate([w_in[:, :, :c2], w_in[:, :, c2:c2 + DN_HEADS], lane_pad,
                              w_in[:, :, c2 + DN_HEADS:], lane_pad], axis=-1).astype(BF16)
    head_pad = lambda a: jnp.pad(a, ((0, 0), (0, LANES - DN_HEADS)))
    par_all = jnp.stack([head_pad(a_log), head_pad(dt_bias)] + [jnp.zeros((depth, LANES), F32)] * 6, axis=1)
    conv_w8 = jnp.pad(conv_w, ((0, 0), (0, 8 - CONV_W), (0, 0)))

    xp = x_prompt.reshape(n_p, d)
    xs = x_sample.reshape(n_s, d)
    zero_pool = jnp.zeros((bp, POOL_BUF, POOL_W), F32)
    zero_conv = jnp.zeros((bp, CONV_W - 1, CONV_CH), F32)
    zero_state = jnp.zeros((bp, DN_HEADS, DN_D, DN_D), F32)
    outs = {k: [] for k in ("pool_p", "conv_p", "delta_p", "pool_s", "conv_s", "delta_s")}
    yp = ys = None
    for l in range(depth):
        lw = {"pool_w": pool_w[l].astype(BF16), "pool_scale": pool_scale[l].reshape(1, POOL_W),
              "conv_w": conv_w8[l], "par": par_all[l], "dn_norm": dn_norm[l].reshape(1, DN_D)}
        mod_p = mod_all[l, :bp].reshape(bp, 1, 6 * d)
        mod_s = mod_all[l, bp:bp + bs].reshape(1, bs, 6 * d)
        nw_mix = norm_mix[l].reshape(1, d)
        nw_ffn = norm_ffn[l].reshape(1, d)
        w_out_b = w_out[l].astype(BF16)

        up, qkvp, zp, bap = _norm_proj(xp, mod_p, nw_mix, w_in_p[l], tm_p, tp // tm_p)
        us, qkvs, zs, bas = _norm_proj(xs, mod_s, nw_mix, w_in_p[l], n_s, 1)
        pool_p, dn_p, npool_p, nconv_p, ns_p = _mixers(up, qkvp, zp, bap, bp, tp, zero_pool, zero_conv,
                                                       zero_state, 0, lw)
        pool_s, dn_s, npool_s, nconv_s, ns_s = _mixers(us, qkvs, zs, bas, bs, ts, state_pool[l], state_conv[l],
                                                       state_delta[l], past_len, lw)
        for key, val in (("pool_p", npool_p), ("conv_p", nconv_p), ("delta_p", ns_p),
                         ("pool_s", npool_s), ("conv_s", nconv_s), ("delta_s", ns_s)):
            outs[key].append(val)

        x1p, h2p = _out_proj(pool_p, dn_p, xp, mod_p, nw_ffn, w_out_b, tm_p, tp // tm_p)
        x1s, h2s = _out_proj(pool_s, dn_s, xs, mod_s, nw_ffn, w_out_b, n_s, 1)

        eidx, wts, rank, counts = _router(h2p, h2s, router_w[l].astype(BF16),
                                          router_bias[l].reshape(1, N_EXPERTS), tm_r)
        pstarts, block_e, nact, clear = _block_tables(counts.reshape(-1).astype(I32), n_tok)
        dest = _dest(eidx, rank, pstarts, tm_r)[:, :TOP_K]
        x_sorted = _dispatch(clear, _dest_tiles(dest, tm_r, False), h2p, h2s, tm_r)
        y_sorted = _experts(block_e, nact, x_sorted, gate_w, up_w, down_w, l)

        shg, shu, shd = sh_gate[l].astype(BF16), sh_up[l].astype(BF16), sh_down[l].astype(BF16)
        nf = norm_final.reshape(1, d)
        xp, yp = _combine(_dest_tiles(dest[:n_p], tm_c, True), y_sorted, wts[:n_p], x1p, h2p, mod_p, shg, shu, shd,
                          nf, tm_c, tp // tm_c)
        xs, ys = _combine(_dest_tiles(dest[n_p:], n_s, True), y_sorted, wts[n_p:], x1s, h2s, mod_s, shg, shu, shd,
                          nf, n_s, 1)

    stack = lambda key: jnp.stack(outs[key])
    return (yp.reshape(bp, tp, d), ys.reshape(bs, ts, d), stack("pool_p"), stack("conv_p"), stack("delta_p"),
            stack("pool_s"), stack("conv_s"), stack("delta_s"))
```

```python
import functools

import jax
import jax.numpy as jnp
from jax import lax
from jax.experimental import pallas as pl
from jax.experimental.pallas import tpu as pltpu

F32, BF16, I32 = jnp.float32, jnp.bfloat16, jnp.int32
HI = lax.Precision.HIGHEST
EPS = 1e-6

LANES = 128
POOL_WINDOWS = (2, 4, 8, 16)
POOL_GW = 128
POOL_W = 512
POOL_BUF = 15
POOL_HALO = 16
DN_HEADS = 4
DN_D = 128
DN_W = 512
CONV_W = 4
CONV_CH = 3 * DN_W
CONV_HALO = 8
CHUNK = 64
INV_BLOCK = 16
GDN_ROWS = 4
N_EXPERTS = 256
TOP_K = 8
N_GROUPS = 8
TOPK_GROUPS = 4
GROUP_SHIFT = 5
ROUTED_SCALE = 2.5
PAST_LEN = 16384
ROW_BLOCK = 256
BA_W = 2 * LANES
D_IN_PAD = POOL_W + CONV_CH + DN_W + BA_W
VMEM_LIMIT = 48 * 2 ** 20


def _cparams(n_axes):
    return pltpu.CompilerParams(dimension_semantics=("arbitrary",) * n_axes,
                                vmem_limit_bytes=VMEM_LIMIT)


def _sigmoid(x):
    return 1.0 / (1.0 + jnp.exp(-x))


def _silu(x):
    return x * _sigmoid(x)


def _softplus(x):
    return jnp.maximum(x, 0.0) + jnp.log(1.0 + jnp.exp(-jnp.abs(x)))


def _rms(x, w):
    return x * lax.rsqrt(jnp.mean(x * x, axis=-1, keepdims=True) + EPS) * w


def _dot(a, b, precision=None):
    return jnp.dot(a, b, preferred_element_type=F32, precision=precision)


def _dot_nt(a, b, precision=None):
    return lax.dot_general(a, b, (((1,), (1,)), ((), ())), preferred_element_type=F32,
                           precision=precision)


def _ada_kernel(c_ref, w_ref, b_ref, o_ref):
    o_ref[0] = _dot(_silu(c_ref[...]), w_ref[0], HI) + b_ref[0]


def _ada(c_all, w_ada, b_ada):
    depth, d, d6 = w_ada.shape
    m = c_all.shape[0]
    tn = 1024
    return pl.pallas_call(
        _ada_kernel, grid=(depth, d6 // tn),
        in_specs=[pl.BlockSpec((m, d), lambda l, j: (0, 0)),
                  pl.BlockSpec((1, d, tn), lambda l, j: (l, 0, j)),
                  pl.BlockSpec((1, 1, tn), lambda l, j: (l, 0, j))],
        out_specs=pl.BlockSpec((1, m, tn), lambda l, j: (l, 0, j)),
        out_shape=jax.ShapeDtypeStruct((depth, m, d6), F32),
        compiler_params=_cparams(2), name="ada")(c_all, w_ada, b_ada.reshape(depth, 1, d6))


def _mod_spec(rows, d, tiles_per_batch, chunk):
    return pl.BlockSpec((1, rows, d), lambda i: (i // tiles_per_batch, 0, chunk))


def _norm_proj_kernel(x_ref, sh_ref, sc_ref, nw_ref, w_ref, u_ref, qkv_ref, z_ref, ba_ref):
    h = _rms(x_ref[...], nw_ref[...]) * (1.0 + sc_ref[0]) + sh_ref[0]
    hb = h.astype(BF16)
    c0, c1, c2 = POOL_W, POOL_W + CONV_CH, POOL_W + CONV_CH + DN_W
    u_ref[...] = _dot(hb, w_ref[:, :c0])
    qkv_ref[...] = _dot(hb, w_ref[:, c0:c1])
    z_ref[...] = _dot(hb, w_ref[:, c1:c2])
    ba_ref[...] = _dot(hb, w_ref[:, c2:])


def _norm_proj(x2d, mod3, nw, w_in_p, tm, tiles_per_batch):
    n, d = x2d.shape
    rows = mod3.shape[1]
    widths = (POOL_W, CONV_CH, DN_W, BA_W)
    return pl.pallas_call(
        _norm_proj_kernel, grid=(n // tm,),
        in_specs=[pl.BlockSpec((tm, d), lambda i: (i, 0)),
                  _mod_spec(rows, d, tiles_per_batch, 0),
                  _mod_spec(rows, d, tiles_per_batch, 1),
                  pl.BlockSpec((1, d), lambda i: (0, 0)),
                  pl.BlockSpec((d, D_IN_PAD), lambda i: (0, 0))],
        out_specs=[pl.BlockSpec((tm, w), lambda i: (i, 0)) for w in widths],
        out_shape=[jax.ShapeDtypeStruct((n, w), F32) for w in widths],
        compiler_params=_cparams(1), name="norm_proj")(x2d, mod3, mod3, nw, w_in_p)


def _pool_kernel(u_ref, buf_ref, pw_ref, ps_ref, o_ref, ext, *, tt, gb, start_pos):
    t = pl.program_id(1)

    @pl.when(t == 0)
    def _():
        ext[:, 0:POOL_HALO, :] = buf_ref[...]

    @pl.when(t > 0)
    def _():
        ext[:, 0:POOL_HALO, :] = ext[:, tt:tt + POOL_HALO, :]

    ext[:, POOL_HALO:POOL_HALO + tt, :] = u_ref[...]
    pos = start_pos + t * tt + lax.broadcasted_iota(I32, (tt, 1), 0)
    for g, win in enumerate(POOL_WINDOWS):
        sl = slice(g * POOL_GW, (g + 1) * POOL_GW)
        cnt = jnp.minimum(pos + 1, win).astype(F32)
        pooled = []
        for rb in range(gb):
            u = u_ref[rb, :, sl]
            s = u
            for j in range(1, win):
                s = s + ext[rb, POOL_HALO - j:POOL_HALO - j + tt, sl]
            pooled.append(s / cnt - u)
        mixed = _dot(jnp.concatenate(pooled, axis=0).astype(BF16), pw_ref[g]) * ps_ref[:, sl]
        for rb in range(gb):
            o_ref[rb, :, sl] = mixed[rb * tt:(rb + 1) * tt]


def _pool(u3, buf16, pool_w_b, pool_scale, start_pos):
    b, t, w = u3.shape
    tt = min(t, 256)
    assert t % tt == 0 and tt % 8 == 0
    gb = next(g for g in (16, 8, 4, 2, 1) if b % g == 0 and g * tt <= 256)
    return pl.pallas_call(
        functools.partial(_pool_kernel, tt=tt, gb=gb, start_pos=start_pos), grid=(b // gb, t // tt),
        in_specs=[pl.BlockSpec((gb, tt, w), lambda i, j: (i, j, 0)),
                  pl.BlockSpec((gb, POOL_HALO, w), lambda i, j: (i, 0, 0)),
                  pl.BlockSpec((len(POOL_WINDOWS), POOL_GW, POOL_GW), lambda i, j: (0, 0, 0)),
                  pl.BlockSpec((1, w), lambda i, j: (0, 0))],
        out_specs=pl.BlockSpec((gb, tt, w), lambda i, j: (i, j, 0)),
        out_shape=jax.ShapeDtypeStruct((b, t, w), F32),
        scratch_shapes=[pltpu.VMEM((gb, POOL_HALO + tt, w), F32)],
        compiler_params=_cparams(2), name="pool")(u3, buf16, pool_w_b, pool_scale)


def _bf(x):
    return x.astype(BF16)


def _dot3(a, b):
    a_hi, b_hi = _bf(a), _bf(b)
    a_lo, b_lo = _bf(a - a_hi.astype(F32)), _bf(b - b_hi.astype(F32))
    return _dot(a_hi, b_hi) + (_dot(a_hi, b_lo) + _dot(a_lo, b_hi))


def _each(fn, *lists):
    return [fn(*args) for args in zip(*lists)]


def _unit_lower_inverse(a, row, col, eye):
    same = (row // INV_BLOCK) == (col // INV_BLOCK)
    dm = _each(lambda x: jnp.where(same, -x, 0.0), a)
    lm = _each(lambda x, d: -x - d, a, dm)
    mm = lambda x, y: _dot(_bf(x), _bf(y))
    d2 = _each(mm, dm, dm)
    d4 = _each(mm, d2, d2)
    d8 = _each(mm, d4, d4)
    p = _each(lambda x, y: mm(eye + x, eye + y), dm, d2)
    p2 = _each(lambda x, y: mm(eye + x, eye + y), d4, d8)
    p = _each(mm, p, p2)
    m = _each(mm, p, lm)
    m2 = _each(mm, m, m)
    q = _each(lambda x, y: mm(eye + x, eye + y), m, m2)
    t0 = _each(mm, q, p)
    at = _each(_dot3, a, t0)
    resid = _each(lambda t, x: (eye - t) - x, t0, at)
    return _each(lambda t, r: t + mm(t, r), t0, resid)


def _gdn_kernel(qkv_ref, z_ref, ba_ref, cbuf_ref, s0_ref, cw_ref, par_ref, nw_ref,
                o_ref, sfin_ref, state, xext, *, t_len, gb):
    c = pl.program_id(1)
    seqs = range(gb)
    heads = range(DN_HEADS)
    hc = DN_HEADS * CHUNK

    @pl.when(c == 0)
    def _():
        for rb in seqs:
            state[rb * DN_HEADS:(rb + 1) * DN_HEADS] = s0_ref[rb]
            xext[rb, 0:CONV_HALO, :] = cbuf_ref[rb]

    @pl.when(c > 0)
    def _():
        for rb in seqs:
            xext[rb, 0:CONV_HALO, :] = xext[rb, CHUNK:CHUNK + CONV_HALO, :]

    valid = (c * CHUNK + lax.broadcasted_iota(I32, (CHUNK, 1), 0)) < t_len
    vf = valid.astype(F32)
    head_lane = lax.broadcasted_iota(I32, (1, LANES), 1) < DN_HEADS
    decay = jnp.where(head_lane, -jnp.exp(par_ref[0:1, :]), 0.0)
    r64 = lax.broadcasted_iota(I32, (CHUNK, CHUNK), 0)
    c64 = lax.broadcasted_iota(I32, (CHUNK, CHUNK), 1)
    tri = (r64 >= c64).astype(F32)
    pick = (lax.broadcasted_iota(I32, (8, LANES), 0) == lax.broadcasted_iota(I32, (8, LANES), 1)).astype(F32)
    row = lax.broadcasted_iota(I32, (hc, hc), 0)
    col = lax.broadcasted_iota(I32, (hc, hc), 1)
    incl = ((row // CHUNK) == (col // CHUNK)) & (row >= col)
    strict = row > col
    eye = (row == col).astype(F32)
    base = CONV_HALO - (CONV_W - 1)

    def conv(rb):
        xext[rb, CONV_HALO:CONV_HALO + CHUNK, :] = qkv_ref[rb]
        acc = xext[rb, base:base + CHUNK, :] * cw_ref[0:1, :]
        for j in range(1, CONV_W):
            acc = acc + xext[rb, base + j:base + j + CHUNK, :] * cw_ref[j:j + 1, :]
        return _silu(acc)
    qkvc = [conv(rb) for rb in seqs]
    ba = [ba_ref[rb] for rb in seqs]
    beta_all = _each(lambda x: _sigmoid(x[:, :LANES]) * vf, ba)
    g_all = _each(lambda x: decay * _softplus(x[:, LANES:] + par_ref[1:2, :]) * vf, ba)
    gam_all = _each(lambda g: _dot(tri, g, HI), g_all)
    gam_rows = _each(lambda g: _dot_nt(pick, g, HI), gam_all)

    def unit(x):
        return x * lax.rsqrt(jnp.sum(x * x, axis=-1, keepdims=True) + EPS)
    q_s = _each(lambda x: jnp.concatenate(
        [unit(x[:, h * DN_D:(h + 1) * DN_D]) * (DN_D ** -0.5) * vf for h in heads], axis=0), qkvc)
    k_s = _each(lambda x: jnp.concatenate(
        [unit(x[:, DN_W + h * DN_D:DN_W + (h + 1) * DN_D]) * vf for h in heads], axis=0), qkvc)
    v_s = _each(lambda x: jnp.concatenate(
        [x[:, 2 * DN_W + h * DN_D:2 * DN_W + (h + 1) * DN_D] * vf for h in heads], axis=0), qkvc)
    beta = _each(lambda x: jnp.concatenate([x[:, h:h + 1] for h in heads], axis=0), beta_all)
    gam = _each(lambda x: jnp.concatenate([x[:, h:h + 1] for h in heads], axis=0), gam_all)
    gam_r = _each(lambda x: jnp.concatenate([x[h:h + 1, :] for h in heads], axis=1), gam_rows)

    dec = _each(lambda g, gr: jnp.where(incl, jnp.exp(jnp.where(incl, g - gr, 0.0)), 0.0), gam, gam_r)
    kb = _each(_bf, k_s)
    qb = _each(_bf, q_s)
    kk = _each(_dot_nt, kb, kb)
    qk = _each(_dot_nt, qb, kb)
    a_mat = _each(lambda b_, x, d_: b_ * x * jnp.where(strict, d_, 0.0), beta, kk, dec)
    qk = _each(lambda x, d_: _bf(x * d_), qk, dec)
    egam = _each(jnp.exp, gam)
    rhs = _each(lambda b_, e, k, v: jnp.concatenate([(b_ * e) * k, b_ * v], axis=-1),
                beta, egam, k_s, v_s)
    t_inv = _unit_lower_inverse(a_mat, row, col, eye)
    sol = _each(_dot3, t_inv, rhs)
    wb = _each(lambda x: _bf(x[:, :DN_D]), sol)
    uv = _each(lambda x: x[:, DN_D:], sol)

    hrows = [slice(h * CHUNK, (h + 1) * CHUNK) for h in heads]
    s_f = [[state[rb * DN_HEADS + h] for h in heads] for rb in seqs]
    s_b = [[_bf(x) for x in per_seq] for per_seq in s_f]
    ub = _each(lambda u_, w_, s_: _bf(jnp.concatenate(
        [u_[hrows[h]] - _dot(w_[hrows[h]], s_[h]) for h in heads], axis=0)), uv, wb, s_b)
    o_intra = _each(_dot, qk, ub)
    for h in heads:
        rows = hrows[h]
        sl = slice(h * DN_D, (h + 1) * DN_D)
        o = _each(lambda e, q_, s_, oi: e[rows] * _dot(q_[rows], s_[h]) + oi[rows], egam, qb, s_b, o_intra)
        g_last = _each(lambda g: g[(h + 1) * CHUNK - 1:(h + 1) * CHUNK, :], gam)
        k_dec = _each(lambda k, gl, g: _bf((k[rows] * jnp.exp(gl - g[rows])).T), k_s, g_last, gam)
        upd = _each(lambda kd, u_: _dot(kd, u_[rows]), k_dec, ub)
        for rb in seqs:
            state[rb * DN_HEADS + h] = jnp.exp(g_last[rb]) * s_f[rb][h] + upd[rb]
            o_ref[rb, :, sl] = _rms(o[rb], nw_ref[...]) * _silu(z_ref[rb, :, sl])

    @pl.when(c == pl.num_programs(1) - 1)
    def _():
        for rb in seqs:
            sfin_ref[rb] = state[rb * DN_HEADS:(rb + 1) * DN_HEADS]


def _gdn(qkv3, z3, ba3, cbuf8, s0, conv_w8, par, dn_norm, t_len):
    b, tp, _ = qkv3.shape
    assert tp % CHUNK == 0
    gb = GDN_ROWS if b % GDN_ROWS == 0 else 1
    kern = functools.partial(_gdn_kernel, t_len=t_len, gb=gb)
    return pl.pallas_call(
        kern, grid=(b // gb, tp // CHUNK),
        in_specs=[pl.BlockSpec((gb, CHUNK, CONV_CH), lambda i, j: (i, j, 0)),
                  pl.BlockSpec((gb, CHUNK, DN_W), lambda i, j: (i, j, 0)),
                  pl.BlockSpec((gb, CHUNK, BA_W), lambda i, j: (i, j, 0)),
                  pl.BlockSpec((gb, CONV_HALO, CONV_CH), lambda i, j: (i, 0, 0)),
                  pl.BlockSpec((gb, DN_HEADS, DN_D, DN_D), lambda i, j: (i, 0, 0, 0)),
                  pl.BlockSpec((8, CONV_CH), lambda i, j: (0, 0)),
                  pl.BlockSpec((8, LANES), lambda i, j: (0, 0)),
                  pl.BlockSpec((1, DN_D), lambda i, j: (0, 0))],
        out_specs=[pl.BlockSpec((gb, CHUNK, DN_W), lambda i, j: (i, j, 0)),
                   pl.BlockSpec((gb, DN_HEADS, DN_D, DN_D), lambda i, j: (i, 0, 0, 0))],
        out_shape=[jax.ShapeDtypeStruct((b, tp, DN_W), F32),
                   jax.ShapeDtypeStruct((b, DN_HEADS, DN_D, DN_D), F32)],
        scratch_shapes=[pltpu.VMEM((gb * DN_HEADS, DN_D, DN_D), F32),
                        pltpu.VMEM((gb, CONV_HALO + CHUNK, CONV_CH), F32)],
        compiler_params=_cparams(2), name="gdn")(qkv3, z3, ba3, cbuf8, s0, conv_w8, par, dn_norm)


STEP_TILE = 8


def _gdn_step_kernel(qkv_ref, z_ref, ba_ref, cst_ref, s0_ref, cw_ref, par_ref, nw_ref, o_ref, s_ref):
    acc = qkv_ref[...] * cw_ref[CONV_W - 1:CONV_W, :]
    for j in range(CONV_W - 1):
        acc = acc + cst_ref[:, j, :] * cw_ref[j:j + 1, :]
    qkvc = _silu(acc)
    ba = ba_ref[...]
    head_lane = lax.broadcasted_iota(I32, (1, LANES), 1) < DN_HEADS
    beta_all = _sigmoid(ba[:, :LANES])
    decay = jnp.where(head_lane, -jnp.exp(par_ref[0:1, :]), 0.0)
    eg_all = jnp.exp(decay * _softplus(ba[:, LANES:] + par_ref[1:2, :]))
    pad6 = jnp.zeros((6, DN_D), F32)
    pad7 = jnp.zeros((7, DN_D), F32)
    for h in range(DN_HEADS):
        q = qkvc[:, h * DN_D:(h + 1) * DN_D]
        k = qkvc[:, DN_W + h * DN_D:DN_W + (h + 1) * DN_D]
        v = qkvc[:, 2 * DN_W + h * DN_D:2 * DN_W + (h + 1) * DN_D]
        q = q * lax.rsqrt(jnp.sum(q * q, axis=-1, keepdims=True) + EPS) * (DN_D ** -0.5)
        k = k * lax.rsqrt(jnp.sum(k * k, axis=-1, keepdims=True) + EPS)
        qk = _dot_nt(_bf(q), _bf(k))
        o_rows = []
        for b in range(STEP_TILE):
            kb, qb_, vb = k[b:b + 1], q[b:b + 1], v[b:b + 1]
            beta = beta_all[b:b + 1, h:h + 1]
            eg = eg_all[b:b + 1, h:h + 1]
            s_old = s0_ref[b, h]
            ks_qs = _dot(_bf(jnp.concatenate([kb, qb_, pad6], axis=0)), _bf(s_old))
            u = beta * (vb - eg * ks_qs[0:1])
            o_rows.append(eg * ks_qs[1:2] + qk[b:b + 1, b:b + 1] * u)
            k_col = jnp.concatenate([kb, pad7], axis=0).T
            s_ref[b, h] = eg * s_old + _dot(_bf(k_col), _bf(jnp.concatenate([u, pad7], axis=0)))
        o = jnp.concatenate(o_rows, axis=0)
        sl = slice(h * DN_D, (h + 1) * DN_D)
        o_ref[:, sl] = _rms(o, nw_ref[...]) * _silu(z_ref[:, sl])


def _gdn_step(qkv2, z2, ba2, conv_state, s0, conv_w8, par, dn_norm):
    b = qkv2.shape[0]
    assert b % STEP_TILE == 0
    row = lambda w: pl.BlockSpec((STEP_TILE, w), lambda i: (i, 0))
    st = pl.BlockSpec((STEP_TILE, DN_HEADS, DN_D, DN_D), lambda i: (i, 0, 0, 0))
    return pl.pallas_call(
        _gdn_step_kernel, grid=(b // STEP_TILE,),
        in_specs=[row(CONV_CH), row(DN_W), row(BA_W),
                  pl.BlockSpec((STEP_TILE, CONV_W - 1, CONV_CH), lambda i: (i, 0, 0)), st,
                  pl.BlockSpec((8, CONV_CH), lambda i: (0, 0)),
                  pl.BlockSpec((8, LANES), lambda i: (0, 0)),
                  pl.BlockSpec((1, DN_D), lambda i: (0, 0))],
        out_specs=[row(DN_W), st],
        out_shape=[jax.ShapeDtypeStruct((b, DN_W), F32),
                   jax.ShapeDtypeStruct((b, DN_HEADS, DN_D, DN_D), F32)],
        compiler_params=_cparams(1), name="gdn_step")(qkv2, z2, ba2, conv_state, s0, conv_w8, par, dn_norm)


def _out_proj_kernel(pool_ref, dn_ref, x_ref, gt_ref, sh_ref, sc_ref, nw_ref, w_ref, x1_ref, h2_ref):
    mix = _dot(pool_ref[...].astype(BF16), w_ref[:POOL_W, :]) + _dot(dn_ref[...].astype(BF16), w_ref[POOL_W:, :])
    x1 = x_ref[...] + gt_ref[0] * mix
    x1_ref[...] = x1
    h2_ref[...] = _rms(x1, nw_ref[...]) * (1.0 + sc_ref[0]) + sh_ref[0]


def _out_proj(pool2, dn2, x2d, mod3, nw, w_out_b, tm, tiles_per_batch):
    n, d = x2d.shape
    rows = mod3.shape[1]
    tok = lambda w: pl.BlockSpec((tm, w), lambda i: (i, 0))
    return pl.pallas_call(
        _out_proj_kernel, grid=(n // tm,),
        in_specs=[tok(POOL_W), tok(DN_W), tok(d),
                  _mod_spec(rows, d, tiles_per_batch, 2),
                  _mod_spec(rows, d, tiles_per_batch, 3),
                  _mod_spec(rows, d, tiles_per_batch, 4),
                  pl.BlockSpec((1, d), lambda i: (0, 0)),
                  pl.BlockSpec((POOL_W + DN_W, d), lambda i: (0, 0))],
        out_specs=[tok(d), tok(d)],
        out_shape=[jax.ShapeDtypeStruct((n, d), F32)] * 2,
        compiler_params=_cparams(1), name="out_proj")(pool2, dn2, x2d, mod3, mod3, mod3, nw, w_out_b)


def _router_kernel(h_ref, rw_ref, rb_ref, cnt0_ref, eidx_ref, wts_ref, rank_ref, cnt_ref, run, *, tm):
    i = pl.program_id(0)

    @pl.when(i == 0)
    def _():
        run[...] = cnt0_ref[...]

    s = _sigmoid(_dot(h_ref[...].astype(BF16), rw_ref[...]))
    sel = s + rb_ref[...]
    lane_i = lax.broadcasted_iota(I32, (tm, N_EXPERTS), 1)
    lane = lane_i.astype(F32)
    grp = lane_i >> GROUP_SHIFT
    neg = -jnp.inf

    def first_max(x):
        m = jnp.max(x, axis=-1, keepdims=True)
        idx = jnp.min(jnp.where(x == m, lane, float(N_EXPERTS)), axis=-1, keepdims=True)
        return m, idx

    gs = []
    for g in range(N_GROUPS):
        xg = jnp.where(grp == g, sel, neg)
        m1, i1 = first_max(xg)
        m2 = jnp.max(jnp.where(lane == i1, neg, xg), axis=-1, keepdims=True)
        gs.append(m1 + m2)
    cur = jnp.full((tm, N_EXPERTS), neg, F32)
    for g in range(N_GROUPS):
        ahead = jnp.zeros((tm, 1), F32)
        for o in range(N_GROUPS):
            if o < g:
                ahead = ahead + jnp.where(gs[o] >= gs[g], 1.0, 0.0)
            elif o > g:
                ahead = ahead + jnp.where(gs[o] > gs[g], 1.0, 0.0)
        cur = jnp.where((grp == g) & (ahead < TOPK_GROUPS), sel, cur)

    hits, idxs, ws = [], [], []
    chosen = jnp.zeros((tm, N_EXPERTS), F32)
    for _ in range(TOP_K):
        _, idx = first_max(cur)
        hit = lane == idx
        ws.append(jnp.sum(jnp.where(hit, s, 0.0), axis=-1, keepdims=True))
        cur = jnp.where(hit, neg, cur)
        chosen = jnp.where(hit, 1.0, chosen)
        hits.append(hit)
        idxs.append(idx)
    onehot = chosen
    r = lax.broadcasted_iota(I32, (tm, tm), 0)
    c = lax.broadcasted_iota(I32, (tm, tm), 1)
    earlier = jnp.where(r > c, 1.0, 0.0).astype(BF16)
    before = _dot(earlier, onehot.astype(BF16)) + run[...]
    run[...] = run[...] + jnp.sum(onehot, axis=0, keepdims=True)

    wsum = ws[0]
    for w in ws[1:]:
        wsum = wsum + w
    out_lane = lax.broadcasted_iota(I32, (tm, LANES), 1)
    eidx = jnp.zeros((tm, LANES), F32)
    wts = jnp.zeros((tm, LANES), F32)
    rank = jnp.zeros((tm, LANES), F32)
    for k in range(TOP_K):
        rk = jnp.sum(jnp.where(hits[k], before, 0.0), axis=-1, keepdims=True)
        eidx = jnp.where(out_lane == k, idxs[k], eidx)
        wts = jnp.where(out_lane == k, ws[k] / wsum * ROUTED_SCALE, wts)
        rank = jnp.where(out_lane == k, rk, rank)
    eidx_ref[...] = eidx.astype(I32)
    wts_ref[...] = wts
    rank_ref[...] = rank.astype(I32)

    @pl.when(i == pl.num_programs(0) - 1)
    def _():
        cnt_ref[...] = run[...]


def _token_specs(n_p, tm, d_block, index_rest):
    np_tiles = n_p // tm
    return (pl.BlockSpec((tm,) + d_block, lambda i, *_: (jnp.minimum(i, np_tiles - 1),) + index_rest),
            pl.BlockSpec((tm,) + d_block, lambda i, *_: (jnp.maximum(i - np_tiles, 0),) + index_rest))


def _router(h, router_w_b, router_bias, counts0):
    n, d = h.shape
    tm = next(t for t in (512, 256, 128, 64, 32, 16, 8) if n % t == 0)
    tok = pl.BlockSpec((tm, LANES), lambda i: (i, 0))
    cnt = pl.BlockSpec((1, N_EXPERTS), lambda i: (0, 0))
    return pl.pallas_call(
        functools.partial(_router_kernel, tm=tm), grid=(n // tm,),
        in_specs=[pl.BlockSpec((tm, d), lambda i: (i, 0)),
                  pl.BlockSpec((d, N_EXPERTS), lambda i: (0, 0)), cnt, cnt],
        out_specs=[tok, tok, tok, cnt],
        out_shape=[jax.ShapeDtypeStruct((n, LANES), I32),
                   jax.ShapeDtypeStruct((n, LANES), F32),
                   jax.ShapeDtypeStruct((n, LANES), I32),
                   jax.ShapeDtypeStruct((1, N_EXPERTS), F32)],
        scratch_shapes=[pltpu.VMEM((1, N_EXPERTS), F32)],
        compiler_params=_cparams(1), name="router")(h, router_w_b, router_bias, counts0)


SUBLANES = 8


def _gather_rows(idx_ref, src_hbm, dst, sem):
    def body(g, carry):
        for j in range(SUBLANES):
            tok = idx_ref[0, 0, g * SUBLANES + j]
            pltpu.make_async_copy(src_hbm.at[pl.ds(tok, 1)], dst.at[g, pl.ds(j, 1)], sem).start(priority=j % 2)
        return carry
    lax.fori_loop(0, dst.shape[0], body, 0)


def _wait_rows(dst, sem):
    pltpu.make_async_copy(dst, dst, sem).wait()


def _dest_kernel(eidx_ref, rank_ref, ps_ref, dest_ref, *, tm):
    lane = lax.broadcasted_iota(I32, (tm, N_EXPERTS), 1)
    out_lane = lax.broadcasted_iota(I32, (tm, LANES), 1)
    eidx = eidx_ref[...]
    rank = rank_ref[...].astype(F32)
    cols = [eidx[:, k:k + 1] for k in range(TOP_K)]
    picked = [jnp.where(lane == c, ps_ref[...], 0.0) for c in cols]
    starts = [jnp.sum(p, axis=-1, keepdims=True) for p in picked]
    dest = jnp.zeros((tm, LANES), F32)
    for k in range(TOP_K):
        dest = jnp.where(out_lane == k, starts[k] + rank[:, k:k + 1], dest)
    dest_ref[...] = dest.astype(I32)


def _dest(eidx, rank, pstarts, tm):
    n = eidx.shape[0]
    tok = pl.BlockSpec((tm, LANES), lambda i: (i, 0))
    return pl.pallas_call(
        functools.partial(_dest_kernel, tm=tm), grid=(n // tm,),
        in_specs=[tok, tok, pl.BlockSpec((1, N_EXPERTS), lambda i: (0, 0))],
        out_specs=tok, out_shape=jax.ShapeDtypeStruct((n, LANES), I32),
        compiler_params=_cparams(1), name="dest")(eidx, rank, pstarts)


def _dispatch_kernel(clear_ref, dest_ref, hp_ref, hs_ref, xs_hbm, pbuf, zbuf, sems, zsem, *, tm, nb, np_tiles):
    i = pl.program_id(0)
    n = pl.num_programs(0)
    slot = i % 2

    def zero_copy(blk):
        return pltpu.make_async_copy(zbuf, xs_hbm.at[pl.ds(blk * ROW_BLOCK, ROW_BLOCK)], zsem)

    def wait_slot(s):
        for _ in range(TOP_K):
            _wait_rows(pbuf.at[s], sems.at[s])

    @pl.when(i == 0)
    def _():
        zbuf[...] = jnp.zeros_like(zbuf)

        def start(blk, carry):
            @pl.when(clear_ref[blk] > 0)
            def _():
                zero_copy(blk).start()
            return carry

        def wait(blk, carry):
            @pl.when(clear_ref[blk] > 0)
            def _():
                zero_copy(blk).wait()
            return carry
        lax.fori_loop(0, nb, start, 0)
        lax.fori_loop(0, nb, wait, 0)

    @pl.when(i >= 2)
    def _():
        wait_slot(slot)

    pbuf[slot] = jnp.where(i < np_tiles, hp_ref[...], hs_ref[...])

    def body(g, carry):
        for j in range(SUBLANES):
            for k in range(TOP_K):
                row = dest_ref[0, 0, (g * SUBLANES + j) * TOP_K + k]
                pltpu.make_async_copy(pbuf.at[slot, g, pl.ds(j, 1)], xs_hbm.at[pl.ds(row, 1)],
                                      sems.at[slot]).start(priority=k % 2)
        return carry
    lax.fori_loop(0, tm // SUBLANES, body, 0)

    @pl.when(i == n - 1)
    def _():
        wait_slot(slot)

    @pl.when(jnp.logical_and(i == n - 1, n > 1))
    def _():
        wait_slot(1 - slot)


def _dispatch(clear, dest3, h_p, h_s, tm):
    (n_p, d), n_s = h_p.shape, h_s.shape[0]
    nb = clear.shape[0]
    groups = tm // SUBLANES
    rows3 = lambda h: h.reshape(h.shape[0] // SUBLANES, SUBLANES, d)
    grid_spec = pltpu.PrefetchScalarGridSpec(
        num_scalar_prefetch=1, grid=((n_p + n_s) // tm,),
        in_specs=[pl.BlockSpec((1, 1, tm * TOP_K), lambda i, cl: (i, 0, 0), memory_space=pltpu.SMEM),
                  *_token_specs(n_p // SUBLANES, groups, (SUBLANES, d), (0, 0))],
        out_specs=pl.BlockSpec(memory_space=pl.ANY),
        scratch_shapes=[pltpu.VMEM((2, groups, SUBLANES, d), F32), pltpu.VMEM((ROW_BLOCK, d), F32),
                        pltpu.SemaphoreType.DMA((2,)), pltpu.SemaphoreType.DMA(())])
    return pl.pallas_call(
        functools.partial(_dispatch_kernel, tm=tm, nb=nb, np_tiles=n_p // tm), grid_spec=grid_spec,
        out_shape=jax.ShapeDtypeStruct((nb * ROW_BLOCK, d), F32),
        compiler_params=_cparams(1), name="dispatch")(clear, dest3, rows3(h_p), rows3(h_s))


def _experts_kernel(be_ref, nact_ref, x_ref, g_hbm, u_hbm, d_hbm, y_ref, gf, uf, df, gb, ub, db, sems, slot_ref,
                    *, base, nb):
    i = pl.program_id(0)
    nact = nact_ref[0]

    def weight_copies(e, s):
        return (pltpu.make_async_copy(g_hbm.at[base + e], gf.at[s], sems.at[s]),
                pltpu.make_async_copy(u_hbm.at[base + e], uf.at[s], sems.at[s]),
                pltpu.make_async_copy(d_hbm.at[base + e], df.at[s], sems.at[s]))

    @pl.when(jnp.logical_and(i == 0, nact > 0))
    def _():
        slot_ref[0] = 0
        for c in weight_copies(be_ref[0], 0):
            c.start()

    first = jnp.logical_or(i == 0, be_ref[i] != be_ref[jnp.maximum(i - 1, 0)])

    @pl.when(jnp.logical_and(i < nact, first))
    def _():
        s = slot_ref[0]
        here = be_ref[i]
        nxt = lax.while_loop(lambda j: jnp.logical_and(j < nact, be_ref[jnp.minimum(j, nb - 1)] == here),
                             lambda j: j + 1, i + 1)

        @pl.when(nxt < nact)
        def _():
            for c in weight_copies(be_ref[jnp.minimum(nxt, nb - 1)], 1 - s):
                c.start()
        for c in weight_copies(here, s):
            c.wait()
        gb[...] = gf[s].astype(BF16)
        ub[...] = uf[s].astype(BF16)
        db[...] = df[s].astype(BF16)
        slot_ref[0] = 1 - s

    @pl.when(i < nact)
    def _():
        x = x_ref[...].astype(BF16)
        hid = _silu(_dot(x, gb[...])) * _dot(x, ub[...])
        y_ref[...] = _dot(hid.astype(BF16), db[...])

    @pl.when(i >= nact)
    def _():
        y_ref[...] = jnp.zeros_like(y_ref)


def _experts(block_e, nact, x_sorted, exp_gate, exp_up, exp_down, layer):
    nb = x_sorted.shape[0] // ROW_BLOCK
    _, d, de = exp_gate.shape
    base = layer * N_EXPERTS
    grid_spec = pltpu.PrefetchScalarGridSpec(
        num_scalar_prefetch=2, grid=(nb,),
        in_specs=[pl.BlockSpec((ROW_BLOCK, d),
                               lambda i, be, na: (jnp.minimum(i, jnp.maximum(na[0] - 1, 0)), 0)),
                  pl.BlockSpec(memory_space=pl.ANY), pl.BlockSpec(memory_space=pl.ANY),
                  pl.BlockSpec(memory_space=pl.ANY)],
        out_specs=pl.BlockSpec((ROW_BLOCK, d), lambda i, be, na: (i, 0)),
        scratch_shapes=[pltpu.VMEM((2, d, de), F32), pltpu.VMEM((2, d, de), F32), pltpu.VMEM((2, de, d), F32),
                        pltpu.VMEM((d, de), BF16), pltpu.VMEM((d, de), BF16), pltpu.VMEM((de, d), BF16),
                        pltpu.SemaphoreType.DMA((2,)), pltpu.SMEM((1,), I32)])
    return pl.pallas_call(
        functools.partial(_experts_kernel, base=base, nb=nb), grid_spec=grid_spec,
        out_shape=jax.ShapeDtypeStruct((nb * ROW_BLOCK, d), F32),
        compiler_params=_cparams(1), name="experts")(block_e, nact, x_sorted, exp_gate, exp_up, exp_down)


def _combine_kernel(cur_ref, nxt_ref, y_hbm, wts_ref, x1_ref, h_ref, gt_ref, sg_ref, su_ref, sd_ref,
                    nf_ref, x2_ref, yn_ref, ybuf, sems, *, tm):
    i = pl.program_id(0)
    n = pl.num_programs(0)
    slot = i % 2
    groups = tm // SUBLANES

    @pl.when(i == 0)
    def _():
        _gather_rows(cur_ref, y_hbm, ybuf.at[0], sems.at[0])

    @pl.when(i + 1 < n)
    def _():
        _gather_rows(nxt_ref, y_hbm, ybuf.at[1 - slot], sems.at[1 - slot])

    hb = h_ref[...].astype(BF16)
    hid = _silu(_dot(hb, sg_ref[...])) * _dot(hb, su_ref[...])
    acc = _dot(hid.astype(BF16), sd_ref[...])
    _wait_rows(ybuf.at[slot], sems.at[slot])
    wts = wts_ref[...]
    for k in range(TOP_K):
        y_k = ybuf[slot, k * groups:(k + 1) * groups].reshape(tm, acc.shape[1])
        acc = acc + y_k * wts[:, k:k + 1]
    x2 = x1_ref[...] + gt_ref[0] * acc
    x2_ref[...] = x2
    yn_ref[...] = _rms(x2, nf_ref[...])


def _combine(dest3, y_sorted, wts, x1, h2, mod3, sh_gate_b, sh_up_b, sh_down_b, norm_final, tm, tiles_per_batch):
    n, d = x1.shape
    nt = n // tm
    rows = mod3.shape[1]
    ds = sh_gate_b.shape[1]
    tok = lambda w: pl.BlockSpec((tm, w), lambda i: (i, 0))
    kern = functools.partial(_combine_kernel, tm=tm)
    return pl.pallas_call(
        kern, grid=(nt,),
        in_specs=[pl.BlockSpec((1, 1, tm * TOP_K), lambda i: (i, 0, 0), memory_space=pltpu.SMEM),
                  pl.BlockSpec((1, 1, tm * TOP_K), lambda i: (jnp.minimum(i + 1, nt - 1), 0, 0),
                               memory_space=pltpu.SMEM),
                  pl.BlockSpec(memory_space=pl.ANY),
                  tok(LANES), tok(d), tok(d),
                  _mod_spec(rows, d, tiles_per_batch, 5),
                  pl.BlockSpec((d, ds), lambda i: (0, 0)),
                  pl.BlockSpec((d, ds), lambda i: (0, 0)),
                  pl.BlockSpec((ds, d), lambda i: (0, 0)),
                  pl.BlockSpec((1, d), lambda i: (0, 0))],
        out_specs=[tok(d), tok(d)],
        out_shape=[jax.ShapeDtypeStruct((n, d), F32)] * 2,
        scratch_shapes=[pltpu.VMEM((2, tm * TOP_K // SUBLANES, SUBLANES, d), F32),
                        pltpu.SemaphoreType.DMA((2,))],
        compiler_params=_cparams(1), name="combine")(
            dest3, dest3, y_sorted, wts, x1, h2, mod3, sh_gate_b, sh_up_b, sh_down_b, norm_final)


def _last_rows(buf, new, keep):
    t = new.shape[1]
    if t >= keep:
        return new[:, t - keep:]
    return jnp.concatenate([buf[:, t:], new], axis=1)


def _mixers(u, qkv, z, ba, b, t, pool_buf, conv_buf, s0, start_pos, lw):
    t_pool = -(-t // 8) * 8
    t_gdn = -(-t // CHUNK) * CHUNK
    u3 = u.reshape(b, t, POOL_W)
    qkv3 = qkv.reshape(b, t, CONV_CH)
    pad_t = lambda a, tp: a if tp == t else jnp.pad(a, ((0, 0), (0, tp - t), (0, 0)))
    buf16 = jnp.pad(pool_buf, ((0, 0), (POOL_HALO - POOL_BUF, 0), (0, 0)))
    pool_out = _pool(pad_t(u3, t_pool), buf16, lw["pool_w"], lw["pool_scale"], start_pos)[:, :t]
    if t == 1:
        dn_out, s_new = _gdn_step(qkv, z, ba, conv_buf, s0, lw["conv_w"], lw["par"], lw["dn_norm"])
    else:
        cbuf8 = jnp.pad(conv_buf, ((0, 0), (CONV_HALO - (CONV_W - 1), 0), (0, 0)))
        dn_out, s_new = _gdn(pad_t(qkv3, t_gdn), pad_t(z.reshape(b, t, DN_W), t_gdn),
                             pad_t(ba.reshape(b, t, BA_W), t_gdn), cbuf8, s0, lw["conv_w"], lw["par"],
                             lw["dn_norm"], t)
        dn_out = dn_out[:, :t].reshape(b * t, DN_W)
    new_pool = _last_rows(pool_buf, u3, POOL_BUF)
    new_conv = _last_rows(conv_buf, qkv3, CONV_W - 1)
    return pool_out.reshape(b * t, POOL_W), dn_out, new_pool, new_conv, s_new


def _block_tables(counts, n_tok):
    nk = n_tok * TOP_K
    nb = -(-(nk + N_EXPERTS * (ROW_BLOCK - 1)) // ROW_BLOCK)
    pcounts = (counts + ROW_BLOCK - 1) // ROW_BLOCK * ROW_BLOCK
    pends = jnp.cumsum(pcounts)
    pstarts = pends - pcounts
    first_row = jnp.arange(nb, dtype=I32)[:, None] * ROW_BLOCK
    block_e = jnp.minimum(jnp.sum((pends[None, :] <= first_row).astype(I32), axis=1), N_EXPERTS - 1)
    nact = (pends[-1] // ROW_BLOCK).astype(I32).reshape(1)
    blk = jnp.arange(nb, dtype=I32)
    next_e = jnp.concatenate([block_e[1:], jnp.full((1,), -1, I32)])
    clear = ((blk >= nact[0] - 1) | (next_e != block_e)).astype(I32)
    return pstarts.astype(F32).reshape(1, N_EXPERTS), block_e, nact, clear


def _dest_tiles(dest, tm, k_major):
    n = dest.shape[0]
    tiles = dest.reshape(n // tm, tm, TOP_K)
    if k_major:
        tiles = tiles.transpose(0, 2, 1)
    return tiles.reshape(n // tm, 1, tm * TOP_K)


def kernel(x_prompt, x_sample, state_pool, state_conv, state_delta, c_prompt, c_sample, norm_mix, norm_ffn, w_ada, b_ada, w_in, pool_w, pool_scale, conv_w, a_log, dt_bias, dn_norm, w_out, router_w, router_bias, exp_gate, exp_up, exp_down, sh_gate, sh_up, sh_down, norm_final):
    bp, tp, d = x_prompt.shape
    bs, ts, _ = x_sample.shape
    depth = w_ada.shape[0]
    past_len = PAST_LEN
    n_p, n_s = bp * tp, bs * ts
    n_tok = n_p + n_s
    tm_p = min(512, tp)
    tm_c = min(128, tp)
    tm_r = next(t for t in (128, 64, 32, 16, 8) if n_p % t == 0 and n_s % t == 0)
    assert tp % tm_p == 0 and ts == 1 and bs % 8 == 0
    n_exp = exp_gate.shape[1]
    gate_w = exp_gate.reshape((depth * n_exp,) + exp_gate.shape[2:])
    up_w = exp_up.reshape((depth * n_exp,) + exp_up.shape[2:])
    down_w = exp_down.reshape((depth * n_exp,) + exp_down.shape[2:])

    c_all = jnp.concatenate([c_prompt, c_sample], axis=0)
    m_rows = -(-c_all.shape[0] // 8) * 8
    mod_all = _ada(jnp.pad(c_all, ((0, m_rows - c_all.shape[0]), (0, 0))), w_ada, b_ada)

    c2 = POOL_W + CONV_CH + DN_W
    lane_pad = jnp.zeros((depth, d, LANES - DN_HEADS), F32)
    w_in_p = jnp.concatenate([w_in[:, :, :c2], w_in[:, :, c2:c2 + DN_HEADS], lane_pad,
                              w_in[:, :, c2 + DN_HEADS:], lane_pad], axis=-1).astype(BF16)
    head_pad = lambda a: jnp.pad(a, ((0, 0), (0, LANES - DN_HEADS)))
    par_all = jnp.stack([head_pad(a_log), head_pad(dt_bias)] + [jnp.zeros((depth, LANES), F32)] * 6, axis=1)
    conv_w8 = jnp.pad(conv_w, ((0, 0), (0, 8 - CONV_W), (0, 0)))

    xp = x_prompt.reshape(n_p, d)
    xs = x_sample.reshape(n_s, d)
    zero_pool = jnp.zeros((bp, POOL_BUF, POOL_W), F32)
    zero_conv = jnp.zeros((bp, CONV_W - 1, CONV_CH), F32)
    zero_state = jnp.zeros((bp, DN_HEADS, DN_D, DN_D), F32)
    outs = {k: [] for k in ("pool_p", "conv_p", "delta_p", "pool_s", "conv_s", "delta_s")}
    yp = ys = None
    for l in range(depth):
        lw = {"pool_w": pool_w[l].astype(BF16), "pool_scale": pool_scale[l].reshape(1, POOL_W),
              "conv_w": conv_w8[l], "par": par_all[l], "dn_norm": dn_norm[l].reshape(1, DN_D)}
        mod_p = mod_all[l, :bp].reshape(bp, 1, 6 * d)
        mod_s = mod_all[l, bp:bp + bs].reshape(1, bs, 6 * d)
        nw_mix = norm_mix[l].reshape(1, d)
        nw_ffn = norm_ffn[l].reshape(1, d)
        w_out_b = w_out[l].astype(BF16)

        up, qkvp, zp, bap = _norm_proj(xp, mod_p, nw_mix, w_in_p[l], tm_p, tp // tm_p)
        us, qkvs, zs, bas = _norm_proj(xs, mod_s, nw_mix, w_in_p[l], n_s, 1)
        pool_p, dn_p, npool_p, nconv_p, ns_p = _mixers(up, qkvp, zp, bap, bp, tp, zero_pool, zero_conv,
                                                       zero_state, 0, lw)
        pool_s, dn_s, npool_s, nconv_s, ns_s = _mixers(us, qkvs, zs, bas, bs, ts, state_pool[l], state_conv[l],
                                                       state_delta[l], past_len, lw)
        for key, val in (("pool_p", npool_p), ("conv_p", nconv_p), ("delta_p", ns_p),
                         ("pool_s", npool_s), ("conv_s", nconv_s), ("delta_s", ns_s)):
            outs[key].append(val)

        x1p, h2p = _out_proj(pool_p, dn_p, xp, mod_p, nw_ffn, w_out_b, tm_p, tp // tm_p)
        x1s, h2s = _out_proj(pool_s, dn_s, xs, mod_s, nw_ffn, w_out_b, n_s, 1)

        rw_b, rb = router_w[l].astype(BF16), router_bias[l].reshape(1, N_EXPERTS)
        eidx_p, wts_p, rank_p, counts = _router(h2p, rw_b, rb, jnp.zeros((1, N_EXPERTS), F32))
        eidx_s, wts_s, rank_s, counts = _router(h2s, rw_b, rb, counts)
        eidx, rank = jnp.concatenate([eidx_p, eidx_s]), jnp.concatenate([rank_p, rank_s])
        pstarts, block_e, nact, clear = _block_tables(counts.reshape(-1).astype(I32), n_tok)
        dest = _dest(eidx, rank, pstarts, tm_r)[:, :TOP_K]
        x_sorted = _dispatch(clear, _dest_tiles(dest, tm_r, False), h2p, h2s, tm_r)
        y_sorted = _experts(block_e, nact, x_sorted, gate_w, up_w, down_w, l)

        shg, shu, shd = sh_gate[l].astype(BF16), sh_up[l].astype(BF16), sh_down[l].astype(BF16)
        nf = norm_final.reshape(1, d)
        xp, yp = _combine(_dest_tiles(dest[:n_p], tm_c, True), y_sorted, wts_p, x1p, h2p, mod_p, shg, shu, shd,
                          nf, tm_c, tp // tm_c)
        xs, ys = _combine(_dest_tiles(dest[n_p:], n_s, True), y_sorted, wts_s, x1s, h2s, mod_s, shg, shu, shd,
                          nf, n_s, 1)

    stack = lambda key: jnp.stack(outs[key])
    return (yp.reshape(bp, tp, d), ys.reshape(bs, ts, d), stack("pool_p"), stack("conv_p"), stack("delta_p"),
            stack("pool_s"), stack("conv_s"), stack("delta_s"))
```

```python
import functools

import jax
import jax.numpy as jnp
from jax import lax
from jax.experimental import pallas as pl
from jax.experimental.pallas import tpu as pltpu

F32, BF16, I32 = jnp.float32, jnp.bfloat16, jnp.int32
HI = lax.Precision.HIGHEST
EPS = 1e-6

LANES = 128
POOL_WINDOWS = (2, 4, 8, 16)
POOL_GW = 128
POOL_W = 512
POOL_BUF = 15
POOL_HALO = 16
POOL_TILE = 512
DN_HEADS = 4
DN_D = 128
DN_W = 512
CONV_W = 4
CONV_CH = 3 * DN_W
CONV_HALO = 8
CHUNK = 64
INV_BLOCK = 16
GDN_ROWS = 4
N_EXPERTS = 256
TOP_K = 8
N_GROUPS = 8
TOPK_GROUPS = 4
GROUP_SHIFT = 5
ROUTED_SCALE = 2.5
PAST_LEN = 16384
ROW_BLOCK = 256
BA_W = 2 * LANES
D_IN_PAD = POOL_W + CONV_CH + DN_W + BA_W
VMEM_LIMIT = 48 * 2 ** 20


def _cparams(n_axes):
    return pltpu.CompilerParams(dimension_semantics=("arbitrary",) * n_axes,
                                vmem_limit_bytes=VMEM_LIMIT)


def _sigmoid(x):
    return 1.0 / (1.0 + jnp.exp(-x))


def _silu(x):
    return x * _sigmoid(x)


def _softplus(x):
    return jnp.maximum(x, 0.0) + jnp.log(1.0 + jnp.exp(-jnp.abs(x)))


def _rms(x, w):
    return x * lax.rsqrt(jnp.mean(x * x, axis=-1, keepdims=True) + EPS) * w


def _dot(a, b, precision=None):
    return jnp.dot(a, b, preferred_element_type=F32, precision=precision)


def _dot_nt(a, b, precision=None):
    return lax.dot_general(a, b, (((1,), (1,)), ((), ())), preferred_element_type=F32,
                           precision=precision)


def _ada_kernel(c_ref, w_ref, b_ref, o_ref):
    o_ref[0] = _dot(_silu(c_ref[...]), w_ref[0], HI) + b_ref[0]


def _ada(c_all, w_ada, b_ada):
    depth, d, d6 = w_ada.shape
    m = c_all.shape[0]
    tn = 1024
    return pl.pallas_call(
        _ada_kernel, grid=(depth, d6 // tn),
        in_specs=[pl.BlockSpec((m, d), lambda l, j: (0, 0)),
                  pl.BlockSpec((1, d, tn), lambda l, j: (l, 0, j)),
                  pl.BlockSpec((1, 1, tn), lambda l, j: (l, 0, j))],
        out_specs=pl.BlockSpec((1, m, tn), lambda l, j: (l, 0, j)),
        out_shape=jax.ShapeDtypeStruct((depth, m, d6), F32),
        compiler_params=_cparams(2), name="ada")(c_all, w_ada, b_ada.reshape(depth, 1, d6))


def _mod_spec(rows, d, tiles_per_batch, chunk):
    return pl.BlockSpec((1, rows, d), lambda i: (i // tiles_per_batch, 0, chunk))


def _norm_proj_kernel(x_ref, sh_ref, sc_ref, nw_ref, w_ref, u_ref, qkv_ref, z_ref, ba_ref):
    h = _rms(x_ref[...], nw_ref[...]) * (1.0 + sc_ref[0]) + sh_ref[0]
    hb = h.astype(BF16)
    c0, c1, c2 = POOL_W, POOL_W + CONV_CH, POOL_W + CONV_CH + DN_W
    u_ref[...] = _dot(hb, w_ref[:, :c0])
    qkv_ref[...] = _dot(hb, w_ref[:, c0:c1])
    z_ref[...] = _dot(hb, w_ref[:, c1:c2])
    ba_ref[...] = _dot(hb, w_ref[:, c2:])


def _norm_proj(x2d, mod3, nw, w_in_p, tm, tiles_per_batch):
    n, d = x2d.shape
    rows = mod3.shape[1]
    widths = (POOL_W, CONV_CH, DN_W, BA_W)
    return pl.pallas_call(
        _norm_proj_kernel, grid=(n // tm,),
        in_specs=[pl.BlockSpec((tm, d), lambda i: (i, 0)),
                  _mod_spec(rows, d, tiles_per_batch, 0),
                  _mod_spec(rows, d, tiles_per_batch, 1),
                  pl.BlockSpec((1, d), lambda i: (0, 0)),
                  pl.BlockSpec((d, D_IN_PAD), lambda i: (0, 0))],
        out_specs=[pl.BlockSpec((tm, w), lambda i: (i, 0)) for w in widths],
        out_shape=[jax.ShapeDtypeStruct((n, w), F32) for w in widths],
        compiler_params=_cparams(1), name="norm_proj")(x2d, mod3, mod3, nw, w_in_p)


def _pool_kernel(u_ref, buf_ref, pw_ref, ps_ref, o_ref, ext, *, tt, gb, start_pos):
    t = pl.program_id(1)

    @pl.when(t == 0)
    def _():
        ext[:, 0:POOL_HALO, :] = buf_ref[...]

    @pl.when(t > 0)
    def _():
        ext[:, 0:POOL_HALO, :] = ext[:, tt:tt + POOL_HALO, :]

    ext[:, POOL_HALO:POOL_HALO + tt, :] = u_ref[...]
    pos = start_pos + t * tt + lax.broadcasted_iota(I32, (tt, 1), 0)
    for g, win in enumerate(POOL_WINDOWS):
        sl = slice(g * POOL_GW, (g + 1) * POOL_GW)
        cnt = jnp.minimum(pos + 1, win).astype(F32)
        pooled = []
        for rb in range(gb):
            u = u_ref[rb, :, sl]
            s = u
            for j in range(1, win):
                s = s + ext[rb, POOL_HALO - j:POOL_HALO - j + tt, sl]
            pooled.append(s / cnt - u)
        mixed = _dot(jnp.concatenate(pooled, axis=0).astype(BF16), pw_ref[g]) * ps_ref[:, sl]
        for rb in range(gb):
            o_ref[rb, :, sl] = mixed[rb * tt:(rb + 1) * tt]


def _pool(u3, buf16, pool_w_b, pool_scale, start_pos):
    b, t, w = u3.shape
    tt = min(t, POOL_TILE)
    assert t % tt == 0 and tt % 8 == 0
    gb = next(g for g in (16, 8, 4, 2, 1) if b % g == 0 and g * tt <= POOL_TILE)
    return pl.pallas_call(
        functools.partial(_pool_kernel, tt=tt, gb=gb, start_pos=start_pos), grid=(b // gb, t // tt),
        in_specs=[pl.BlockSpec((gb, tt, w), lambda i, j: (i, j, 0)),
                  pl.BlockSpec((gb, POOL_HALO, w), lambda i, j: (i, 0, 0)),
                  pl.BlockSpec((len(POOL_WINDOWS), POOL_GW, POOL_GW), lambda i, j: (0, 0, 0)),
                  pl.BlockSpec((1, w), lambda i, j: (0, 0))],
        out_specs=pl.BlockSpec((gb, tt, w), lambda i, j: (i, j, 0)),
        out_shape=jax.ShapeDtypeStruct((b, t, w), F32),
        scratch_shapes=[pltpu.VMEM((gb, POOL_HALO + tt, w), F32)],
        compiler_params=_cparams(2), name="pool")(u3, buf16, pool_w_b, pool_scale)


def _bf(x):
    return x.astype(BF16)


def _dot3(a, b):
    a_hi, b_hi = _bf(a), _bf(b)
    a_lo, b_lo = _bf(a - a_hi.astype(F32)), _bf(b - b_hi.astype(F32))
    return _dot(a_hi, b_hi) + (_dot(a_hi, b_lo) + _dot(a_lo, b_hi))


def _each(fn, *lists):
    return [fn(*args) for args in zip(*lists)]


def _unit_lower_inverse(a, row, col, eye):
    same = (row // INV_BLOCK) == (col // INV_BLOCK)
    dm = _each(lambda x: jnp.where(same, -x, 0.0), a)
    lm = _each(lambda x, d: -x - d, a, dm)
    mm = lambda x, y: _dot(_bf(x), _bf(y))
    d2 = _each(mm, dm, dm)
    d4 = _each(mm, d2, d2)
    d8 = _each(mm, d4, d4)
    p = _each(lambda x, y: mm(eye + x, eye + y), dm, d2)
    p2 = _each(lambda x, y: mm(eye + x, eye + y), d4, d8)
    p = _each(mm, p, p2)
    m = _each(mm, p, lm)
    m2 = _each(mm, m, m)
    q = _each(lambda x, y: mm(eye + x, eye + y), m, m2)
    t0 = _each(mm, q, p)
    at = _each(_dot3, a, t0)
    resid = _each(lambda t, x: (eye - t) - x, t0, at)
    return _each(lambda t, r: t + mm(t, r), t0, resid)


def _gdn_kernel(qkv_ref, z_ref, ba_ref, cbuf_ref, s0_ref, cw_ref, par_ref, nw_ref,
                o_ref, sfin_ref, state, xext, *, t_len, gb):
    c = pl.program_id(1)
    seqs = range(gb)
    heads = range(DN_HEADS)
    hc = DN_HEADS * CHUNK

    @pl.when(c == 0)
    def _():
        for rb in seqs:
            state[rb * DN_HEADS:(rb + 1) * DN_HEADS] = s0_ref[rb]
            xext[rb, 0:CONV_HALO, :] = cbuf_ref[rb]

    @pl.when(c > 0)
    def _():
        for rb in seqs:
            xext[rb, 0:CONV_HALO, :] = xext[rb, CHUNK:CHUNK + CONV_HALO, :]

    valid = (c * CHUNK + lax.broadcasted_iota(I32, (CHUNK, 1), 0)) < t_len
    vf = valid.astype(F32)
    head_lane = lax.broadcasted_iota(I32, (1, LANES), 1) < DN_HEADS
    decay = jnp.where(head_lane, -jnp.exp(par_ref[0:1, :]), 0.0)
    r64 = lax.broadcasted_iota(I32, (CHUNK, CHUNK), 0)
    c64 = lax.broadcasted_iota(I32, (CHUNK, CHUNK), 1)
    tri = (r64 >= c64).astype(F32)
    pick = (lax.broadcasted_iota(I32, (8, LANES), 0) == lax.broadcasted_iota(I32, (8, LANES), 1)).astype(F32)
    row = lax.broadcasted_iota(I32, (hc, hc), 0)
    col = lax.broadcasted_iota(I32, (hc, hc), 1)
    incl = ((row // CHUNK) == (col // CHUNK)) & (row >= col)
    strict = row > col
    eye = (row == col).astype(F32)
    base = CONV_HALO - (CONV_W - 1)

    def conv(rb):
        xext[rb, CONV_HALO:CONV_HALO + CHUNK, :] = qkv_ref[rb]
        acc = xext[rb, base:base + CHUNK, :] * cw_ref[0:1, :]
        for j in range(1, CONV_W):
            acc = acc + xext[rb, base + j:base + j + CHUNK, :] * cw_ref[j:j + 1, :]
        return _silu(acc)
    qkvc = [conv(rb) for rb in seqs]
    ba = [ba_ref[rb] for rb in seqs]
    beta_all = _each(lambda x: _sigmoid(x[:, :LANES]) * vf, ba)
    g_all = _each(lambda x: decay * _softplus(x[:, LANES:] + par_ref[1:2, :]) * vf, ba)
    gam_all = _each(lambda g: _dot(tri, g, HI), g_all)
    gam_rows = _each(lambda g: _dot_nt(pick, g, HI), gam_all)

    def unit(x):
        return x * lax.rsqrt(jnp.sum(x * x, axis=-1, keepdims=True) + EPS)
    q_s = _each(lambda x: jnp.concatenate(
        [unit(x[:, h * DN_D:(h + 1) * DN_D]) * (DN_D ** -0.5) * vf for h in heads], axis=0), qkvc)
    k_s = _each(lambda x: jnp.concatenate(
        [unit(x[:, DN_W + h * DN_D:DN_W + (h + 1) * DN_D]) * vf for h in heads], axis=0), qkvc)
    v_s = _each(lambda x: jnp.concatenate(
        [x[:, 2 * DN_W + h * DN_D:2 * DN_W + (h + 1) * DN_D] * vf for h in heads], axis=0), qkvc)
    beta = _each(lambda x: jnp.concatenate([x[:, h:h + 1] for h in heads], axis=0), beta_all)
    gam = _each(lambda x: jnp.concatenate([x[:, h:h + 1] for h in heads], axis=0), gam_all)
    gam_r = _each(lambda x: jnp.concatenate([x[h:h + 1, :] for h in heads], axis=1), gam_rows)

    dec = _each(lambda g, gr: jnp.where(incl, jnp.exp(jnp.where(incl, g - gr, 0.0)), 0.0), gam, gam_r)
    kb = _each(_bf, k_s)
    qb = _each(_bf, q_s)
    kk = _each(_dot_nt, kb, kb)
    qk = _each(_dot_nt, qb, kb)
    a_mat = _each(lambda b_, x, d_: b_ * x * jnp.where(strict, d_, 0.0), beta, kk, dec)
    qk = _each(lambda x, d_: _bf(x * d_), qk, dec)
    egam = _each(jnp.exp, gam)
    rhs = _each(lambda b_, e, k, v: jnp.concatenate([(b_ * e) * k, b_ * v], axis=-1),
                beta, egam, k_s, v_s)
    t_inv = _unit_lower_inverse(a_mat, row, col, eye)
    sol = _each(_dot3, t_inv, rhs)
    wb = _each(lambda x: _bf(x[:, :DN_D]), sol)
    uv = _each(lambda x: x[:, DN_D:], sol)

    hrows = [slice(h * CHUNK, (h + 1) * CHUNK) for h in heads]
    s_f = [[state[rb * DN_HEADS + h] for h in heads] for rb in seqs]
    s_b = [[_bf(x) for x in per_seq] for per_seq in s_f]
    ub = _each(lambda u_, w_, s_: _bf(jnp.concatenate(
        [u_[hrows[h]] - _dot(w_[hrows[h]], s_[h]) for h in heads], axis=0)), uv, wb, s_b)
    o_intra = _each(_dot, qk, ub)
    for h in heads:
        rows = hrows[h]
        sl = slice(h * DN_D, (h + 1) * DN_D)
        o = _each(lambda e, q_, s_, oi: e[rows] * _dot(q_[rows], s_[h]) + oi[rows], egam, qb, s_b, o_intra)
        g_last = _each(lambda g: g[(h + 1) * CHUNK - 1:(h + 1) * CHUNK, :], gam)
        k_dec = _each(lambda k, gl, g: _bf((k[rows] * jnp.exp(gl - g[rows])).T), k_s, g_last, gam)
        upd = _each(lambda kd, u_: _dot(kd, u_[rows]), k_dec, ub)
        for rb in seqs:
            state[rb * DN_HEADS + h] = jnp.exp(g_last[rb]) * s_f[rb][h] + upd[rb]
            o_ref[rb, :, sl] = _rms(o[rb], nw_ref[...]) * _silu(z_ref[rb, :, sl])

    @pl.when(c == pl.num_programs(1) - 1)
    def _():
        for rb in seqs:
            sfin_ref[rb] = state[rb * DN_HEADS:(rb + 1) * DN_HEADS]


def _gdn(qkv3, z3, ba3, cbuf8, s0, conv_w8, par, dn_norm, t_len):
    b, tp, _ = qkv3.shape
    assert tp % CHUNK == 0
    gb = GDN_ROWS if b % GDN_ROWS == 0 else 1
    kern = functools.partial(_gdn_kernel, t_len=t_len, gb=gb)
    return pl.pallas_call(
        kern, grid=(b // gb, tp // CHUNK),
        in_specs=[pl.BlockSpec((gb, CHUNK, CONV_CH), lambda i, j: (i, j, 0)),
                  pl.BlockSpec((gb, CHUNK, DN_W), lambda i, j: (i, j, 0)),
                  pl.BlockSpec((gb, CHUNK, BA_W), lambda i, j: (i, j, 0)),
                  pl.BlockSpec((gb, CONV_HALO, CONV_CH), lambda i, j: (i, 0, 0)),
                  pl.BlockSpec((gb, DN_HEADS, DN_D, DN_D), lambda i, j: (i, 0, 0, 0)),
                  pl.BlockSpec((8, CONV_CH), lambda i, j: (0, 0)),
                  pl.BlockSpec((8, LANES), lambda i, j: (0, 0)),
                  pl.BlockSpec((1, DN_D), lambda i, j: (0, 0))],
        out_specs=[pl.BlockSpec((gb, CHUNK, DN_W), lambda i, j: (i, j, 0)),
                   pl.BlockSpec((gb, DN_HEADS, DN_D, DN_D), lambda i, j: (i, 0, 0, 0))],
        out_shape=[jax.ShapeDtypeStruct((b, tp, DN_W), F32),
                   jax.ShapeDtypeStruct((b, DN_HEADS, DN_D, DN_D), F32)],
        scratch_shapes=[pltpu.VMEM((gb * DN_HEADS, DN_D, DN_D), F32),
                        pltpu.VMEM((gb, CONV_HALO + CHUNK, CONV_CH), F32)],
        compiler_params=_cparams(2), name="gdn")(qkv3, z3, ba3, cbuf8, s0, conv_w8, par, dn_norm)


STEP_TILE = 8


def _gdn_step_kernel(qkv_ref, z_ref, ba_ref, cst_ref, s0_ref, cw_ref, par_ref, nw_ref, o_ref, s_ref):
    acc = qkv_ref[...] * cw_ref[CONV_W - 1:CONV_W, :]
    for j in range(CONV_W - 1):
        acc = acc + cst_ref[:, j, :] * cw_ref[j:j + 1, :]
    qkvc = _silu(acc)
    ba = ba_ref[...]
    head_lane = lax.broadcasted_iota(I32, (1, LANES), 1) < DN_HEADS
    beta_all = _sigmoid(ba[:, :LANES])
    decay = jnp.where(head_lane, -jnp.exp(par_ref[0:1, :]), 0.0)
    eg_all = jnp.exp(decay * _softplus(ba[:, LANES:] + par_ref[1:2, :]))
    pad6 = jnp.zeros((6, DN_D), F32)
    pad7 = jnp.zeros((7, DN_D), F32)
    for h in range(DN_HEADS):
        q = qkvc[:, h * DN_D:(h + 1) * DN_D]
        k = qkvc[:, DN_W + h * DN_D:DN_W + (h + 1) * DN_D]
        v = qkvc[:, 2 * DN_W + h * DN_D:2 * DN_W + (h + 1) * DN_D]
        q = q * lax.rsqrt(jnp.sum(q * q, axis=-1, keepdims=True) + EPS) * (DN_D ** -0.5)
        k = k * lax.rsqrt(jnp.sum(k * k, axis=-1, keepdims=True) + EPS)
        qk = _dot_nt(_bf(q), _bf(k))
        seqs = list(range(STEP_TILE))
        beta = [beta_all[b:b + 1, h:h + 1] for b in seqs]
        eg = [eg_all[b:b + 1, h:h + 1] for b in seqs]
        s_old = [s0_ref[b, h] for b in seqs]
        ks_qs = _each(lambda b, s_: _dot(_bf(jnp.concatenate([k[b:b + 1], q[b:b + 1], pad6], axis=0)), _bf(s_)),
                      seqs, s_old)
        u = _each(lambda b, b_, e, r: b_ * (v[b:b + 1] - e * r[0:1]), seqs, beta, eg, ks_qs)
        o_rows = _each(lambda b, e, r, u_: e * r[1:2] + qk[b:b + 1, b:b + 1] * u_, seqs, eg, ks_qs, u)
        k_col = [_bf(jnp.concatenate([k[b:b + 1], pad7], axis=0).T) for b in seqs]
        upd = _each(lambda kc, u_: _dot(kc, _bf(jnp.concatenate([u_, pad7], axis=0))), k_col, u)
        for b in seqs:
            s_ref[b, h] = eg[b] * s_old[b] + upd[b]
        o = jnp.concatenate(o_rows, axis=0)
        sl = slice(h * DN_D, (h + 1) * DN_D)
        o_ref[:, sl] = _rms(o, nw_ref[...]) * _silu(z_ref[:, sl])


def _gdn_step(qkv2, z2, ba2, conv_state, s0, conv_w8, par, dn_norm):
    b = qkv2.shape[0]
    assert b % STEP_TILE == 0
    row = lambda w: pl.BlockSpec((STEP_TILE, w), lambda i: (i, 0))
    st = pl.BlockSpec((STEP_TILE, DN_HEADS, DN_D, DN_D), lambda i: (i, 0, 0, 0))
    return pl.pallas_call(
        _gdn_step_kernel, grid=(b // STEP_TILE,),
        in_specs=[row(CONV_CH), row(DN_W), row(BA_W),
                  pl.BlockSpec((STEP_TILE, CONV_W - 1, CONV_CH), lambda i: (i, 0, 0)), st,
                  pl.BlockSpec((8, CONV_CH), lambda i: (0, 0)),
                  pl.BlockSpec((8, LANES), lambda i: (0, 0)),
                  pl.BlockSpec((1, DN_D), lambda i: (0, 0))],
        out_specs=[row(DN_W), st],
        out_shape=[jax.ShapeDtypeStruct((b, DN_W), F32),
                   jax.ShapeDtypeStruct((b, DN_HEADS, DN_D, DN_D), F32)],
        compiler_params=_cparams(1), name="gdn_step")(qkv2, z2, ba2, conv_state, s0, conv_w8, par, dn_norm)


def _out_proj_kernel(pool_ref, dn_ref, x_ref, gt_ref, sh_ref, sc_ref, nw_ref, w_ref, x1_ref, h2_ref):
    mix = _dot(pool_ref[...].astype(BF16), w_ref[:POOL_W, :]) + _dot(dn_ref[...].astype(BF16), w_ref[POOL_W:, :])
    x1 = x_ref[...] + gt_ref[0] * mix
    x1_ref[...] = x1
    h2_ref[...] = _rms(x1, nw_ref[...]) * (1.0 + sc_ref[0]) + sh_ref[0]


def _out_proj(pool2, dn2, x2d, mod3, nw, w_out_b, tm, tiles_per_batch):
    n, d = x2d.shape
    rows = mod3.shape[1]
    tok = lambda w: pl.BlockSpec((tm, w), lambda i: (i, 0))
    return pl.pallas_call(
        _out_proj_kernel, grid=(n // tm,),
        in_specs=[tok(POOL_W), tok(DN_W), tok(d),
                  _mod_spec(rows, d, tiles_per_batch, 2),
                  _mod_spec(rows, d, tiles_per_batch, 3),
                  _mod_spec(rows, d, tiles_per_batch, 4),
                  pl.BlockSpec((1, d), lambda i: (0, 0)),
                  pl.BlockSpec((POOL_W + DN_W, d), lambda i: (0, 0))],
        out_specs=[tok(d), tok(d)],
        out_shape=[jax.ShapeDtypeStruct((n, d), F32)] * 2,
        compiler_params=_cparams(1), name="out_proj")(pool2, dn2, x2d, mod3, mod3, mod3, nw, w_out_b)


def _router_kernel(h_ref, rw_ref, rb_ref, cnt0_ref, eidx_ref, wts_ref, rank_ref, cnt_ref, run, *, tm):
    i = pl.program_id(0)

    @pl.when(i == 0)
    def _():
        run[...] = cnt0_ref[...]

    s = _sigmoid(_dot(h_ref[...].astype(BF16), rw_ref[...]))
    sel = s + rb_ref[...]
    lane_i = lax.broadcasted_iota(I32, (tm, N_EXPERTS), 1)
    lane = lane_i.astype(F32)
    grp = lane_i >> GROUP_SHIFT
    neg = -jnp.inf

    def first_max(x):
        m = jnp.max(x, axis=-1, keepdims=True)
        idx = jnp.min(jnp.where(x == m, lane, float(N_EXPERTS)), axis=-1, keepdims=True)
        return m, idx

    gs = []
    for g in range(N_GROUPS):
        xg = jnp.where(grp == g, sel, neg)
        m1, i1 = first_max(xg)
        m2 = jnp.max(jnp.where(lane == i1, neg, xg), axis=-1, keepdims=True)
        gs.append(m1 + m2)
    cur = jnp.full((tm, N_EXPERTS), neg, F32)
    for g in range(N_GROUPS):
        ahead = jnp.zeros((tm, 1), F32)
        for o in range(N_GROUPS):
            if o < g:
                ahead = ahead + jnp.where(gs[o] >= gs[g], 1.0, 0.0)
            elif o > g:
                ahead = ahead + jnp.where(gs[o] > gs[g], 1.0, 0.0)
        cur = jnp.where((grp == g) & (ahead < TOPK_GROUPS), sel, cur)

    hits, idxs, ws = [], [], []
    chosen = jnp.zeros((tm, N_EXPERTS), F32)
    for _ in range(TOP_K):
        _, idx = first_max(cur)
        hit = lane == idx
        ws.append(jnp.sum(jnp.where(hit, s, 0.0), axis=-1, keepdims=True))
        cur = jnp.where(hit, neg, cur)
        chosen = jnp.where(hit, 1.0, chosen)
        hits.append(hit)
        idxs.append(idx)
    onehot = chosen
    r = lax.broadcasted_iota(I32, (tm, tm), 0)
    c = lax.broadcasted_iota(I32, (tm, tm), 1)
    earlier = jnp.where(r > c, 1.0, 0.0).astype(BF16)
    before = _dot(earlier, onehot.astype(BF16)) + run[...]
    run[...] = run[...] + jnp.sum(onehot, axis=0, keepdims=True)

    wsum = ws[0]
    for w in ws[1:]:
        wsum = wsum + w
    out_lane = lax.broadcasted_iota(I32, (tm, LANES), 1)
    eidx = jnp.zeros((tm, LANES), F32)
    wts = jnp.zeros((tm, LANES), F32)
    rank = jnp.zeros((tm, LANES), F32)
    for k in range(TOP_K):
        rk = jnp.sum(jnp.where(hits[k], before, 0.0), axis=-1, keepdims=True)
        eidx = jnp.where(out_lane == k, idxs[k], eidx)
        wts = jnp.where(out_lane == k, ws[k] / wsum * ROUTED_SCALE, wts)
        rank = jnp.where(out_lane == k, rk, rank)
    eidx_ref[...] = eidx.astype(I32)
    wts_ref[...] = wts
    rank_ref[...] = rank.astype(I32)

    @pl.when(i == pl.num_programs(0) - 1)
    def _():
        cnt_ref[...] = run[...]


def _token_specs(n_p, tm, d_block, index_rest):
    np_tiles = n_p // tm
    return (pl.BlockSpec((tm,) + d_block, lambda i, *_: (jnp.minimum(i, np_tiles - 1),) + index_rest),
            pl.BlockSpec((tm,) + d_block, lambda i, *_: (jnp.maximum(i - np_tiles, 0),) + index_rest))


def _router(h, router_w_b, router_bias, counts0):
    n, d = h.shape
    tm = next(t for t in (512, 256, 128, 64, 32, 16, 8) if n % t == 0)
    tok = pl.BlockSpec((tm, LANES), lambda i: (i, 0))
    cnt = pl.BlockSpec((1, N_EXPERTS), lambda i: (0, 0))
    return pl.pallas_call(
        functools.partial(_router_kernel, tm=tm), grid=(n // tm,),
        in_specs=[pl.BlockSpec((tm, d), lambda i: (i, 0)),
                  pl.BlockSpec((d, N_EXPERTS), lambda i: (0, 0)), cnt, cnt],
        out_specs=[tok, tok, tok, cnt],
        out_shape=[jax.ShapeDtypeStruct((n, LANES), I32),
                   jax.ShapeDtypeStruct((n, LANES), F32),
                   jax.ShapeDtypeStruct((n, LANES), I32),
                   jax.ShapeDtypeStruct((1, N_EXPERTS), F32)],
        scratch_shapes=[pltpu.VMEM((1, N_EXPERTS), F32)],
        compiler_params=_cparams(1), name="router")(h, router_w_b, router_bias, counts0)


SUBLANES = 8


def _gather_rows(idx_ref, src_hbm, dst, sem):
    def body(g, carry):
        for j in range(SUBLANES):
            tok = idx_ref[0, 0, g * SUBLANES + j]
            pltpu.make_async_copy(src_hbm.at[pl.ds(tok, 1)], dst.at[g, pl.ds(j, 1)], sem).start(priority=j % 2)
        return carry
    lax.fori_loop(0, dst.shape[0], body, 0)


def _wait_rows(dst, sem):
    pltpu.make_async_copy(dst, dst, sem).wait()


def _dest_kernel(eidx_ref, rank_ref, ps_ref, dest_ref, *, tm):
    lane = lax.broadcasted_iota(I32, (tm, N_EXPERTS), 1)
    out_lane = lax.broadcasted_iota(I32, (tm, LANES), 1)
    eidx = eidx_ref[...]
    rank = rank_ref[...].astype(F32)
    cols = [eidx[:, k:k + 1] for k in range(TOP_K)]
    picked = [jnp.where(lane == c, ps_ref[...], 0.0) for c in cols]
    starts = [jnp.sum(p, axis=-1, keepdims=True) for p in picked]
    dest = jnp.zeros((tm, LANES), F32)
    for k in range(TOP_K):
        dest = jnp.where(out_lane == k, starts[k] + rank[:, k:k + 1], dest)
    dest_ref[...] = dest.astype(I32)


def _dest(eidx, rank, pstarts, tm):
    n = eidx.shape[0]
    tok = pl.BlockSpec((tm, LANES), lambda i: (i, 0))
    return pl.pallas_call(
        functools.partial(_dest_kernel, tm=tm), grid=(n // tm,),
        in_specs=[tok, tok, pl.BlockSpec((1, N_EXPERTS), lambda i: (0, 0))],
        out_specs=tok, out_shape=jax.ShapeDtypeStruct((n, LANES), I32),
        compiler_params=_cparams(1), name="dest")(eidx, rank, pstarts)


def _dispatch_kernel(clear_ref, dest_ref, hp_ref, hs_ref, xs_hbm, pbuf, zbuf, sems, zsem, *, tm, nb, np_tiles):
    i = pl.program_id(0)
    n = pl.num_programs(0)
    slot = i % 2

    def zero_copy(blk):
        return pltpu.make_async_copy(zbuf, xs_hbm.at[pl.ds(blk * ROW_BLOCK, ROW_BLOCK)], zsem)

    def wait_slot(s):
        for _ in range(TOP_K):
            _wait_rows(pbuf.at[s], sems.at[s])

    @pl.when(i == 0)
    def _():
        zbuf[...] = jnp.zeros_like(zbuf)

        def start(blk, carry):
            @pl.when(clear_ref[blk] > 0)
            def _():
                zero_copy(blk).start()
            return carry

        def wait(blk, carry):
            @pl.when(clear_ref[blk] > 0)
            def _():
                zero_copy(blk).wait()
            return carry
        lax.fori_loop(0, nb, start, 0)
        lax.fori_loop(0, nb, wait, 0)

    @pl.when(i >= 2)
    def _():
        wait_slot(slot)

    pbuf[slot] = jnp.where(i < np_tiles, hp_ref[...], hs_ref[...])

    def body(g, carry):
        for j in range(SUBLANES):
            for k in range(TOP_K):
                row = dest_ref[0, 0, (g * SUBLANES + j) * TOP_K + k]
                pltpu.make_async_copy(pbuf.at[slot, g, pl.ds(j, 1)], xs_hbm.at[pl.ds(row, 1)],
                                      sems.at[slot]).start(priority=k % 2)
        return carry
    lax.fori_loop(0, tm // SUBLANES, body, 0)

    @pl.when(i == n - 1)
    def _():
        wait_slot(slot)

    @pl.when(jnp.logical_and(i == n - 1, n > 1))
    def _():
        wait_slot(1 - slot)


def _dispatch(clear, dest3, h_p, h_s, tm):
    (n_p, d), n_s = h_p.shape, h_s.shape[0]
    nb = clear.shape[0]
    groups = tm // SUBLANES
    rows3 = lambda h: h.reshape(h.shape[0] // SUBLANES, SUBLANES, d)
    grid_spec = pltpu.PrefetchScalarGridSpec(
        num_scalar_prefetch=1, grid=((n_p + n_s) // tm,),
        in_specs=[pl.BlockSpec((1, 1, tm * TOP_K), lambda i, cl: (i, 0, 0), memory_space=pltpu.SMEM),
                  *_token_specs(n_p // SUBLANES, groups, (SUBLANES, d), (0, 0))],
        out_specs=pl.BlockSpec(memory_space=pl.ANY),
        scratch_shapes=[pltpu.VMEM((2, groups, SUBLANES, d), F32), pltpu.VMEM((ROW_BLOCK, d), F32),
                        pltpu.SemaphoreType.DMA((2,)), pltpu.SemaphoreType.DMA(())])
    return pl.pallas_call(
        functools.partial(_dispatch_kernel, tm=tm, nb=nb, np_tiles=n_p // tm), grid_spec=grid_spec,
        out_shape=jax.ShapeDtypeStruct((nb * ROW_BLOCK, d), F32),
        compiler_params=_cparams(1), name="dispatch")(clear, dest3, rows3(h_p), rows3(h_s))


def _experts_kernel(be_ref, nact_ref, x_ref, g_hbm, u_hbm, d_hbm, y_ref, gf, uf, df, gb, ub, db, sems, slot_ref,
                    *, base, nb):
    i = pl.program_id(0)
    nact = nact_ref[0]

    def weight_copies(e, s):
        return (pltpu.make_async_copy(g_hbm.at[base + e], gf.at[s], sems.at[s]),
                pltpu.make_async_copy(u_hbm.at[base + e], uf.at[s], sems.at[s]),
                pltpu.make_async_copy(d_hbm.at[base + e], df.at[s], sems.at[s]))

    @pl.when(jnp.logical_and(i == 0, nact > 0))
    def _():
        slot_ref[0] = 0
        for c in weight_copies(be_ref[0], 0):
            c.start()

    first = jnp.logical_or(i == 0, be_ref[i] != be_ref[jnp.maximum(i - 1, 0)])

    @pl.when(jnp.logical_and(i < nact, first))
    def _():
        s = slot_ref[0]
        here = be_ref[i]
        nxt = lax.while_loop(lambda j: jnp.logical_and(j < nact, be_ref[jnp.minimum(j, nb - 1)] == here),
                             lambda j: j + 1, i + 1)

        @pl.when(nxt < nact)
        def _():
            for c in weight_copies(be_ref[jnp.minimum(nxt, nb - 1)], 1 - s):
                c.start()
        for c in weight_copies(here, s):
            c.wait()
        gb[...] = gf[s].astype(BF16)
        ub[...] = uf[s].astype(BF16)
        db[...] = df[s].astype(BF16)
        slot_ref[0] = 1 - s

    @pl.when(i < nact)
    def _():
        x = x_ref[...].astype(BF16)
        hid = _silu(_dot(x, gb[...])) * _dot(x, ub[...])
        y_ref[...] = _dot(hid.astype(BF16), db[...])


def _experts(block_e, nact, x_sorted, exp_gate, exp_up, exp_down, layer):
    nb = x_sorted.shape[0] // ROW_BLOCK
    _, d, de = exp_gate.shape
    base = layer * N_EXPERTS
    rows = pl.BlockSpec((ROW_BLOCK, d), lambda i, be, na: (jnp.minimum(i, jnp.maximum(na[0] - 1, 0)), 0))
    grid_spec = pltpu.PrefetchScalarGridSpec(
        num_scalar_prefetch=2, grid=(nb,),
        in_specs=[rows, pl.BlockSpec(memory_space=pl.ANY), pl.BlockSpec(memory_space=pl.ANY),
                  pl.BlockSpec(memory_space=pl.ANY)],
        out_specs=rows,
        scratch_shapes=[pltpu.VMEM((2, d, de), F32), pltpu.VMEM((2, d, de), F32), pltpu.VMEM((2, de, d), F32),
                        pltpu.VMEM((d, de), BF16), pltpu.VMEM((d, de), BF16), pltpu.VMEM((de, d), BF16),
                        pltpu.SemaphoreType.DMA((2,)), pltpu.SMEM((1,), I32)])
    return pl.pallas_call(
        functools.partial(_experts_kernel, base=base, nb=nb), grid_spec=grid_spec,
        out_shape=jax.ShapeDtypeStruct((nb * ROW_BLOCK, d), F32),
        input_output_aliases={2: 0},
        compiler_params=_cparams(1), name="experts")(block_e, nact, x_sorted, exp_gate, exp_up, exp_down)


def _combine_kernel(cur_ref, nxt_ref, y_hbm, wts_ref, x1_ref, h_ref, gt_ref, sg_ref, su_ref, sd_ref,
                    nf_ref, x2_ref, yn_ref, ybuf, sems, *, tm):
    i = pl.program_id(0)
    n = pl.num_programs(0)
    slot = i % 2
    groups = tm // SUBLANES

    @pl.when(i == 0)
    def _():
        _gather_rows(cur_ref, y_hbm, ybuf.at[0], sems.at[0])

    @pl.when(i + 1 < n)
    def _():
        _gather_rows(nxt_ref, y_hbm, ybuf.at[1 - slot], sems.at[1 - slot])

    hb = h_ref[...].astype(BF16)
    hid = _silu(_dot(hb, sg_ref[...])) * _dot(hb, su_ref[...])
    acc = _dot(hid.astype(BF16), sd_ref[...])
    _wait_rows(ybuf.at[slot], sems.at[slot])
    wts = wts_ref[...]
    for k in range(TOP_K):
        y_k = ybuf[slot, k * groups:(k + 1) * groups].reshape(tm, acc.shape[1])
        acc = acc + y_k * wts[:, k:k + 1]
    x2 = x1_ref[...] + gt_ref[0] * acc
    x2_ref[...] = x2
    yn_ref[...] = _rms(x2, nf_ref[...])


def _combine(dest3, y_sorted, wts, x1, h2, mod3, sh_gate_b, sh_up_b, sh_down_b, norm_final, tm, tiles_per_batch):
    n, d = x1.shape
    nt = n // tm
    rows = mod3.shape[1]
    ds = sh_gate_b.shape[1]
    tok = lambda w: pl.BlockSpec((tm, w), lambda i: (i, 0))
    kern = functools.partial(_combine_kernel, tm=tm)
    return pl.pallas_call(
        kern, grid=(nt,),
        in_specs=[pl.BlockSpec((1, 1, tm * TOP_K), lambda i: (i, 0, 0), memory_space=pltpu.SMEM),
                  pl.BlockSpec((1, 1, tm * TOP_K), lambda i: (jnp.minimum(i + 1, nt - 1), 0, 0),
                               memory_space=pltpu.SMEM),
                  pl.BlockSpec(memory_space=pl.ANY),
                  tok(LANES), tok(d), tok(d),
                  _mod_spec(rows, d, tiles_per_batch, 5),
                  pl.BlockSpec((d, ds), lambda i: (0, 0)),
                  pl.BlockSpec((d, ds), lambda i: (0, 0)),
                  pl.BlockSpec((ds, d), lambda i: (0, 0)),
                  pl.BlockSpec((1, d), lambda i: (0, 0))],
        out_specs=[tok(d), tok(d)],
        out_shape=[jax.ShapeDtypeStruct((n, d), F32)] * 2,
        scratch_shapes=[pltpu.VMEM((2, tm * TOP_K // SUBLANES, SUBLANES, d), F32),
                        pltpu.SemaphoreType.DMA((2,))],
        compiler_params=_cparams(1), name="combine")(
            dest3, dest3, y_sorted, wts, x1, h2, mod3, sh_gate_b, sh_up_b, sh_down_b, norm_final)


def _last_rows(buf, new, keep):
    t = new.shape[1]
    if t >= keep:
        return new[:, t - keep:]
    return jnp.concatenate([buf[:, t:], new], axis=1)


def _mixers(u, qkv, z, ba, b, t, pool_buf, conv_buf, s0, start_pos, lw):
    t_pool = -(-t // 8) * 8
    t_gdn = -(-t // CHUNK) * CHUNK
    u3 = u.reshape(b, t, POOL_W)
    qkv3 = qkv.reshape(b, t, CONV_CH)
    pad_t = lambda a, tp: a if tp == t else jnp.pad(a, ((0, 0), (0, tp - t), (0, 0)))
    buf16 = jnp.pad(pool_buf, ((0, 0), (POOL_HALO - POOL_BUF, 0), (0, 0)))
    pool_out = _pool(pad_t(u3, t_pool), buf16, lw["pool_w"], lw["pool_scale"], start_pos)[:, :t]
    if t == 1:
        dn_out, s_new = _gdn_step(qkv, z, ba, conv_buf, s0, lw["conv_w"], lw["par"], lw["dn_norm"])
    else:
        cbuf8 = jnp.pad(conv_buf, ((0, 0), (CONV_HALO - (CONV_W - 1), 0), (0, 0)))
        dn_out, s_new = _gdn(pad_t(qkv3, t_gdn), pad_t(z.reshape(b, t, DN_W), t_gdn),
                             pad_t(ba.reshape(b, t, BA_W), t_gdn), cbuf8, s0, lw["conv_w"], lw["par"],
                             lw["dn_norm"], t)
        dn_out = dn_out[:, :t].reshape(b * t, DN_W)
    new_pool = _last_rows(pool_buf, u3, POOL_BUF)
    new_conv = _last_rows(conv_buf, qkv3, CONV_W - 1)
    return pool_out.reshape(b * t, POOL_W), dn_out, new_pool, new_conv, s_new


def _block_tables(counts, n_tok):
    nk = n_tok * TOP_K
    nb = -(-(nk + N_EXPERTS * (ROW_BLOCK - 1)) // ROW_BLOCK)
    pcounts = (counts + ROW_BLOCK - 1) // ROW_BLOCK * ROW_BLOCK
    pends = jnp.cumsum(pcounts)
    pstarts = pends - pcounts
    first_row = jnp.arange(nb, dtype=I32)[:, None] * ROW_BLOCK
    block_e = jnp.minimum(jnp.sum((pends[None, :] <= first_row).astype(I32), axis=1), N_EXPERTS - 1)
    nact = (pends[-1] // ROW_BLOCK).astype(I32).reshape(1)
    blk = jnp.arange(nb, dtype=I32)
    next_e = jnp.concatenate([block_e[1:], jnp.full((1,), -1, I32)])
    clear = ((blk >= nact[0] - 1) | (next_e != block_e)).astype(I32)
    return pstarts.astype(F32).reshape(1, N_EXPERTS), block_e, nact, clear


def _dest_tiles(dest, tm, k_major):
    n = dest.shape[0]
    tiles = dest.reshape(n // tm, tm, TOP_K)
    if k_major:
        tiles = tiles.transpose(0, 2, 1)
    return tiles.reshape(n // tm, 1, tm * TOP_K)


def kernel(x_prompt, x_sample, state_pool, state_conv, state_delta, c_prompt, c_sample, norm_mix, norm_ffn, w_ada, b_ada, w_in, pool_w, pool_scale, conv_w, a_log, dt_bias, dn_norm, w_out, router_w, router_bias, exp_gate, exp_up, exp_down, sh_gate, sh_up, sh_down, norm_final):
    bp, tp, d = x_prompt.shape
    bs, ts, _ = x_sample.shape
    depth = w_ada.shape[0]
    past_len = PAST_LEN
    n_p, n_s = bp * tp, bs * ts
    n_tok = n_p + n_s
    tm_p = min(512, tp)
    tm_c = min(128, tp)
    tm_r = next(t for t in (128, 64, 32, 16, 8) if n_p % t == 0 and n_s % t == 0)
    assert tp % tm_p == 0 and ts == 1 and bs % 8 == 0
    n_exp = exp_gate.shape[1]
    gate_w = exp_gate.reshape((depth * n_exp,) + exp_gate.shape[2:])
    up_w = exp_up.reshape((depth * n_exp,) + exp_up.shape[2:])
    down_w = exp_down.reshape((depth * n_exp,) + exp_down.shape[2:])

    c_all = jnp.concatenate([c_prompt, c_sample], axis=0)
    m_rows = -(-c_all.shape[0] // 8) * 8
    mod_all = _ada(jnp.pad(c_all, ((0, m_rows - c_all.shape[0]), (0, 0))), w_ada, b_ada)

    c2 = POOL_W + CONV_CH + DN_W
    lane_pad = jnp.zeros((depth, d, LANES - DN_HEADS), F32)
    w_in_p = jnp.concatenate([w_in[:, :, :c2], w_in[:, :, c2:c2 + DN_HEADS], lane_pad,
                              w_in[:, :, c2 + DN_HEADS:], lane_pad], axis=-1).astype(BF16)
    head_pad = lambda a: jnp.pad(a, ((0, 0), (0, LANES - DN_HEADS)))
    par_all = jnp.stack([head_pad(a_log), head_pad(dt_bias)] + [jnp.zeros((depth, LANES), F32)] * 6, axis=1)
    conv_w8 = jnp.pad(conv_w, ((0, 0), (0, 8 - CONV_W), (0, 0)))

    xp = x_prompt.reshape(n_p, d)
    xs = x_sample.reshape(n_s, d)
    zero_pool = jnp.zeros((bp, POOL_BUF, POOL_W), F32)
    zero_conv = jnp.zeros((bp, CONV_W - 1, CONV_CH), F32)
    zero_state = jnp.zeros((bp, DN_HEADS, DN_D, DN_D), F32)
    outs = {k: [] for k in ("pool_p", "conv_p", "delta_p", "pool_s", "conv_s", "delta_s")}
    yp = ys = None
    for l in range(depth):
        lw = {"pool_w": pool_w[l].astype(BF16), "pool_scale": pool_scale[l].reshape(1, POOL_W),
              "conv_w": conv_w8[l], "par": par_all[l], "dn_norm": dn_norm[l].reshape(1, DN_D)}
        mod_p = mod_all[l, :bp].reshape(bp, 1, 6 * d)
        mod_s = mod_all[l, bp:bp + bs].reshape(1, bs, 6 * d)
        nw_mix = norm_mix[l].reshape(1, d)
        nw_ffn = norm_ffn[l].reshape(1, d)
        w_out_b = w_out[l].astype(BF16)

        up, qkvp, zp, bap = _norm_proj(xp, mod_p, nw_mix, w_in_p[l], tm_p, tp // tm_p)
        us, qkvs, zs, bas = _norm_proj(xs, mod_s, nw_mix, w_in_p[l], n_s, 1)
        pool_p, dn_p, npool_p, nconv_p, ns_p = _mixers(up, qkvp, zp, bap, bp, tp, zero_pool, zero_conv,
                                                       zero_state, 0, lw)
        pool_s, dn_s, npool_s, nconv_s, ns_s = _mixers(us, qkvs, zs, bas, bs, ts, state_pool[l], state_conv[l],
                                                       state_delta[l], past_len, lw)
        for key, val in (("pool_p", npool_p), ("conv_p", nconv_p), ("delta_p", ns_p),
                         ("pool_s", npool_s), ("conv_s", nconv_s), ("delta_s", ns_s)):
            outs[key].append(val)

        x1p, h2p = _out_proj(pool_p, dn_p, xp, mod_p, nw_ffn, w_out_b, tm_p, tp // tm_p)
        x1s, h2s = _out_proj(pool_s, dn_s, xs, mod_s, nw_ffn, w_out_b, n_s, 1)

        rw_b, rb = router_w[l].astype(BF16), router_bias[l].reshape(1, N_EXPERTS)
        eidx_p, wts_p, rank_p, counts = _router(h2p, rw_b, rb, jnp.zeros((1, N_EXPERTS), F32))
        eidx_s, wts_s, rank_s, counts = _router(h2s, rw_b, rb, counts)
        eidx, rank = jnp.concatenate([eidx_p, eidx_s]), jnp.concatenate([rank_p, rank_s])
        pstarts, block_e, nact, clear = _block_tables(counts.reshape(-1).astype(I32), n_tok)
        dest = _dest(eidx, rank, pstarts, tm_r)[:, :TOP_K]
        x_sorted = _dispatch(clear, _dest_tiles(dest, tm_r, False), h2p, h2s, tm_r)
        y_sorted = _experts(block_e, nact, x_sorted, gate_w, up_w, down_w, l)

        shg, shu, shd = sh_gate[l].astype(BF16), sh_up[l].astype(BF16), sh_down[l].astype(BF16)
        nf = norm_final.reshape(1, d)
        xp, yp = _combine(_dest_tiles(dest[:n_p], tm_c, True), y_sorted, wts_p, x1p, h2p, mod_p, shg, shu, shd,
                          nf, tm_c, tp // tm_c)
        xs, ys = _combine(_dest_tiles(dest[n_p:], n_s, True), y_sorted, wts_s, x1s, h2s, mod_s, shg, shu, shd,
                          nf, n_s, 1)

    stack = lambda key: jnp.stack(outs[key])
    return (yp.reshape(bp, tp, d), ys.reshape(bs, ts, d), stack("pool_p"), stack("conv_p"), stack("delta_p"),
            stack("pool_s"), stack("conv_s"), stack("delta_s"))
```

```python
import functools
import math

import jax
import jax.numpy as jnp
from jax import lax
from jax.experimental import pallas as pl
from jax.experimental.pallas import tpu as pltpu

F32, BF16, I32 = jnp.float32, jnp.bfloat16, jnp.int32
HI = lax.Precision.HIGHEST
EPS = 1e-6

LANES = 128
POOL_WINDOWS = (2, 4, 8, 16)
POOL_GW = 128
POOL_W = 512
POOL_BUF = 15
POOL_HALO = 16
POOL_TILE = 512
DN_HEADS = 4
DN_D = 128
DN_W = 512
CONV_W = 4
CONV_CH = 3 * DN_W
CONV_HALO = 8
CHUNK = 64
INV_BLOCK = 16
GDN_ROWS = 4
N_EXPERTS = 256
TOP_K = 8
N_GROUPS = 8
TOPK_GROUPS = 4
GROUP_SHIFT = 5
ROUTED_SCALE = 2.5
PAST_LEN = 16384
ROW_SHIFT = 8
ROW_BLOCK = 1 << ROW_SHIFT
BA_W = 2 * LANES
D_IN_PAD = POOL_W + CONV_CH + DN_W + BA_W
VMEM_LIMIT = 48 * 2 ** 20


def _cparams(n_axes):
    return pltpu.CompilerParams(dimension_semantics=("arbitrary",) * n_axes,
                                vmem_limit_bytes=VMEM_LIMIT)


def _sigmoid(x):
    return 1.0 / (1.0 + jnp.exp(-x))


def _silu(x):
    return x * _sigmoid(x)


def _softplus(x):
    return jnp.maximum(x, 0.0) + jnp.log(1.0 + jnp.exp(-jnp.abs(x)))


def _rms(x, w):
    return x * lax.rsqrt(jnp.mean(x * x, axis=-1, keepdims=True) + EPS) * w


def _dot(a, b, precision=None):
    return jnp.dot(a, b, preferred_element_type=F32, precision=precision)


def _dot_nt(a, b, precision=None):
    return lax.dot_general(a, b, (((1,), (1,)), ((), ())), preferred_element_type=F32,
                           precision=precision)


def _ada_kernel(c_ref, w_ref, b_ref, o_ref):
    o_ref[0] = _dot(_silu(c_ref[...]), w_ref[0], HI) + b_ref[0]


def _ada(c_all, w_ada, b_ada):
    depth, d, d6 = w_ada.shape
    m = c_all.shape[0]
    tn = 1024
    return pl.pallas_call(
        _ada_kernel, grid=(depth, d6 // tn),
        in_specs=[pl.BlockSpec((m, d), lambda l, j: (0, 0)),
                  pl.BlockSpec((1, d, tn), lambda l, j: (l, 0, j)),
                  pl.BlockSpec((1, 1, tn), lambda l, j: (l, 0, j))],
        out_specs=pl.BlockSpec((1, m, tn), lambda l, j: (l, 0, j)),
        out_shape=jax.ShapeDtypeStruct((depth, m, d6), F32),
        compiler_params=_cparams(2), name="ada")(c_all, w_ada, b_ada.reshape(depth, 1, d6))


def _mod_spec(rows, d, tiles_per_batch, chunk):
    return pl.BlockSpec((1, rows, d), lambda i: (i // tiles_per_batch, 0, chunk))


def _norm_proj_kernel(x_ref, sh_ref, sc_ref, nw_ref, w_ref, u_ref, qkv_ref, z_ref, ba_ref):
    h = _rms(x_ref[...], nw_ref[...]) * (1.0 + sc_ref[0]) + sh_ref[0]
    hb = h.astype(BF16)
    c0, c1, c2 = POOL_W, POOL_W + CONV_CH, POOL_W + CONV_CH + DN_W
    u_ref[...] = _dot(hb, w_ref[:, :c0])
    qkv_ref[...] = _dot(hb, w_ref[:, c0:c1])
    z_ref[...] = _dot(hb, w_ref[:, c1:c2])
    ba_ref[...] = _dot(hb, w_ref[:, c2:])


def _norm_proj(x2d, mod3, nw, w_in_p, tm, tiles_per_batch):
    n, d = x2d.shape
    rows = mod3.shape[1]
    widths = (POOL_W, CONV_CH, DN_W, BA_W)
    return pl.pallas_call(
        _norm_proj_kernel, grid=(n // tm,),
        in_specs=[pl.BlockSpec((tm, d), lambda i: (i, 0)),
                  _mod_spec(rows, d, tiles_per_batch, 0),
                  _mod_spec(rows, d, tiles_per_batch, 1),
                  pl.BlockSpec((1, d), lambda i: (0, 0)),
                  pl.BlockSpec((d, D_IN_PAD), lambda i: (0, 0))],
        out_specs=[pl.BlockSpec((tm, w), lambda i: (i, 0)) for w in widths],
        out_shape=[jax.ShapeDtypeStruct((n, w), F32) for w in widths],
        compiler_params=_cparams(1), name="norm_proj")(x2d, mod3, mod3, nw, w_in_p)


def _pool_kernel(u_ref, buf_ref, pw_ref, ps_ref, o_ref, ext, *, tt, gb, start_pos):
    t = pl.program_id(1)

    @pl.when(t == 0)
    def _():
        ext[:, 0:POOL_HALO, :] = buf_ref[...]

    @pl.when(t > 0)
    def _():
        ext[:, 0:POOL_HALO, :] = ext[:, tt:tt + POOL_HALO, :]

    ext[:, POOL_HALO:POOL_HALO + tt, :] = u_ref[...]
    pos = start_pos + t * tt + lax.broadcasted_iota(I32, (tt, 1), 0)
    for g, win in enumerate(POOL_WINDOWS):
        sl = slice(g * POOL_GW, (g + 1) * POOL_GW)
        cnt = jnp.minimum(pos + 1, win).astype(F32)
        pooled = []
        for rb in range(gb):
            u = u_ref[rb, :, sl]
            s = u
            for j in range(1, win):
                s = s + ext[rb, POOL_HALO - j:POOL_HALO - j + tt, sl]
            pooled.append(s / cnt - u)
        mixed = _dot(jnp.concatenate(pooled, axis=0).astype(BF16), pw_ref[g]) * ps_ref[:, sl]
        for rb in range(gb):
            o_ref[rb, :, sl] = mixed[rb * tt:(rb + 1) * tt]


def _pool(u3, buf16, pool_w_b, pool_scale, start_pos):
    b, t, w = u3.shape
    tt = min(t, POOL_TILE)
    assert t % tt == 0 and tt % 8 == 0
    gb = next(g for g in (16, 8, 4, 2, 1) if b % g == 0 and g * tt <= POOL_TILE)
    return pl.pallas_call(
        functools.partial(_pool_kernel, tt=tt, gb=gb, start_pos=start_pos), grid=(b // gb, t // tt),
        in_specs=[pl.BlockSpec((gb, tt, w), lambda i, j: (i, j, 0)),
                  pl.BlockSpec((gb, POOL_HALO, w), lambda i, j: (i, 0, 0)),
                  pl.BlockSpec((len(POOL_WINDOWS), POOL_GW, POOL_GW), lambda i, j: (0, 0, 0)),
                  pl.BlockSpec((1, w), lambda i, j: (0, 0))],
        out_specs=pl.BlockSpec((gb, tt, w), lambda i, j: (i, j, 0)),
        out_shape=jax.ShapeDtypeStruct((b, t, w), F32),
        scratch_shapes=[pltpu.VMEM((gb, POOL_HALO + tt, w), F32)],
        compiler_params=_cparams(2), name="pool")(u3, buf16, pool_w_b, pool_scale)


def _bf(x):
    return x.astype(BF16)


def _dot3(a, b):
    a_hi, b_hi = _bf(a), _bf(b)
    a_lo, b_lo = _bf(a - a_hi.astype(F32)), _bf(b - b_hi.astype(F32))
    return _dot(a_hi, b_hi) + (_dot(a_hi, b_lo) + _dot(a_lo, b_hi))


def _each(fn, *lists):
    return [fn(*args) for args in zip(*lists)]


def _unit_lower_inverse(a, row, col, eye):
    same = (row // INV_BLOCK) == (col // INV_BLOCK)
    dm = _each(lambda x: jnp.where(same, -x, 0.0), a)
    lm = _each(lambda x, d: -x - d, a, dm)
    mm = lambda x, y: _dot(_bf(x), _bf(y))
    d2 = _each(mm, dm, dm)
    d4 = _each(mm, d2, d2)
    d8 = _each(mm, d4, d4)
    p = _each(lambda x, y: mm(eye + x, eye + y), dm, d2)
    p2 = _each(lambda x, y: mm(eye + x, eye + y), d4, d8)
    p = _each(mm, p, p2)
    m = _each(mm, p, lm)
    m2 = _each(mm, m, m)
    q = _each(lambda x, y: mm(eye + x, eye + y), m, m2)
    t0 = _each(mm, q, p)
    at = _each(_dot3, a, t0)
    resid = _each(lambda t, x: (eye - t) - x, t0, at)
    return _each(lambda t, r: t + mm(t, r), t0, resid)


def _gdn_kernel(qkv_ref, z_ref, ba_ref, cbuf_ref, s0_ref, cw_ref, par_ref, nw_ref,
                o_ref, sfin_ref, state, xext, *, t_len, gb):
    c = pl.program_id(1)
    seqs = range(gb)
    heads = range(DN_HEADS)
    hc = DN_HEADS * CHUNK

    @pl.when(c == 0)
    def _():
        for rb in seqs:
            state[rb * DN_HEADS:(rb + 1) * DN_HEADS] = s0_ref[rb]
            xext[rb, 0:CONV_HALO, :] = cbuf_ref[rb]

    @pl.when(c > 0)
    def _():
        for rb in seqs:
            xext[rb, 0:CONV_HALO, :] = xext[rb, CHUNK:CHUNK + CONV_HALO, :]

    valid = (c * CHUNK + lax.broadcasted_iota(I32, (CHUNK, 1), 0)) < t_len
    vf = valid.astype(F32)
    head_lane = lax.broadcasted_iota(I32, (1, LANES), 1) < DN_HEADS
    decay = jnp.where(head_lane, -jnp.exp(par_ref[0:1, :]), 0.0)
    r64 = lax.broadcasted_iota(I32, (CHUNK, CHUNK), 0)
    c64 = lax.broadcasted_iota(I32, (CHUNK, CHUNK), 1)
    tri = (r64 >= c64).astype(F32)
    pick = (lax.broadcasted_iota(I32, (8, LANES), 0) == lax.broadcasted_iota(I32, (8, LANES), 1)).astype(F32)
    row = lax.broadcasted_iota(I32, (hc, hc), 0)
    col = lax.broadcasted_iota(I32, (hc, hc), 1)
    incl = ((row // CHUNK) == (col // CHUNK)) & (row >= col)
    strict = row > col
    eye = (row == col).astype(F32)
    base = CONV_HALO - (CONV_W - 1)

    def conv(rb):
        xext[rb, CONV_HALO:CONV_HALO + CHUNK, :] = qkv_ref[rb]
        acc = xext[rb, base:base + CHUNK, :] * cw_ref[0:1, :]
        for j in range(1, CONV_W):
            acc = acc + xext[rb, base + j:base + j + CHUNK, :] * cw_ref[j:j + 1, :]
        return _silu(acc)
    qkvc = [conv(rb) for rb in seqs]
    ba = [ba_ref[rb] for rb in seqs]
    beta_all = _each(lambda x: _sigmoid(x[:, :LANES]) * vf, ba)
    g_all = _each(lambda x: decay * _softplus(x[:, LANES:] + par_ref[1:2, :]) * vf, ba)
    gam_all = _each(lambda g: _dot(tri, g, HI), g_all)
    gam_rows = _each(lambda g: _dot_nt(pick, g, HI), gam_all)

    def unit(x):
        return x * lax.rsqrt(jnp.sum(x * x, axis=-1, keepdims=True) + EPS)
    q_s = _each(lambda x: jnp.concatenate(
        [unit(x[:, h * DN_D:(h + 1) * DN_D]) * (DN_D ** -0.5) * vf for h in heads], axis=0), qkvc)
    k_s = _each(lambda x: jnp.concatenate(
        [unit(x[:, DN_W + h * DN_D:DN_W + (h + 1) * DN_D]) * vf for h in heads], axis=0), qkvc)
    v_s = _each(lambda x: jnp.concatenate(
        [x[:, 2 * DN_W + h * DN_D:2 * DN_W + (h + 1) * DN_D] * vf for h in heads], axis=0), qkvc)
    beta = _each(lambda x: jnp.concatenate([x[:, h:h + 1] for h in heads], axis=0), beta_all)
    gam = _each(lambda x: jnp.concatenate([x[:, h:h + 1] for h in heads], axis=0), gam_all)
    gam_r = _each(lambda x: jnp.concatenate([x[h:h + 1, :] for h in heads], axis=1), gam_rows)

    dec = _each(lambda g, gr: jnp.where(incl, jnp.exp(jnp.where(incl, g - gr, 0.0)), 0.0), gam, gam_r)
    kb = _each(_bf, k_s)
    qb = _each(_bf, q_s)
    kk = _each(_dot_nt, kb, kb)
    qk = _each(_dot_nt, qb, kb)
    a_mat = _each(lambda b_, x, d_: b_ * x * jnp.where(strict, d_, 0.0), beta, kk, dec)
    qk = _each(lambda x, d_: _bf(x * d_), qk, dec)
    egam = _each(jnp.exp, gam)
    rhs = _each(lambda b_, e, k, v: jnp.concatenate([(b_ * e) * k, b_ * v], axis=-1),
                beta, egam, k_s, v_s)
    t_inv = _unit_lower_inverse(a_mat, row, col, eye)
    sol = _each(_dot3, t_inv, rhs)
    wb = _each(lambda x: _bf(x[:, :DN_D]), sol)
    uv = _each(lambda x: x[:, DN_D:], sol)

    hrows = [slice(h * CHUNK, (h + 1) * CHUNK) for h in heads]
    s_f = [[state[rb * DN_HEADS + h] for h in heads] for rb in seqs]
    s_b = [[_bf(x) for x in per_seq] for per_seq in s_f]
    ub = _each(lambda u_, w_, s_: _bf(jnp.concatenate(
        [u_[hrows[h]] - _dot(w_[hrows[h]], s_[h]) for h in heads], axis=0)), uv, wb, s_b)
    o_intra = _each(_dot, qk, ub)
    for h in heads:
        rows = hrows[h]
        sl = slice(h * DN_D, (h + 1) * DN_D)
        o = _each(lambda e, q_, s_, oi: e[rows] * _dot(q_[rows], s_[h]) + oi[rows], egam, qb, s_b, o_intra)
        g_last = _each(lambda g: g[(h + 1) * CHUNK - 1:(h + 1) * CHUNK, :], gam)
        k_dec = _each(lambda k, gl, g: _bf((k[rows] * jnp.exp(gl - g[rows])).T), k_s, g_last, gam)
        upd = _each(lambda kd, u_: _dot(kd, u_[rows]), k_dec, ub)
        for rb in seqs:
            state[rb * DN_HEADS + h] = jnp.exp(g_last[rb]) * s_f[rb][h] + upd[rb]
            o_ref[rb, :, sl] = _rms(o[rb], nw_ref[...]) * _silu(z_ref[rb, :, sl])

    @pl.when(c == pl.num_programs(1) - 1)
    def _():
        for rb in seqs:
            sfin_ref[rb] = state[rb * DN_HEADS:(rb + 1) * DN_HEADS]


def _gdn(qkv3, z3, ba3, cbuf8, s0, conv_w8, par, dn_norm, t_len):
    b, tp, _ = qkv3.shape
    assert tp % CHUNK == 0
    gb = GDN_ROWS if b % GDN_ROWS == 0 else 1
    kern = functools.partial(_gdn_kernel, t_len=t_len, gb=gb)
    return pl.pallas_call(
        kern, grid=(b // gb, tp // CHUNK),
        in_specs=[pl.BlockSpec((gb, CHUNK, CONV_CH), lambda i, j: (i, j, 0)),
                  pl.BlockSpec((gb, CHUNK, DN_W), lambda i, j: (i, j, 0)),
                  pl.BlockSpec((gb, CHUNK, BA_W), lambda i, j: (i, j, 0)),
                  pl.BlockSpec((gb, CONV_HALO, CONV_CH), lambda i, j: (i, 0, 0)),
                  pl.BlockSpec((gb, DN_HEADS, DN_D, DN_D), lambda i, j: (i, 0, 0, 0)),
                  pl.BlockSpec((8, CONV_CH), lambda i, j: (0, 0)),
                  pl.BlockSpec((8, LANES), lambda i, j: (0, 0)),
                  pl.BlockSpec((1, DN_D), lambda i, j: (0, 0))],
        out_specs=[pl.BlockSpec((gb, CHUNK, DN_W), lambda i, j: (i, j, 0)),
                   pl.BlockSpec((gb, DN_HEADS, DN_D, DN_D), lambda i, j: (i, 0, 0, 0))],
        out_shape=[jax.ShapeDtypeStruct((b, tp, DN_W), F32),
                   jax.ShapeDtypeStruct((b, DN_HEADS, DN_D, DN_D), F32)],
        scratch_shapes=[pltpu.VMEM((gb * DN_HEADS, DN_D, DN_D), F32),
                        pltpu.VMEM((gb, CONV_HALO + CHUNK, CONV_CH), F32)],
        compiler_params=_cparams(2), name="gdn")(qkv3, z3, ba3, cbuf8, s0, conv_w8, par, dn_norm)


STEP_TILE = 8


def _gdn_step_kernel(qkv_ref, z_ref, ba_ref, cst_ref, s0_ref, cw_ref, par_ref, nw_ref, o_ref, s_ref):
    acc = qkv_ref[...] * cw_ref[CONV_W - 1:CONV_W, :]
    for j in range(CONV_W - 1):
        acc = acc + cst_ref[:, j, :] * cw_ref[j:j + 1, :]
    qkvc = _silu(acc)
    ba = ba_ref[...]
    head_lane = lax.broadcasted_iota(I32, (1, LANES), 1) < DN_HEADS
    beta_all = _sigmoid(ba[:, :LANES])
    decay = jnp.where(head_lane, -jnp.exp(par_ref[0:1, :]), 0.0)
    eg_all = jnp.exp(decay * _softplus(ba[:, LANES:] + par_ref[1:2, :]))
    pad6 = jnp.zeros((6, DN_D), F32)
    pad7 = jnp.zeros((7, DN_D), F32)
    for h in range(DN_HEADS):
        q = qkvc[:, h * DN_D:(h + 1) * DN_D]
        k = qkvc[:, DN_W + h * DN_D:DN_W + (h + 1) * DN_D]
        v = qkvc[:, 2 * DN_W + h * DN_D:2 * DN_W + (h + 1) * DN_D]
        q = q * lax.rsqrt(jnp.sum(q * q, axis=-1, keepdims=True) + EPS) * (DN_D ** -0.5)
        k = k * lax.rsqrt(jnp.sum(k * k, axis=-1, keepdims=True) + EPS)
        qk = _dot_nt(_bf(q), _bf(k))
        seqs = list(range(STEP_TILE))
        beta = [beta_all[b:b + 1, h:h + 1] for b in seqs]
        eg = [eg_all[b:b + 1, h:h + 1] for b in seqs]
        s_old = [s0_ref[b, h] for b in seqs]
        ks_qs = _each(lambda b, s_: _dot(_bf(jnp.concatenate([k[b:b + 1], q[b:b + 1], pad6], axis=0)), _bf(s_)),
                      seqs, s_old)
        u = _each(lambda b, b_, e, r: b_ * (v[b:b + 1] - e * r[0:1]), seqs, beta, eg, ks_qs)
        o_rows = _each(lambda b, e, r, u_: e * r[1:2] + qk[b:b + 1, b:b + 1] * u_, seqs, eg, ks_qs, u)
        k_col = [_bf(jnp.concatenate([k[b:b + 1], pad7], axis=0).T) for b in seqs]
        upd = _each(lambda kc, u_: _dot(kc, _bf(jnp.concatenate([u_, pad7], axis=0))), k_col, u)
        for b in seqs:
            s_ref[b, h] = eg[b] * s_old[b] + upd[b]
        o = jnp.concatenate(o_rows, axis=0)
        sl = slice(h * DN_D, (h + 1) * DN_D)
        o_ref[:, sl] = _rms(o, nw_ref[...]) * _silu(z_ref[:, sl])


def _gdn_step(qkv2, z2, ba2, conv_state, s0, conv_w8, par, dn_norm):
    b = qkv2.shape[0]
    assert b % STEP_TILE == 0
    row = lambda w: pl.BlockSpec((STEP_TILE, w), lambda i: (i, 0))
    st = pl.BlockSpec((STEP_TILE, DN_HEADS, DN_D, DN_D), lambda i: (i, 0, 0, 0))
    return pl.pallas_call(
        _gdn_step_kernel, grid=(b // STEP_TILE,),
        in_specs=[row(CONV_CH), row(DN_W), row(BA_W),
                  pl.BlockSpec((STEP_TILE, CONV_W - 1, CONV_CH), lambda i: (i, 0, 0)), st,
                  pl.BlockSpec((8, CONV_CH), lambda i: (0, 0)),
                  pl.BlockSpec((8, LANES), lambda i: (0, 0)),
                  pl.BlockSpec((1, DN_D), lambda i: (0, 0))],
        out_specs=[row(DN_W), st],
        out_shape=[jax.ShapeDtypeStruct((b, DN_W), F32),
                   jax.ShapeDtypeStruct((b, DN_HEADS, DN_D, DN_D), F32)],
        compiler_params=_cparams(1), name="gdn_step")(qkv2, z2, ba2, conv_state, s0, conv_w8, par, dn_norm)


def _out_proj_kernel(pool_ref, dn_ref, x_ref, gt_ref, sh_ref, sc_ref, nw_ref, w_ref, x1_ref, h2_ref):
    mix = _dot(pool_ref[...].astype(BF16), w_ref[:POOL_W, :]) + _dot(dn_ref[...].astype(BF16), w_ref[POOL_W:, :])
    x1 = x_ref[...] + gt_ref[0] * mix
    x1_ref[...] = x1
    h2_ref[...] = _rms(x1, nw_ref[...]) * (1.0 + sc_ref[0]) + sh_ref[0]


def _out_proj(pool2, dn2, x2d, mod3, nw, w_out_b, tm, tiles_per_batch):
    n, d = x2d.shape
    rows = mod3.shape[1]
    tok = lambda w: pl.BlockSpec((tm, w), lambda i: (i, 0))
    return pl.pallas_call(
        _out_proj_kernel, grid=(n // tm,),
        in_specs=[tok(POOL_W), tok(DN_W), tok(d),
                  _mod_spec(rows, d, tiles_per_batch, 2),
                  _mod_spec(rows, d, tiles_per_batch, 3),
                  _mod_spec(rows, d, tiles_per_batch, 4),
                  pl.BlockSpec((1, d), lambda i: (0, 0)),
                  pl.BlockSpec((POOL_W + DN_W, d), lambda i: (0, 0))],
        out_specs=[tok(d), tok(d)],
        out_shape=[jax.ShapeDtypeStruct((n, d), F32)] * 2,
        compiler_params=_cparams(1), name="out_proj")(pool2, dn2, x2d, mod3, mod3, mod3, nw, w_out_b)


def _router_kernel(h_ref, rw_ref, rb_ref, cnt0_ref, eidx_ref, wts_ref, rank_ref, cnt_ref, run, *, tm):
    i = pl.program_id(0)

    @pl.when(i == 0)
    def _():
        run[...] = cnt0_ref[...]

    s = _sigmoid(_dot(h_ref[...].astype(BF16), rw_ref[...]))
    sel = s + rb_ref[...]
    lane_i = lax.broadcasted_iota(I32, (tm, N_EXPERTS), 1)
    lane = lane_i.astype(F32)
    grp = lane_i >> GROUP_SHIFT
    neg = -jnp.inf

    def first_max(x):
        m = jnp.max(x, axis=-1, keepdims=True)
        idx = jnp.min(jnp.where(x == m, lane, float(N_EXPERTS)), axis=-1, keepdims=True)
        return m, idx

    gs = []
    for g in range(N_GROUPS):
        xg = jnp.where(grp == g, sel, neg)
        m1, i1 = first_max(xg)
        m2 = jnp.max(jnp.where(lane == i1, neg, xg), axis=-1, keepdims=True)
        gs.append(m1 + m2)
    cur = jnp.full((tm, N_EXPERTS), neg, F32)
    for g in range(N_GROUPS):
        ahead = jnp.zeros((tm, 1), F32)
        for o in range(N_GROUPS):
            if o < g:
                ahead = ahead + jnp.where(gs[o] >= gs[g], 1.0, 0.0)
            elif o > g:
                ahead = ahead + jnp.where(gs[o] > gs[g], 1.0, 0.0)
        cur = jnp.where((grp == g) & (ahead < TOPK_GROUPS), sel, cur)

    hits, idxs, ws = [], [], []
    chosen = jnp.zeros((tm, N_EXPERTS), F32)
    for _ in range(TOP_K):
        _, idx = first_max(cur)
        hit = lane == idx
        ws.append(jnp.sum(jnp.where(hit, s, 0.0), axis=-1, keepdims=True))
        cur = jnp.where(hit, neg, cur)
        chosen = jnp.where(hit, 1.0, chosen)
        hits.append(hit)
        idxs.append(idx)
    onehot = chosen
    r = lax.broadcasted_iota(I32, (tm, tm), 0)
    c = lax.broadcasted_iota(I32, (tm, tm), 1)
    earlier = jnp.where(r > c, 1.0, 0.0).astype(BF16)
    before = _dot(earlier, onehot.astype(BF16)) + run[...]
    run[...] = run[...] + jnp.sum(onehot, axis=0, keepdims=True)

    wsum = ws[0]
    for w in ws[1:]:
        wsum = wsum + w
    out_lane = lax.broadcasted_iota(I32, (tm, LANES), 1)
    eidx = jnp.zeros((tm, LANES), F32)
    wts = jnp.zeros((tm, LANES), F32)
    rank = jnp.zeros((tm, LANES), F32)
    for k in range(TOP_K):
        rk = jnp.sum(jnp.where(hits[k], before, 0.0), axis=-1, keepdims=True)
        eidx = jnp.where(out_lane == k, idxs[k], eidx)
        wts = jnp.where(out_lane == k, ws[k] / wsum * ROUTED_SCALE, wts)
        rank = jnp.where(out_lane == k, rk, rank)
    eidx_ref[...] = eidx.astype(I32)
    wts_ref[...] = wts
    rank_ref[...] = rank.astype(I32)

    @pl.when(i == pl.num_programs(0) - 1)
    def _():
        cnt_ref[...] = run[...]


def _token_specs(n_p, tm, d_block, index_rest):
    np_tiles = n_p // tm
    return (pl.BlockSpec((tm,) + d_block, lambda i, *_: (jnp.minimum(i, np_tiles - 1),) + index_rest),
            pl.BlockSpec((tm,) + d_block, lambda i, *_: (jnp.maximum(i - np_tiles, 0),) + index_rest))


def _router(h, router_w_b, router_bias, counts0):
    n, d = h.shape
    tm = next(t for t in (512, 256, 128, 64, 32, 16, 8) if n % t == 0)
    tok = pl.BlockSpec((tm, LANES), lambda i: (i, 0))
    cnt = pl.BlockSpec((1, N_EXPERTS), lambda i: (0, 0))
    return pl.pallas_call(
        functools.partial(_router_kernel, tm=tm), grid=(n // tm,),
        in_specs=[pl.BlockSpec((tm, d), lambda i: (i, 0)),
                  pl.BlockSpec((d, N_EXPERTS), lambda i: (0, 0)), cnt, cnt],
        out_specs=[tok, tok, tok, cnt],
        out_shape=[jax.ShapeDtypeStruct((n, LANES), I32),
                   jax.ShapeDtypeStruct((n, LANES), F32),
                   jax.ShapeDtypeStruct((n, LANES), I32),
                   jax.ShapeDtypeStruct((1, N_EXPERTS), F32)],
        scratch_shapes=[pltpu.VMEM((1, N_EXPERTS), F32)],
        compiler_params=_cparams(1), name="router")(h, router_w_b, router_bias, counts0)


SUBLANES = 8


def _gather_rows(idx_ref, src_hbm, dst, sem):
    def body(g, carry):
        for j in range(SUBLANES):
            tok = idx_ref[0, 0, g * SUBLANES + j]
            pltpu.make_async_copy(src_hbm.at[pl.ds(tok, 1)], dst.at[g, pl.ds(j, 1)], sem).start(priority=j % 2)
        return carry
    lax.fori_loop(0, dst.shape[0], body, 0)


def _wait_rows(dst, sem):
    pltpu.make_async_copy(dst, dst, sem).wait()


def _dest_kernel(eidx_ref, rank_ref, ps_ref, dest_ref, *, tm):
    lane = lax.broadcasted_iota(I32, (tm, N_EXPERTS), 1)
    out_lane = lax.broadcasted_iota(I32, (tm, LANES), 1)
    eidx = eidx_ref[...]
    rank = rank_ref[...].astype(F32)
    cols = [eidx[:, k:k + 1] for k in range(TOP_K)]
    picked = [jnp.where(lane == c, ps_ref[...], 0.0) for c in cols]
    starts = [jnp.sum(p, axis=-1, keepdims=True) for p in picked]
    dest = jnp.zeros((tm, LANES), F32)
    for k in range(TOP_K):
        dest = jnp.where(out_lane == k, starts[k] + rank[:, k:k + 1], dest)
    dest_ref[...] = dest.astype(I32)


def _dest(eidx, rank, pstarts, tm):
    n = eidx.shape[0]
    tok = pl.BlockSpec((tm, LANES), lambda i: (i, 0))
    return pl.pallas_call(
        functools.partial(_dest_kernel, tm=tm), grid=(n // tm,),
        in_specs=[tok, tok, pl.BlockSpec((1, N_EXPERTS), lambda i: (0, 0))],
        out_specs=tok, out_shape=jax.ShapeDtypeStruct((n, LANES), I32),
        compiler_params=_cparams(1), name="dest")(eidx, rank, pstarts)


ID_LANES = LANES


def _dispatch_kernel(clear_ref, dest_ref, hp_ref, hs_ref, xs_hbm, pbuf, zbuf, sems, zsem,
                     *, tm, nb, np_tiles, plane, n_slots):
    i = pl.program_id(0)
    n = pl.num_programs(0)
    slot = i % 2
    d = hp_ref.shape[-1]
    groups = tm // SUBLANES

    def zero_copy(blk):
        return pltpu.make_async_copy(zbuf.at[blk % 2], xs_hbm.at[pl.ds(blk * ROW_BLOCK, ROW_BLOCK)], zsem)

    def wait_slot(s):
        for k in range(TOP_K):
            _wait_rows(pbuf.at[s, k], sems.at[s])

    @pl.when(i == 0)
    def _():
        pad_row = lax.broadcasted_iota(I32, (ROW_BLOCK, ID_LANES), 0)
        for parity in range(2):
            zbuf[parity, :, :d] = jnp.zeros((ROW_BLOCK, d), F32)
            zbuf[parity, :, d:] = (n_slots + parity * ROW_BLOCK + pad_row).astype(F32)

        def start(blk, carry):
            @pl.when(clear_ref[blk] > 0)
            def _():
                zero_copy(blk).start()
            return carry

        def wait(blk, carry):
            @pl.when(clear_ref[blk] > 0)
            def _():
                zero_copy(blk).wait()
            return carry
        lax.fori_loop(0, nb, start, 0)
        lax.fori_loop(0, nb, wait, 0)

    @pl.when(i >= 2)
    def _():
        wait_slot(slot)

    h = jnp.where(i < np_tiles, hp_ref[...], hs_ref[...])
    tok = (i * tm + lax.broadcasted_iota(I32, (groups, SUBLANES, ID_LANES), 0) * SUBLANES
           + lax.broadcasted_iota(I32, (groups, SUBLANES, ID_LANES), 1))
    for k in range(TOP_K):
        pbuf[slot, k, :, :, :d] = h
        pbuf[slot, k, :, :, d:] = (k * plane + tok).astype(F32)

    def body(g, carry):
        for j in range(SUBLANES):
            for k in range(TOP_K):
                row = dest_ref[0, 0, (g * SUBLANES + j) * TOP_K + k]
                pltpu.make_async_copy(pbuf.at[slot, k, g, pl.ds(j, 1)], xs_hbm.at[pl.ds(row, 1)],
                                      sems.at[slot]).start(priority=k % 2)
        return carry
    lax.fori_loop(0, groups, body, 0)

    @pl.when(i == n - 1)
    def _():
        wait_slot(slot)

    @pl.when(jnp.logical_and(i == n - 1, n > 1))
    def _():
        wait_slot(1 - slot)


def _dispatch(clear, dest3, h_p, h_s, tm, plane):
    (n_p, d), n_s = h_p.shape, h_s.shape[0]
    nb = clear.shape[0]
    groups = tm // SUBLANES
    dx = d + ID_LANES
    rows3 = lambda h: h.reshape(h.shape[0] // SUBLANES, SUBLANES, d)
    grid_spec = pltpu.PrefetchScalarGridSpec(
        num_scalar_prefetch=1, grid=((n_p + n_s) // tm,),
        in_specs=[pl.BlockSpec((1, 1, tm * TOP_K), lambda i, cl: (i, 0, 0), memory_space=pltpu.SMEM),
                  *_token_specs(n_p // SUBLANES, groups, (SUBLANES, d), (0, 0))],
        out_specs=pl.BlockSpec(memory_space=pl.ANY),
        scratch_shapes=[pltpu.VMEM((2, TOP_K, groups, SUBLANES, dx), F32), pltpu.VMEM((2, ROW_BLOCK, dx), F32),
                        pltpu.SemaphoreType.DMA((2,)), pltpu.SemaphoreType.DMA(())])
    kern = functools.partial(_dispatch_kernel, tm=tm, nb=nb, np_tiles=n_p // tm, plane=plane,
                             n_slots=TOP_K * plane)
    return pl.pallas_call(
        kern, grid_spec=grid_spec,
        out_shape=jax.ShapeDtypeStruct((nb * ROW_BLOCK, dx), F32),
        compiler_params=_cparams(1), name="dispatch")(clear, dest3, rows3(h_p), rows3(h_s))


def _experts_kernel(be_ref, nact_ref, x_ref, g_hbm, u_hbm, d_hbm, ys_hbm, gf, uf, df, gb, ub, db, sems,
                    slot_ref, ybuf, ysems, idbuf, ids, isems, zrows, *, base, nb):
    i = pl.program_id(0)
    nact = nact_ref[0]
    d = gf.shape[1]
    groups = ROW_BLOCK // SUBLANES

    def weight_copies(e, s):
        return (pltpu.make_async_copy(g_hbm.at[base + e], gf.at[s], sems.at[s]),
                pltpu.make_async_copy(u_hbm.at[base + e], uf.at[s], sems.at[s]),
                pltpu.make_async_copy(d_hbm.at[base + e], df.at[s], sems.at[s]))

    @pl.when(jnp.logical_and(i == 0, nact > 0))
    def _():
        slot_ref[0] = 0
        for c in weight_copies(be_ref[0], 0):
            c.start()
        zrows[...] = jnp.zeros_like(zrows)
        n_slots = ys_hbm.shape[0] - 2 * ROW_BLOCK
        for parity in range(2):
            dump = pltpu.make_async_copy(zrows, ys_hbm.at[pl.ds(n_slots + parity * ROW_BLOCK, ROW_BLOCK)],
                                         ysems.at[0])
            dump.start()
            dump.wait()

    first = jnp.logical_or(i == 0, be_ref[i] != be_ref[jnp.maximum(i - 1, 0)])

    @pl.when(jnp.logical_and(i < nact, first))
    def _():
        s = slot_ref[0]
        here = be_ref[i]
        nxt = lax.while_loop(lambda j: jnp.logical_and(j < nact, be_ref[jnp.minimum(j, nb - 1)] == here),
                             lambda j: j + 1, i + 1)

        @pl.when(nxt < nact)
        def _():
            for c in weight_copies(be_ref[jnp.minimum(nxt, nb - 1)], 1 - s):
                c.start()
        for c in weight_copies(here, s):
            c.wait()
        gb[...] = gf[s].astype(BF16)
        ub[...] = uf[s].astype(BF16)
        db[...] = df[s].astype(BF16)
        slot_ref[0] = 1 - s

    ys = i % 2

    def id_copy(s):
        return pltpu.make_async_copy(idbuf.at[s], ids.at[s], isems.at[s])

    def send_rows(s):
        id_copy(s).wait()
        ids_s, ybuf_s, sem_s = ids.at[s], ybuf.at[s], ysems.at[s]

        def body(g, carry):
            for j in range(SUBLANES):
                dst = ids_s[g, j]
                pltpu.make_async_copy(ybuf_s.at[g, pl.ds(j, 1)], ys_hbm.at[pl.ds(dst, 1)],
                                      sem_s).start(priority=j % 2)
            return carry
        lax.fori_loop(0, groups, body, 0)

    def wait_rows(s):
        _wait_rows(ybuf.at[s], ysems.at[s])

    @pl.when(jnp.logical_and(i >= 1, i < nact))
    def _():
        send_rows(1 - ys)

    @pl.when(jnp.logical_and(i >= 2, i < nact))
    def _():
        wait_rows(ys)

    @pl.when(i < nact)
    def _():
        x = x_ref[:, :d].astype(BF16)
        hid = _silu(_dot(x, gb[...])) * _dot(x, ub[...])
        y = _dot(hid.astype(BF16), db[...])
        ybuf[ys] = y.reshape(groups, SUBLANES, d)
        id_lanes = x_ref[:, d:].reshape(groups, SUBLANES, ID_LANES)
        diag = (lax.broadcasted_iota(I32, (1, SUBLANES, ID_LANES), 1)
                == lax.broadcasted_iota(I32, (1, SUBLANES, ID_LANES), 2))
        idbuf[ys] = jnp.sum(jnp.where(diag, id_lanes, 0.0), axis=1).astype(I32)
        id_copy(ys).start()

    @pl.when(i == nact - 1)
    def _():
        send_rows(ys)

    @pl.when(jnp.logical_and(i == nact - 1, nact >= 2))
    def _():
        wait_rows(1 - ys)

    @pl.when(i == nact - 1)
    def _():
        wait_rows(ys)


def _experts(block_e, nact, x_sorted, exp_gate, exp_up, exp_down, layer, n_slot_rows):
    nb, dx = x_sorted.shape[0] // ROW_BLOCK, x_sorted.shape[1]
    _, d, de = exp_gate.shape
    base = layer * N_EXPERTS
    grid_spec = pltpu.PrefetchScalarGridSpec(
        num_scalar_prefetch=2, grid=(nb,),
        in_specs=[pl.BlockSpec((ROW_BLOCK, dx), lambda i, be, na: (jnp.minimum(i, jnp.maximum(na[0] - 1, 0)), 0)),
                  pl.BlockSpec(memory_space=pl.ANY), pl.BlockSpec(memory_space=pl.ANY),
                  pl.BlockSpec(memory_space=pl.ANY)],
        out_specs=pl.BlockSpec(memory_space=pl.ANY),
        scratch_shapes=[pltpu.VMEM((2, d, de), F32), pltpu.VMEM((2, d, de), F32), pltpu.VMEM((2, de, d), F32),
                        pltpu.VMEM((d, de), BF16), pltpu.VMEM((d, de), BF16), pltpu.VMEM((de, d), BF16),
                        pltpu.SemaphoreType.DMA((2,)), pltpu.SMEM((1,), I32),
                        pltpu.VMEM((2, ROW_BLOCK // SUBLANES, SUBLANES, d), F32), pltpu.SemaphoreType.DMA((2,)),
                        pltpu.VMEM((2, ROW_BLOCK // SUBLANES, ID_LANES), I32),
                        pltpu.SMEM((2, ROW_BLOCK // SUBLANES, ID_LANES), I32),
                        pltpu.SemaphoreType.DMA((2,)), pltpu.VMEM((ROW_BLOCK, d), F32)])
    return pl.pallas_call(
        functools.partial(_experts_kernel, base=base, nb=nb), grid_spec=grid_spec,
        out_shape=jax.ShapeDtypeStruct((n_slot_rows, d), F32),
        compiler_params=_cparams(1), name="experts")(block_e, nact, x_sorted, exp_gate, exp_up, exp_down)


def _combine_kernel(*refs, tm):
    y_refs = refs[:TOP_K]
    wts_ref, x1_ref, h_ref, gt_ref, sg_ref, su_ref, sd_ref, nf_ref, x2_ref, yn_ref = refs[TOP_K:]
    hb = h_ref[...].astype(BF16)
    hid = _silu(_dot(hb, sg_ref[...])) * _dot(hb, su_ref[...])
    acc = _dot(hid.astype(BF16), sd_ref[...])
    wts = wts_ref[...]
    for k in range(TOP_K):
        acc = acc + y_refs[k][...] * wts[:, k:k + 1]
    x2 = x1_ref[...] + gt_ref[0] * acc
    x2_ref[...] = x2
    yn_ref[...] = _rms(x2, nf_ref[...])


def _combine(y_slots, plane, first_tok, wts, x1, h2, mod3, sh_gate_b, sh_up_b, sh_down_b, norm_final, tm,
             tiles_per_batch):
    n, d = x1.shape
    rows = mod3.shape[1]
    ds = sh_gate_b.shape[1]
    tok = lambda w: pl.BlockSpec((tm, w), lambda i: (i, 0))
    assert plane % tm == 0 and first_tok % tm == 0
    slot_specs = [pl.BlockSpec((tm, d), functools.partial(lambda i, off: (off + i, 0), off=(k * plane + first_tok) // tm))
                  for k in range(TOP_K)]
    return pl.pallas_call(
        functools.partial(_combine_kernel, tm=tm), grid=(n // tm,),
        in_specs=[*slot_specs, tok(LANES), tok(d), tok(d),
                  _mod_spec(rows, d, tiles_per_batch, 5),
                  pl.BlockSpec((d, ds), lambda i: (0, 0)),
                  pl.BlockSpec((d, ds), lambda i: (0, 0)),
                  pl.BlockSpec((ds, d), lambda i: (0, 0)),
                  pl.BlockSpec((1, d), lambda i: (0, 0))],
        out_specs=[tok(d), tok(d)],
        out_shape=[jax.ShapeDtypeStruct((n, d), F32)] * 2,
        compiler_params=_cparams(1), name="combine")(
            *([y_slots] * TOP_K), wts, x1, h2, mod3, sh_gate_b, sh_up_b, sh_down_b, norm_final)


def _last_rows(buf, new, keep):
    t = new.shape[1]
    if t >= keep:
        return new[:, t - keep:]
    return jnp.concatenate([buf[:, t:], new], axis=1)


def _mixers(u, qkv, z, ba, b, t, pool_buf, conv_buf, s0, start_pos, lw):
    t_pool = -(-t // 8) * 8
    t_gdn = -(-t // CHUNK) * CHUNK
    u3 = u.reshape(b, t, POOL_W)
    qkv3 = qkv.reshape(b, t, CONV_CH)
    pad_t = lambda a, tp: a if tp == t else jnp.pad(a, ((0, 0), (0, tp - t), (0, 0)))
    buf16 = jnp.pad(pool_buf, ((0, 0), (POOL_HALO - POOL_BUF, 0), (0, 0)))
    pool_out = _pool(pad_t(u3, t_pool), buf16, lw["pool_w"], lw["pool_scale"], start_pos)[:, :t]
    if t == 1:
        dn_out, s_new = _gdn_step(qkv, z, ba, conv_buf, s0, lw["conv_w"], lw["par"], lw["dn_norm"])
    else:
        cbuf8 = jnp.pad(conv_buf, ((0, 0), (CONV_HALO - (CONV_W - 1), 0), (0, 0)))
        dn_out, s_new = _gdn(pad_t(qkv3, t_gdn), pad_t(z.reshape(b, t, DN_W), t_gdn),
                             pad_t(ba.reshape(b, t, BA_W), t_gdn), cbuf8, s0, lw["conv_w"], lw["par"],
                             lw["dn_norm"], t)
        dn_out = dn_out[:, :t].reshape(b * t, DN_W)
    new_pool = _last_rows(pool_buf, u3, POOL_BUF)
    new_conv = _last_rows(conv_buf, qkv3, CONV_W - 1)
    return pool_out.reshape(b * t, POOL_W), dn_out, new_pool, new_conv, s_new


def _block_tables(counts, n_tok):
    nk = n_tok * TOP_K
    nb = -(-(nk + N_EXPERTS * (ROW_BLOCK - 1)) // ROW_BLOCK)
    pcounts = (counts + ROW_BLOCK - 1) // ROW_BLOCK * ROW_BLOCK
    pends = jnp.cumsum(pcounts)
    pstarts = pends - pcounts
    first_row = jnp.arange(nb, dtype=I32)[:, None] * ROW_BLOCK
    block_e = jnp.minimum(jnp.sum((pends[None, :] <= first_row).astype(I32), axis=1), N_EXPERTS - 1)
    nact = (pends[-1] // ROW_BLOCK).astype(I32).reshape(1)
    blk = jnp.arange(nb, dtype=I32)
    next_e = jnp.concatenate([block_e[1:], jnp.full((1,), -1, I32)])
    clear = ((blk >= nact[0] - 1) | (next_e != block_e)).astype(I32)
    return pstarts.astype(F32).reshape(1, N_EXPERTS), block_e, nact, clear


def _dest_tiles(dest, tm, k_major):
    n = dest.shape[0]
    tiles = dest.reshape(n // tm, tm, TOP_K)
    if k_major:
        tiles = tiles.transpose(0, 2, 1)
    return tiles.reshape(n // tm, 1, tm * TOP_K)


def kernel(x_prompt, x_sample, state_pool, state_conv, state_delta, c_prompt, c_sample, norm_mix, norm_ffn, w_ada, b_ada, w_in, pool_w, pool_scale, conv_w, a_log, dt_bias, dn_norm, w_out, router_w, router_bias, exp_gate, exp_up, exp_down, sh_gate, sh_up, sh_down, norm_final):
    bp, tp, d = x_prompt.shape
    bs, ts, _ = x_sample.shape
    depth = w_ada.shape[0]
    past_len = PAST_LEN
    n_p, n_s = bp * tp, bs * ts
    n_tok = n_p + n_s
    tm_p = min(512, tp)
    tm_c = min(128, tp)
    tm_r = next(t for t in (128, 64, 32, 16, 8) if n_p % t == 0 and n_s % t == 0)
    unit = tm_c * n_s // math.gcd(tm_c, n_s)
    plane = -(-n_tok // unit) * unit
    assert tp % tm_p == 0 and ts == 1 and bs % 8 == 0
    n_exp = exp_gate.shape[1]
    gate_w = exp_gate.reshape((depth * n_exp,) + exp_gate.shape[2:])
    up_w = exp_up.reshape((depth * n_exp,) + exp_up.shape[2:])
    down_w = exp_down.reshape((depth * n_exp,) + exp_down.shape[2:])

    c_all = jnp.concatenate([c_prompt, c_sample], axis=0)
    m_rows = -(-c_all.shape[0] // 8) * 8
    mod_all = _ada(jnp.pad(c_all, ((0, m_rows - c_all.shape[0]), (0, 0))), w_ada, b_ada)

    c2 = POOL_W + CONV_CH + DN_W
    lane_pad = jnp.zeros((depth, d, LANES - DN_HEADS), F32)
    w_in_p = jnp.concatenate([w_in[:, :, :c2], w_in[:, :, c2:c2 + DN_HEADS], lane_pad,
                              w_in[:, :, c2 + DN_HEADS:], lane_pad], axis=-1).astype(BF16)
    head_pad = lambda a: jnp.pad(a, ((0, 0), (0, LANES - DN_HEADS)))
    par_all = jnp.stack([head_pad(a_log), head_pad(dt_bias)] + [jnp.zeros((depth, LANES), F32)] * 6, axis=1)
    conv_w8 = jnp.pad(conv_w, ((0, 0), (0, 8 - CONV_W), (0, 0)))

    xp = x_prompt.reshape(n_p, d)
    xs = x_sample.reshape(n_s, d)
    zero_pool = jnp.zeros((bp, POOL_BUF, POOL_W), F32)
    zero_conv = jnp.zeros((bp, CONV_W - 1, CONV_CH), F32)
    zero_state = jnp.zeros((bp, DN_HEADS, DN_D, DN_D), F32)
    outs = {k: [] for k in ("pool_p", "conv_p", "delta_p", "pool_s", "conv_s", "delta_s")}
    yp = ys = None
    for l in range(depth):
        lw = {"pool_w": pool_w[l].astype(BF16), "pool_scale": pool_scale[l].reshape(1, POOL_W),
              "conv_w": conv_w8[l], "par": par_all[l], "dn_norm": dn_norm[l].reshape(1, DN_D)}
        mod_p = mod_all[l, :bp].reshape(bp, 1, 6 * d)
        mod_s = mod_all[l, bp:bp + bs].reshape(1, bs, 6 * d)
        nw_mix = norm_mix[l].reshape(1, d)
        nw_ffn = norm_ffn[l].reshape(1, d)
        w_out_b = w_out[l].astype(BF16)

        up, qkvp, zp, bap = _norm_proj(xp, mod_p, nw_mix, w_in_p[l], tm_p, tp // tm_p)
        us, qkvs, zs, bas = _norm_proj(xs, mod_s, nw_mix, w_in_p[l], n_s, 1)
        pool_p, dn_p, npool_p, nconv_p, ns_p = _mixers(up, qkvp, zp, bap, bp, tp, zero_pool, zero_conv,
                                                       zero_state, 0, lw)
        pool_s, dn_s, npool_s, nconv_s, ns_s = _mixers(us, qkvs, zs, bas, bs, ts, state_pool[l], state_conv[l],
                                                       state_delta[l], past_len, lw)
        for key, val in (("pool_p", npool_p), ("conv_p", nconv_p), ("delta_p", ns_p),
                         ("pool_s", npool_s), ("conv_s", nconv_s), ("delta_s", ns_s)):
            outs[key].append(val)

        x1p, h2p = _out_proj(pool_p, dn_p, xp, mod_p, nw_ffn, w_out_b, tm_p, tp // tm_p)
        x1s, h2s = _out_proj(pool_s, dn_s, xs, mod_s, nw_ffn, w_out_b, n_s, 1)

        rw_b, rb = router_w[l].astype(BF16), router_bias[l].reshape(1, N_EXPERTS)
        eidx_p, wts_p, rank_p, counts = _router(h2p, rw_b, rb, jnp.zeros((1, N_EXPERTS), F32))
        eidx_s, wts_s, rank_s, counts = _router(h2s, rw_b, rb, counts)
        eidx, rank = jnp.concatenate([eidx_p, eidx_s]), jnp.concatenate([rank_p, rank_s])
        pstarts, block_e, nact, clear = _block_tables(counts.reshape(-1).astype(I32), n_tok)
        dest = _dest(eidx, rank, pstarts, tm_r)[:, :TOP_K]
        x_sorted = _dispatch(clear, _dest_tiles(dest, tm_r, False), h2p, h2s, tm_r, plane)
        y_slots = _experts(block_e, nact, x_sorted, gate_w, up_w, down_w, l, TOP_K * plane + 2 * ROW_BLOCK)

        shg, shu, shd = sh_gate[l].astype(BF16), sh_up[l].astype(BF16), sh_down[l].astype(BF16)
        nf = norm_final.reshape(1, d)
        xp, yp = _combine(y_slots, plane, 0, wts_p, x1p, h2p, mod_p, shg, shu, shd, nf, tm_c, tp // tm_c)
        xs, ys = _combine(y_slots, plane, n_p, wts_s, x1s, h2s, mod_s, shg, shu, shd, nf, n_s, 1)

    stack = lambda key: jnp.stack(outs[key])
    return (yp.reshape(bp, tp, d), ys.reshape(bs, ts, d), stack("pool_p"), stack("conv_p"), stack("delta_p"),
            stack("pool_s"), stack("conv_s"), stack("delta_s"))
```

```python
import functools

import jax
import jax.numpy as jnp
from jax import lax
from jax.experimental import pallas as pl
from jax.experimental.pallas import tpu as pltpu

F32, BF16, I32 = jnp.float32, jnp.bfloat16, jnp.int32
HI = lax.Precision.HIGHEST
EPS = 1e-6

LANES = 128
POOL_WINDOWS = (2, 4, 8, 16)
POOL_GW = 128
POOL_W = 512
POOL_BUF = 15
POOL_HALO = 16
POOL_TILE = 512
DN_HEADS = 4
DN_D = 128
DN_W = 512
CONV_W = 4
CONV_CH = 3 * DN_W
CONV_HALO = 8
CHUNK = 64
INV_BLOCK = 16
GDN_ROWS = 4
N_EXPERTS = 256
TOP_K = 8
N_GROUPS = 8
TOPK_GROUPS = 4
GROUP_SHIFT = 5
ROUTED_SCALE = 2.5
PAST_LEN = 16384
ROW_BLOCK = 256
BA_W = 2 * LANES
D_IN_PAD = POOL_W + CONV_CH + DN_W + BA_W
VMEM_LIMIT = 48 * 2 ** 20


def _cparams(n_axes):
    return pltpu.CompilerParams(dimension_semantics=("arbitrary",) * n_axes,
                                vmem_limit_bytes=VMEM_LIMIT)


def _sigmoid(x):
    return 1.0 / (1.0 + jnp.exp(-x))


def _silu(x):
    return x * _sigmoid(x)


def _softplus(x):
    return jnp.maximum(x, 0.0) + jnp.log(1.0 + jnp.exp(-jnp.abs(x)))


def _rms(x, w):
    return x * lax.rsqrt(jnp.mean(x * x, axis=-1, keepdims=True) + EPS) * w


def _dot(a, b, precision=None):
    return jnp.dot(a, b, preferred_element_type=F32, precision=precision)


def _dot_nt(a, b, precision=None):
    return lax.dot_general(a, b, (((1,), (1,)), ((), ())), preferred_element_type=F32,
                           precision=precision)


def _ada_kernel(c_ref, w_ref, b_ref, o_ref):
    o_ref[0] = _dot(_silu(c_ref[...]), w_ref[0], HI) + b_ref[0]


def _ada(c_all, w_ada, b_ada):
    depth, d, d6 = w_ada.shape
    m = c_all.shape[0]
    tn = 1024
    return pl.pallas_call(
        _ada_kernel, grid=(depth, d6 // tn),
        in_specs=[pl.BlockSpec((m, d), lambda l, j: (0, 0)),
                  pl.BlockSpec((1, d, tn), lambda l, j: (l, 0, j)),
                  pl.BlockSpec((1, 1, tn), lambda l, j: (l, 0, j))],
        out_specs=pl.BlockSpec((1, m, tn), lambda l, j: (l, 0, j)),
        out_shape=jax.ShapeDtypeStruct((depth, m, d6), F32),
        compiler_params=_cparams(2), name="ada")(c_all, w_ada, b_ada.reshape(depth, 1, d6))


def _mod_spec(rows, d, tiles_per_batch, chunk):
    return pl.BlockSpec((1, rows, d), lambda i: (i // tiles_per_batch, 0, chunk))


def _norm_proj_kernel(x_ref, sh_ref, sc_ref, nw_ref, w_ref, u_ref, qkv_ref, z_ref, ba_ref):
    h = _rms(x_ref[...], nw_ref[...]) * (1.0 + sc_ref[0]) + sh_ref[0]
    hb = h.astype(BF16)
    c0, c1, c2 = POOL_W, POOL_W + CONV_CH, POOL_W + CONV_CH + DN_W
    u_ref[...] = _dot(hb, w_ref[:, :c0])
    qkv_ref[...] = _dot(hb, w_ref[:, c0:c1])
    z_ref[...] = _dot(hb, w_ref[:, c1:c2])
    ba_ref[...] = _dot(hb, w_ref[:, c2:])


def _norm_proj(x2d, mod3, nw, w_in_p, tm, tiles_per_batch):
    n, d = x2d.shape
    rows = mod3.shape[1]
    widths = (POOL_W, CONV_CH, DN_W, BA_W)
    return pl.pallas_call(
        _norm_proj_kernel, grid=(n // tm,),
        in_specs=[pl.BlockSpec((tm, d), lambda i: (i, 0)),
                  _mod_spec(rows, d, tiles_per_batch, 0),
                  _mod_spec(rows, d, tiles_per_batch, 1),
                  pl.BlockSpec((1, d), lambda i: (0, 0)),
                  pl.BlockSpec((d, D_IN_PAD), lambda i: (0, 0))],
        out_specs=[pl.BlockSpec((tm, w), lambda i: (i, 0)) for w in widths],
        out_shape=[jax.ShapeDtypeStruct((n, w), F32) for w in widths],
        compiler_params=_cparams(1), name="norm_proj")(x2d, mod3, mod3, nw, w_in_p)


def _pool_kernel(u_ref, buf_ref, pw_ref, ps_ref, o_ref, ext, *, tt, gb, start_pos):
    t = pl.program_id(1)

    @pl.when(t == 0)
    def _():
        ext[:, 0:POOL_HALO, :] = buf_ref[...]

    @pl.when(t > 0)
    def _():
        ext[:, 0:POOL_HALO, :] = ext[:, tt:tt + POOL_HALO, :]

    ext[:, POOL_HALO:POOL_HALO + tt, :] = u_ref[...]
    pos = start_pos + t * tt + lax.broadcasted_iota(I32, (tt, 1), 0)
    for g, win in enumerate(POOL_WINDOWS):
        sl = slice(g * POOL_GW, (g + 1) * POOL_GW)
        cnt = jnp.minimum(pos + 1, win).astype(F32)
        pooled = []
        for rb in range(gb):
            u = u_ref[rb, :, sl]
            s = u
            for j in range(1, win):
                s = s + ext[rb, POOL_HALO - j:POOL_HALO - j + tt, sl]
            pooled.append(s / cnt - u)
        mixed = _dot(jnp.concatenate(pooled, axis=0).astype(BF16), pw_ref[g]) * ps_ref[:, sl]
        for rb in range(gb):
            o_ref[rb, :, sl] = mixed[rb * tt:(rb + 1) * tt]


def _pool(u3, buf16, pool_w_b, pool_scale, start_pos):
    b, t, w = u3.shape
    tt = min(t, POOL_TILE)
    assert t % tt == 0 and tt % 8 == 0
    gb = next(g for g in (16, 8, 4, 2, 1) if b % g == 0 and g * tt <= POOL_TILE)
    return pl.pallas_call(
        functools.partial(_pool_kernel, tt=tt, gb=gb, start_pos=start_pos), grid=(b // gb, t // tt),
        in_specs=[pl.BlockSpec((gb, tt, w), lambda i, j: (i, j, 0)),
                  pl.BlockSpec((gb, POOL_HALO, w), lambda i, j: (i, 0, 0)),
                  pl.BlockSpec((len(POOL_WINDOWS), POOL_GW, POOL_GW), lambda i, j: (0, 0, 0)),
                  pl.BlockSpec((1, w), lambda i, j: (0, 0))],
        out_specs=pl.BlockSpec((gb, tt, w), lambda i, j: (i, j, 0)),
        out_shape=jax.ShapeDtypeStruct((b, t, w), F32),
        scratch_shapes=[pltpu.VMEM((gb, POOL_HALO + tt, w), F32)],
        compiler_params=_cparams(2), name="pool")(u3, buf16, pool_w_b, pool_scale)


def _bf(x):
    return x.astype(BF16)


def _dot3(a, b):
    a_hi, b_hi = _bf(a), _bf(b)
    a_lo, b_lo = _bf(a - a_hi.astype(F32)), _bf(b - b_hi.astype(F32))
    return _dot(a_hi, b_hi) + (_dot(a_hi, b_lo) + _dot(a_lo, b_hi))


def _each(fn, *lists):
    return [fn(*args) for args in zip(*lists)]


def _unit_lower_inverse(a, row, col, eye):
    same = (row // INV_BLOCK) == (col // INV_BLOCK)
    dm = _each(lambda x: jnp.where(same, -x, 0.0), a)
    lm = _each(lambda x, d: -x - d, a, dm)
    mm = lambda x, y: _dot(_bf(x), _bf(y))
    d2 = _each(mm, dm, dm)
    d4 = _each(mm, d2, d2)
    d8 = _each(mm, d4, d4)
    p = _each(lambda x, y: mm(eye + x, eye + y), dm, d2)
    p2 = _each(lambda x, y: mm(eye + x, eye + y), d4, d8)
    p = _each(mm, p, p2)
    m = _each(mm, p, lm)
    m2 = _each(mm, m, m)
    q = _each(lambda x, y: mm(eye + x, eye + y), m, m2)
    t0 = _each(mm, q, p)
    at = _each(_dot3, a, t0)
    resid = _each(lambda t, x: (eye - t) - x, t0, at)
    return _each(lambda t, r: t + mm(t, r), t0, resid)


def _gdn_kernel(qkv_ref, z_ref, ba_ref, cbuf_ref, s0_ref, cw_ref, par_ref, nw_ref,
                o_ref, sfin_ref, state, xext, *, t_len, gb):
    c = pl.program_id(1)
    seqs = range(gb)
    heads = range(DN_HEADS)
    hc = DN_HEADS * CHUNK

    @pl.when(c == 0)
    def _():
        for rb in seqs:
            state[rb * DN_HEADS:(rb + 1) * DN_HEADS] = s0_ref[rb]
            xext[rb, 0:CONV_HALO, :] = cbuf_ref[rb]

    @pl.when(c > 0)
    def _():
        for rb in seqs:
            xext[rb, 0:CONV_HALO, :] = xext[rb, CHUNK:CHUNK + CONV_HALO, :]

    valid = (c * CHUNK + lax.broadcasted_iota(I32, (CHUNK, 1), 0)) < t_len
    vf = valid.astype(F32)
    head_lane = lax.broadcasted_iota(I32, (1, LANES), 1) < DN_HEADS
    decay = jnp.where(head_lane, -jnp.exp(par_ref[0:1, :]), 0.0)
    r64 = lax.broadcasted_iota(I32, (CHUNK, CHUNK), 0)
    c64 = lax.broadcasted_iota(I32, (CHUNK, CHUNK), 1)
    tri = (r64 >= c64).astype(F32)
    pick = (lax.broadcasted_iota(I32, (8, LANES), 0) == lax.broadcasted_iota(I32, (8, LANES), 1)).astype(F32)
    row = lax.broadcasted_iota(I32, (hc, hc), 0)
    col = lax.broadcasted_iota(I32, (hc, hc), 1)
    incl = ((row // CHUNK) == (col // CHUNK)) & (row >= col)
    strict = row > col
    eye = (row == col).astype(F32)
    base = CONV_HALO - (CONV_W - 1)

    def conv(rb):
        xext[rb, CONV_HALO:CONV_HALO + CHUNK, :] = qkv_ref[rb]
        acc = xext[rb, base:base + CHUNK, :] * cw_ref[0:1, :]
        for j in range(1, CONV_W):
            acc = acc + xext[rb, base + j:base + j + CHUNK, :] * cw_ref[j:j + 1, :]
        return _silu(acc)
    qkvc = [conv(rb) for rb in seqs]
    ba = [ba_ref[rb] for rb in seqs]
    beta_all = _each(lambda x: _sigmoid(x[:, :LANES]) * vf, ba)
    g_all = _each(lambda x: decay * _softplus(x[:, LANES:] + par_ref[1:2, :]) * vf, ba)
    gam_all = _each(lambda g: _dot(tri, g, HI), g_all)
    gam_rows = _each(lambda g: _dot_nt(pick, g, HI), gam_all)

    def unit(x):
        return x * lax.rsqrt(jnp.sum(x * x, axis=-1, keepdims=True) + EPS)
    q_s = _each(lambda x: jnp.concatenate(
        [unit(x[:, h * DN_D:(h + 1) * DN_D]) * (DN_D ** -0.5) * vf for h in heads], axis=0), qkvc)
    k_s = _each(lambda x: jnp.concatenate(
        [unit(x[:, DN_W + h * DN_D:DN_W + (h + 1) * DN_D]) * vf for h in heads], axis=0), qkvc)
    v_s = _each(lambda x: jnp.concatenate(
        [x[:, 2 * DN_W + h * DN_D:2 * DN_W + (h + 1) * DN_D] * vf for h in heads], axis=0), qkvc)
    beta = _each(lambda x: jnp.concatenate([x[:, h:h + 1] for h in heads], axis=0), beta_all)
    gam = _each(lambda x: jnp.concatenate([x[:, h:h + 1] for h in heads], axis=0), gam_all)
    gam_r = _each(lambda x: jnp.concatenate([x[h:h + 1, :] for h in heads], axis=1), gam_rows)

    dec = _each(lambda g, gr: jnp.where(incl, jnp.exp(jnp.where(incl, g - gr, 0.0)), 0.0), gam, gam_r)
    kb = _each(_bf, k_s)
    qb = _each(_bf, q_s)
    kk = _each(_dot_nt, kb, kb)
    qk = _each(_dot_nt, qb, kb)
    a_mat = _each(lambda b_, x, d_: b_ * x * jnp.where(strict, d_, 0.0), beta, kk, dec)
    qk = _each(lambda x, d_: _bf(x * d_), qk, dec)
    egam = _each(jnp.exp, gam)
    rhs = _each(lambda b_, e, k, v: jnp.concatenate([(b_ * e) * k, b_ * v], axis=-1),
                beta, egam, k_s, v_s)
    t_inv = _unit_lower_inverse(a_mat, row, col, eye)
    sol = _each(_dot3, t_inv, rhs)
    wb = _each(lambda x: _bf(x[:, :DN_D]), sol)
    uv = _each(lambda x: x[:, DN_D:], sol)

    hrows = [slice(h * CHUNK, (h + 1) * CHUNK) for h in heads]
    s_f = [[state[rb * DN_HEADS + h] for h in heads] for rb in seqs]
    s_b = [[_bf(x) for x in per_seq] for per_seq in s_f]
    ub = _each(lambda u_, w_, s_: _bf(jnp.concatenate(
        [u_[hrows[h]] - _dot(w_[hrows[h]], s_[h]) for h in heads], axis=0)), uv, wb, s_b)
    o_intra = _each(_dot, qk, ub)
    for h in heads:
        rows = hrows[h]
        sl = slice(h * DN_D, (h + 1) * DN_D)
        o = _each(lambda e, q_, s_, oi: e[rows] * _dot(q_[rows], s_[h]) + oi[rows], egam, qb, s_b, o_intra)
        g_last = _each(lambda g: g[(h + 1) * CHUNK - 1:(h + 1) * CHUNK, :], gam)
        k_dec = _each(lambda k, gl, g: _bf((k[rows] * jnp.exp(gl - g[rows])).T), k_s, g_last, gam)
        upd = _each(lambda kd, u_: _dot(kd, u_[rows]), k_dec, ub)
        for rb in seqs:
            state[rb * DN_HEADS + h] = jnp.exp(g_last[rb]) * s_f[rb][h] + upd[rb]
            o_ref[rb, :, sl] = _rms(o[rb], nw_ref[...]) * _silu(z_ref[rb, :, sl])

    @pl.when(c == pl.num_programs(1) - 1)
    def _():
        for rb in seqs:
            sfin_ref[rb] = state[rb * DN_HEADS:(rb + 1) * DN_HEADS]


def _gdn(qkv3, z3, ba3, cbuf8, s0, conv_w8, par, dn_norm, t_len):
    b, tp, _ = qkv3.shape
    assert tp % CHUNK == 0
    gb = GDN_ROWS if b % GDN_ROWS == 0 else 1
    kern = functools.partial(_gdn_kernel, t_len=t_len, gb=gb)
    return pl.pallas_call(
        kern, grid=(b // gb, tp // CHUNK),
        in_specs=[pl.BlockSpec((gb, CHUNK, CONV_CH), lambda i, j: (i, j, 0)),
                  pl.BlockSpec((gb, CHUNK, DN_W), lambda i, j: (i, j, 0)),
                  pl.BlockSpec((gb, CHUNK, BA_W), lambda i, j: (i, j, 0)),
                  pl.BlockSpec((gb, CONV_HALO, CONV_CH), lambda i, j: (i, 0, 0)),
                  pl.BlockSpec((gb, DN_HEADS, DN_D, DN_D), lambda i, j: (i, 0, 0, 0)),
                  pl.BlockSpec((8, CONV_CH), lambda i, j: (0, 0)),
                  pl.BlockSpec((8, LANES), lambda i, j: (0, 0)),
                  pl.BlockSpec((1, DN_D), lambda i, j: (0, 0))],
        out_specs=[pl.BlockSpec((gb, CHUNK, DN_W), lambda i, j: (i, j, 0)),
                   pl.BlockSpec((gb, DN_HEADS, DN_D, DN_D), lambda i, j: (i, 0, 0, 0))],
        out_shape=[jax.ShapeDtypeStruct((b, tp, DN_W), F32),
                   jax.ShapeDtypeStruct((b, DN_HEADS, DN_D, DN_D), F32)],
        scratch_shapes=[pltpu.VMEM((gb * DN_HEADS, DN_D, DN_D), F32),
                        pltpu.VMEM((gb, CONV_HALO + CHUNK, CONV_CH), F32)],
        compiler_params=_cparams(2), name="gdn")(qkv3, z3, ba3, cbuf8, s0, conv_w8, par, dn_norm)


STEP_TILE = 8


def _gdn_step_kernel(qkv_ref, z_ref, ba_ref, cst_ref, s0_all_ref, cw_ref, par_ref, nw_ref, o_ref, s_ref):
    s0_ref = s0_all_ref.at[0]
    acc = qkv_ref[...] * cw_ref[CONV_W - 1:CONV_W, :]
    for j in range(CONV_W - 1):
        acc = acc + cst_ref[:, j, :] * cw_ref[j:j + 1, :]
    qkvc = _silu(acc)
    ba = ba_ref[...]
    head_lane = lax.broadcasted_iota(I32, (1, LANES), 1) < DN_HEADS
    beta_all = _sigmoid(ba[:, :LANES])
    decay = jnp.where(head_lane, -jnp.exp(par_ref[0:1, :]), 0.0)
    eg_all = jnp.exp(decay * _softplus(ba[:, LANES:] + par_ref[1:2, :]))
    pad6 = jnp.zeros((6, DN_D), F32)
    pad7 = jnp.zeros((7, DN_D), F32)
    for h in range(DN_HEADS):
        q = qkvc[:, h * DN_D:(h + 1) * DN_D]
        k = qkvc[:, DN_W + h * DN_D:DN_W + (h + 1) * DN_D]
        v = qkvc[:, 2 * DN_W + h * DN_D:2 * DN_W + (h + 1) * DN_D]
        q = q * lax.rsqrt(jnp.sum(q * q, axis=-1, keepdims=True) + EPS) * (DN_D ** -0.5)
        k = k * lax.rsqrt(jnp.sum(k * k, axis=-1, keepdims=True) + EPS)
        qk = _dot_nt(_bf(q), _bf(k))
        seqs = list(range(STEP_TILE))
        beta = [beta_all[b:b + 1, h:h + 1] for b in seqs]
        eg = [eg_all[b:b + 1, h:h + 1] for b in seqs]
        s_old = [s0_ref[b, h] for b in seqs]
        ks_qs = _each(lambda b, s_: _dot(_bf(jnp.concatenate([k[b:b + 1], q[b:b + 1], pad6], axis=0)), _bf(s_)),
                      seqs, s_old)
        u = _each(lambda b, b_, e, r: b_ * (v[b:b + 1] - e * r[0:1]), seqs, beta, eg, ks_qs)
        o_rows = _each(lambda b, e, r, u_: e * r[1:2] + qk[b:b + 1, b:b + 1] * u_, seqs, eg, ks_qs, u)
        k_col = [_bf(jnp.concatenate([k[b:b + 1], pad7], axis=0).T) for b in seqs]
        upd = _each(lambda kc, u_: _dot(kc, _bf(jnp.concatenate([u_, pad7], axis=0))), k_col, u)
        for b in seqs:
            s_ref[b, h] = eg[b] * s_old[b] + upd[b]
        o = jnp.concatenate(o_rows, axis=0)
        sl = slice(h * DN_D, (h + 1) * DN_D)
        o_ref[:, sl] = _rms(o, nw_ref[...]) * _silu(z_ref[:, sl])


def _gdn_step(qkv2, z2, ba2, conv_state, s0_all, layer, conv_w8, par, dn_norm):
    b = qkv2.shape[0]
    assert b % STEP_TILE == 0
    row = lambda w: pl.BlockSpec((STEP_TILE, w), lambda i: (i, 0))
    st = pl.BlockSpec((STEP_TILE, DN_HEADS, DN_D, DN_D), lambda i: (i, 0, 0, 0))
    st_in = pl.BlockSpec((1, STEP_TILE, DN_HEADS, DN_D, DN_D), lambda i: (layer, i, 0, 0, 0))
    return pl.pallas_call(
        _gdn_step_kernel, grid=(b // STEP_TILE,),
        in_specs=[row(CONV_CH), row(DN_W), row(BA_W),
                  pl.BlockSpec((STEP_TILE, CONV_W - 1, CONV_CH), lambda i: (i, 0, 0)), st_in,
                  pl.BlockSpec((8, CONV_CH), lambda i: (0, 0)),
                  pl.BlockSpec((8, LANES), lambda i: (0, 0)),
                  pl.BlockSpec((1, DN_D), lambda i: (0, 0))],
        out_specs=[row(DN_W), st],
        out_shape=[jax.ShapeDtypeStruct((b, DN_W), F32),
                   jax.ShapeDtypeStruct((b, DN_HEADS, DN_D, DN_D), F32)],
        compiler_params=_cparams(1), name="gdn_step")(qkv2, z2, ba2, conv_state, s0_all, conv_w8, par, dn_norm)


def _out_proj_kernel(pool_ref, dn_ref, x_ref, gt_ref, sh_ref, sc_ref, nw_ref, w_ref, x1_ref, h2_ref):
    mix = _dot(pool_ref[...].astype(BF16), w_ref[:POOL_W, :]) + _dot(dn_ref[...].astype(BF16), w_ref[POOL_W:, :])
    x1 = x_ref[...] + gt_ref[0] * mix
    x1_ref[...] = x1
    h2_ref[...] = _rms(x1, nw_ref[...]) * (1.0 + sc_ref[0]) + sh_ref[0]


def _out_proj(pool2, dn2, x2d, mod3, nw, w_out_b, tm, tiles_per_batch):
    n, d = x2d.shape
    rows = mod3.shape[1]
    tok = lambda w: pl.BlockSpec((tm, w), lambda i: (i, 0))
    return pl.pallas_call(
        _out_proj_kernel, grid=(n // tm,),
        in_specs=[tok(POOL_W), tok(DN_W), tok(d),
                  _mod_spec(rows, d, tiles_per_batch, 2),
                  _mod_spec(rows, d, tiles_per_batch, 3),
                  _mod_spec(rows, d, tiles_per_batch, 4),
                  pl.BlockSpec((1, d), lambda i: (0, 0)),
                  pl.BlockSpec((POOL_W + DN_W, d), lambda i: (0, 0))],
        out_specs=[tok(d), tok(d)],
        out_shape=[jax.ShapeDtypeStruct((n, d), F32)] * 2,
        compiler_params=_cparams(1), name="out_proj")(pool2, dn2, x2d, mod3, mod3, mod3, nw, w_out_b)


def _router_kernel(h_ref, rw_ref, rb_ref, cnt0_ref, eidx_ref, wts_ref, rank_ref, cnt_ref, run, *, tm):
    i = pl.program_id(0)

    @pl.when(i == 0)
    def _():
        run[...] = cnt0_ref[...]

    s = _sigmoid(_dot(h_ref[...].astype(BF16), rw_ref[...]))
    sel = s + rb_ref[...]
    lane_i = lax.broadcasted_iota(I32, (tm, N_EXPERTS), 1)
    lane = lane_i.astype(F32)
    grp = lane_i >> GROUP_SHIFT
    neg = -jnp.inf

    def first_max(x):
        m = jnp.max(x, axis=-1, keepdims=True)
        idx = jnp.min(jnp.where(x == m, lane, float(N_EXPERTS)), axis=-1, keepdims=True)
        return m, idx

    gs = []
    for g in range(N_GROUPS):
        xg = jnp.where(grp == g, sel, neg)
        m1, i1 = first_max(xg)
        m2 = jnp.max(jnp.where(lane == i1, neg, xg), axis=-1, keepdims=True)
        gs.append(m1 + m2)
    cur = jnp.full((tm, N_EXPERTS), neg, F32)
    for g in range(N_GROUPS):
        ahead = jnp.zeros((tm, 1), F32)
        for o in range(N_GROUPS):
            if o < g:
                ahead = ahead + jnp.where(gs[o] >= gs[g], 1.0, 0.0)
            elif o > g:
                ahead = ahead + jnp.where(gs[o] > gs[g], 1.0, 0.0)
        cur = jnp.where((grp == g) & (ahead < TOPK_GROUPS), sel, cur)

    hits, idxs, ws = [], [], []
    chosen = jnp.zeros((tm, N_EXPERTS), F32)
    for _ in range(TOP_K):
        _, idx = first_max(cur)
        hit = lane == idx
        ws.append(jnp.sum(jnp.where(hit, s, 0.0), axis=-1, keepdims=True))
        cur = jnp.where(hit, neg, cur)
        chosen = jnp.where(hit, 1.0, chosen)
        hits.append(hit)
        idxs.append(idx)
    onehot = chosen
    r = lax.broadcasted_iota(I32, (tm, tm), 0)
    c = lax.broadcasted_iota(I32, (tm, tm), 1)
    earlier = jnp.where(r > c, 1.0, 0.0).astype(BF16)
    before = _dot(earlier, onehot.astype(BF16)) + run[...]
    run[...] = run[...] + jnp.sum(onehot, axis=0, keepdims=True)

    wsum = ws[0]
    for w in ws[1:]:
        wsum = wsum + w
    out_lane = lax.broadcasted_iota(I32, (tm, LANES), 1)
    eidx = jnp.zeros((tm, LANES), F32)
    wts = jnp.zeros((tm, LANES), F32)
    rank = jnp.zeros((tm, LANES), F32)
    for k in range(TOP_K):
        rk = jnp.sum(jnp.where(hits[k], before, 0.0), axis=-1, keepdims=True)
        eidx = jnp.where(out_lane == k, idxs[k], eidx)
        wts = jnp.where(out_lane == k, ws[k] / wsum * ROUTED_SCALE, wts)
        rank = jnp.where(out_lane == k, rk, rank)
    eidx_ref[...] = eidx.astype(I32)
    wts_ref[...] = wts
    rank_ref[...] = rank.astype(I32)

    @pl.when(i == pl.num_programs(0) - 1)
    def _():
        cnt_ref[...] = run[...]


def _token_specs(n_p, tm, d_block, index_rest):
    np_tiles = n_p // tm
    return (pl.BlockSpec((tm,) + d_block, lambda i, *_: (jnp.minimum(i, np_tiles - 1),) + index_rest),
            pl.BlockSpec((tm,) + d_block, lambda i, *_: (jnp.maximum(i - np_tiles, 0),) + index_rest))


def _router(h, router_w_b, router_bias, counts0):
    n, d = h.shape
    tm = next(t for t in (512, 256, 128, 64, 32, 16, 8) if n % t == 0)
    tok = pl.BlockSpec((tm, LANES), lambda i: (i, 0))
    cnt = pl.BlockSpec((1, N_EXPERTS), lambda i: (0, 0))
    return pl.pallas_call(
        functools.partial(_router_kernel, tm=tm), grid=(n // tm,),
        in_specs=[pl.BlockSpec((tm, d), lambda i: (i, 0)),
                  pl.BlockSpec((d, N_EXPERTS), lambda i: (0, 0)), cnt, cnt],
        out_specs=[tok, tok, tok, cnt],
        out_shape=[jax.ShapeDtypeStruct((n, LANES), I32),
                   jax.ShapeDtypeStruct((n, LANES), F32),
                   jax.ShapeDtypeStruct((n, LANES), I32),
                   jax.ShapeDtypeStruct((1, N_EXPERTS), F32)],
        scratch_shapes=[pltpu.VMEM((1, N_EXPERTS), F32)],
        compiler_params=_cparams(1), name="router")(h, router_w_b, router_bias, counts0)


SUBLANES = 8


def _gather_rows(idx_ref, src_hbm, dst, sem):
    def body(g, carry):
        for j in range(SUBLANES):
            tok = idx_ref[0, 0, g * SUBLANES + j]
            pltpu.make_async_copy(src_hbm.at[pl.ds(tok, 1)], dst.at[g, pl.ds(j, 1)], sem).start(priority=j % 2)
        return carry
    lax.fori_loop(0, dst.shape[0], body, 0)


def _wait_rows(dst, sem):
    pltpu.make_async_copy(dst, dst, sem).wait()


def _dest_kernel(eidx_ref, rank_ref, ps_ref, dest_ref, *, tm):
    lane = lax.broadcasted_iota(I32, (tm, N_EXPERTS), 1)
    out_lane = lax.broadcasted_iota(I32, (tm, LANES), 1)
    eidx = eidx_ref[...]
    rank = rank_ref[...].astype(F32)
    cols = [eidx[:, k:k + 1] for k in range(TOP_K)]
    picked = [jnp.where(lane == c, ps_ref[...], 0.0) for c in cols]
    starts = [jnp.sum(p, axis=-1, keepdims=True) for p in picked]
    dest = jnp.zeros((tm, LANES), F32)
    for k in range(TOP_K):
        dest = jnp.where(out_lane == k, starts[k] + rank[:, k:k + 1], dest)
    dest_ref[...] = dest.astype(I32)


def _dest(eidx, rank, pstarts, tm):
    n = eidx.shape[0]
    tok = pl.BlockSpec((tm, LANES), lambda i: (i, 0))
    return pl.pallas_call(
        functools.partial(_dest_kernel, tm=tm), grid=(n // tm,),
        in_specs=[tok, tok, pl.BlockSpec((1, N_EXPERTS), lambda i: (0, 0))],
        out_specs=tok, out_shape=jax.ShapeDtypeStruct((n, LANES), I32),
        compiler_params=_cparams(1), name="dest")(eidx, rank, pstarts)


def _dispatch_kernel(clear_ref, dest_ref, hp_ref, hs_ref, xs_hbm, pbuf, zbuf, sems, zsem, *, tm, nb, np_tiles):
    i = pl.program_id(0)
    n = pl.num_programs(0)
    slot = i % 2

    def zero_copy(blk):
        return pltpu.make_async_copy(zbuf, xs_hbm.at[pl.ds(blk * ROW_BLOCK, ROW_BLOCK)], zsem)

    def wait_slot(s):
        for _ in range(TOP_K):
            _wait_rows(pbuf.at[s], sems.at[s])

    @pl.when(i == 0)
    def _():
        zbuf[...] = jnp.zeros_like(zbuf)

        def start(blk, carry):
            @pl.when(clear_ref[blk] > 0)
            def _():
                zero_copy(blk).start()
            return carry

        def wait(blk, carry):
            @pl.when(clear_ref[blk] > 0)
            def _():
                zero_copy(blk).wait()
            return carry
        lax.fori_loop(0, nb, start, 0)
        lax.fori_loop(0, nb, wait, 0)

    @pl.when(i >= 2)
    def _():
        wait_slot(slot)

    pbuf[slot] = jnp.where(i < np_tiles, hp_ref[...], hs_ref[...])

    def body(g, carry):
        for j in range(SUBLANES):
            for k in range(TOP_K):
                row = dest_ref[0, 0, (g * SUBLANES + j) * TOP_K + k]
                pltpu.make_async_copy(pbuf.at[slot, g, pl.ds(j, 1)], xs_hbm.at[pl.ds(row, 1)],
                                      sems.at[slot]).start(priority=k % 2)
        return carry
    lax.fori_loop(0, tm // SUBLANES, body, 0)

    @pl.when(i == n - 1)
    def _():
        wait_slot(slot)

    @pl.when(jnp.logical_and(i == n - 1, n > 1))
    def _():
        wait_slot(1 - slot)


def _dispatch(clear, dest3, h_p, h_s, tm):
    (n_p, d), n_s = h_p.shape, h_s.shape[0]
    nb = clear.shape[0]
    groups = tm // SUBLANES
    rows3 = lambda h: h.reshape(h.shape[0] // SUBLANES, SUBLANES, d)
    grid_spec = pltpu.PrefetchScalarGridSpec(
        num_scalar_prefetch=1, grid=((n_p + n_s) // tm,),
        in_specs=[pl.BlockSpec((1, 1, tm * TOP_K), lambda i, cl: (i, 0, 0), memory_space=pltpu.SMEM),
                  *_token_specs(n_p // SUBLANES, groups, (SUBLANES, d), (0, 0))],
        out_specs=pl.BlockSpec(memory_space=pl.ANY),
        scratch_shapes=[pltpu.VMEM((2, groups, SUBLANES, d), F32), pltpu.VMEM((ROW_BLOCK, d), F32),
                        pltpu.SemaphoreType.DMA((2,)), pltpu.SemaphoreType.DMA(())])
    return pl.pallas_call(
        functools.partial(_dispatch_kernel, tm=tm, nb=nb, np_tiles=n_p // tm), grid_spec=grid_spec,
        out_shape=jax.ShapeDtypeStruct((nb * ROW_BLOCK, d), F32),
        compiler_params=_cparams(1), name="dispatch")(clear, dest3, rows3(h_p), rows3(h_s))


def _experts_kernel(be_ref, nact_ref, x_ref, g_hbm, u_hbm, d_hbm, y_ref, gf, uf, df, gb, ub, db, sems, slot_ref,
                    *, base, nb):
    i = pl.program_id(0)
    nact = nact_ref[0]

    def weight_copies(e, s):
        return (pltpu.make_async_copy(g_hbm.at[base + e], gf.at[s], sems.at[s]),
                pltpu.make_async_copy(u_hbm.at[base + e], uf.at[s], sems.at[s]),
                pltpu.make_async_copy(d_hbm.at[base + e], df.at[s], sems.at[s]))

    @pl.when(jnp.logical_and(i == 0, nact > 0))
    def _():
        slot_ref[0] = 0
        for c in weight_copies(be_ref[0], 0):
            c.start()

    first = jnp.logical_or(i == 0, be_ref[i] != be_ref[jnp.maximum(i - 1, 0)])

    @pl.when(jnp.logical_and(i < nact, first))
    def _():
        s = slot_ref[0]
        here = be_ref[i]
        nxt = lax.while_loop(lambda j: jnp.logical_and(j < nact, be_ref[jnp.minimum(j, nb - 1)] == here),
                             lambda j: j + 1, i + 1)

        @pl.when(nxt < nact)
        def _():
            for c in weight_copies(be_ref[jnp.minimum(nxt, nb - 1)], 1 - s):
                c.start()
        for c in weight_copies(here, s):
            c.wait()
        gb[...] = gf[s].astype(BF16)
        ub[...] = uf[s].astype(BF16)
        db[...] = df[s].astype(BF16)
        slot_ref[0] = 1 - s

    @pl.when(i < nact)
    def _():
        x = x_ref[...].astype(BF16)
        hid = _silu(_dot(x, gb[...])) * _dot(x, ub[...])
        y_ref[...] = _dot(hid.astype(BF16), db[...])


def _experts(block_e, nact, x_sorted, exp_gate, exp_up, exp_down, layer):
    nb = x_sorted.shape[0] // ROW_BLOCK
    _, d, de = exp_gate.shape
    base = layer * N_EXPERTS
    rows = pl.BlockSpec((ROW_BLOCK, d), lambda i, be, na: (jnp.minimum(i, jnp.maximum(na[0] - 1, 0)), 0))
    grid_spec = pltpu.PrefetchScalarGridSpec(
        num_scalar_prefetch=2, grid=(nb,),
        in_specs=[rows, pl.BlockSpec(memory_space=pl.ANY), pl.BlockSpec(memory_space=pl.ANY),
                  pl.BlockSpec(memory_space=pl.ANY)],
        out_specs=rows,
        scratch_shapes=[pltpu.VMEM((2, d, de), F32), pltpu.VMEM((2, d, de), F32), pltpu.VMEM((2, de, d), F32),
                        pltpu.VMEM((d, de), BF16), pltpu.VMEM((d, de), BF16), pltpu.VMEM((de, d), BF16),
                        pltpu.SemaphoreType.DMA((2,)), pltpu.SMEM((1,), I32)])
    return pl.pallas_call(
        functools.partial(_experts_kernel, base=base, nb=nb), grid_spec=grid_spec,
        out_shape=jax.ShapeDtypeStruct((nb * ROW_BLOCK, d), F32),
        input_output_aliases={2: 0},
        compiler_params=_cparams(1), name="experts")(block_e, nact, x_sorted, exp_gate, exp_up, exp_down)


def _combine_kernel(cur_ref, nxt_ref, y_hbm, wts_ref, x1_ref, h_ref, gt_ref, sg_ref, su_ref, sd_ref,
                    nf_ref, x2_ref, yn_ref, ybuf, sems, *, tm):
    i = pl.program_id(0)
    n = pl.num_programs(0)
    slot = i % 2
    groups = tm // SUBLANES

    @pl.when(i == 0)
    def _():
        _gather_rows(cur_ref, y_hbm, ybuf.at[0], sems.at[0])

    @pl.when(i + 1 < n)
    def _():
        _gather_rows(nxt_ref, y_hbm, ybuf.at[1 - slot], sems.at[1 - slot])

    hb = h_ref[...].astype(BF16)
    hid = _silu(_dot(hb, sg_ref[...])) * _dot(hb, su_ref[...])
    acc = _dot(hid.astype(BF16), sd_ref[...])
    _wait_rows(ybuf.at[slot], sems.at[slot])
    wts = wts_ref[...]
    for k in range(TOP_K):
        y_k = ybuf[slot, k * groups:(k + 1) * groups].reshape(tm, acc.shape[1])
        acc = acc + y_k * wts[:, k:k + 1]
    x2 = x1_ref[...] + gt_ref[0] * acc
    x2_ref[...] = x2
    yn_ref[...] = _rms(x2, nf_ref[...])


def _combine(dest3, y_sorted, wts, x1, h2, mod3, sh_gate_b, sh_up_b, sh_down_b, norm_final, tm, tiles_per_batch):
    n, d = x1.shape
    nt = n // tm
    rows = mod3.shape[1]
    ds = sh_gate_b.shape[1]
    tok = lambda w: pl.BlockSpec((tm, w), lambda i: (i, 0))
    kern = functools.partial(_combine_kernel, tm=tm)
    return pl.pallas_call(
        kern, grid=(nt,),
        in_specs=[pl.BlockSpec((1, 1, tm * TOP_K), lambda i: (i, 0, 0), memory_space=pltpu.SMEM),
                  pl.BlockSpec((1, 1, tm * TOP_K), lambda i: (jnp.minimum(i + 1, nt - 1), 0, 0),
                               memory_space=pltpu.SMEM),
                  pl.BlockSpec(memory_space=pl.ANY),
                  tok(LANES), tok(d), tok(d),
                  _mod_spec(rows, d, tiles_per_batch, 5),
                  pl.BlockSpec((d, ds), lambda i: (0, 0)),
                  pl.BlockSpec((d, ds), lambda i: (0, 0)),
                  pl.BlockSpec((ds, d), lambda i: (0, 0)),
                  pl.BlockSpec((1, d), lambda i: (0, 0))],
        out_specs=[tok(d), tok(d)],
        out_shape=[jax.ShapeDtypeStruct((n, d), F32)] * 2,
        scratch_shapes=[pltpu.VMEM((2, tm * TOP_K // SUBLANES, SUBLANES, d), F32),
                        pltpu.SemaphoreType.DMA((2,))],
        compiler_params=_cparams(1), name="combine")(
            dest3, dest3, y_sorted, wts, x1, h2, mod3, sh_gate_b, sh_up_b, sh_down_b, norm_final)


def _last_rows(buf, new, keep):
    t = new.shape[1]
    if t >= keep:
        return new[:, t - keep:]
    return jnp.concatenate([buf[:, t:], new], axis=1)


def _mixers(u, qkv, z, ba, b, t, pool_buf, conv_buf, s0, start_pos, lw):
    t_pool = -(-t // 8) * 8
    t_gdn = -(-t // CHUNK) * CHUNK
    u3 = u.reshape(b, t, POOL_W)
    qkv3 = qkv.reshape(b, t, CONV_CH)
    pad_t = lambda a, tp: a if tp == t else jnp.pad(a, ((0, 0), (0, tp - t), (0, 0)))
    buf16 = jnp.pad(pool_buf, ((0, 0), (POOL_HALO - POOL_BUF, 0), (0, 0)))
    pool_out = _pool(pad_t(u3, t_pool), buf16, lw["pool_w"], lw["pool_scale"], start_pos)[:, :t]
    if t == 1:
        s0_all, layer = s0
        dn_out, s_new = _gdn_step(qkv, z, ba, conv_buf, s0_all, layer, lw["conv_w"], lw["par"], lw["dn_norm"])
    else:
        cbuf8 = jnp.pad(conv_buf, ((0, 0), (CONV_HALO - (CONV_W - 1), 0), (0, 0)))
        dn_out, s_new = _gdn(pad_t(qkv3, t_gdn), pad_t(z.reshape(b, t, DN_W), t_gdn),
                             pad_t(ba.reshape(b, t, BA_W), t_gdn), cbuf8, s0, lw["conv_w"], lw["par"],
                             lw["dn_norm"], t)
        dn_out = dn_out[:, :t].reshape(b * t, DN_W)
    new_pool = _last_rows(pool_buf, u3, POOL_BUF)
    new_conv = _last_rows(conv_buf, qkv3, CONV_W - 1)
    return pool_out.reshape(b * t, POOL_W), dn_out, new_pool, new_conv, s_new


def _block_tables(counts, n_tok):
    nk = n_tok * TOP_K
    nb = -(-(nk + N_EXPERTS * (ROW_BLOCK - 1)) // ROW_BLOCK)
    pcounts = (counts + ROW_BLOCK - 1) // ROW_BLOCK * ROW_BLOCK
    pends = jnp.cumsum(pcounts)
    pstarts = pends - pcounts
    first_row = jnp.arange(nb, dtype=I32)[:, None] * ROW_BLOCK
    block_e = jnp.minimum(jnp.sum((pends[None, :] <= first_row).astype(I32), axis=1), N_EXPERTS - 1)
    nact = (pends[-1] // ROW_BLOCK).astype(I32).reshape(1)
    blk = jnp.arange(nb, dtype=I32)
    next_e = jnp.concatenate([block_e[1:], jnp.full((1,), -1, I32)])
    clear = ((blk >= nact[0] - 1) | (next_e != block_e)).astype(I32)
    return pstarts.astype(F32).reshape(1, N_EXPERTS), block_e, nact, clear


def _dest_tiles(dest, tm, k_major):
    n = dest.shape[0]
    tiles = dest.reshape(n // tm, tm, TOP_K)
    if k_major:
        tiles = tiles.transpose(0, 2, 1)
    return tiles.reshape(n // tm, 1, tm * TOP_K)


def kernel(x_prompt, x_sample, state_pool, state_conv, state_delta, c_prompt, c_sample, norm_mix, norm_ffn, w_ada, b_ada, w_in, pool_w, pool_scale, conv_w, a_log, dt_bias, dn_norm, w_out, router_w, router_bias, exp_gate, exp_up, exp_down, sh_gate, sh_up, sh_down, norm_final):
    bp, tp, d = x_prompt.shape
    bs, ts, _ = x_sample.shape
    depth = w_ada.shape[0]
    past_len = PAST_LEN
    n_p, n_s = bp * tp, bs * ts
    n_tok = n_p + n_s
    tm_p = min(512, tp)
    tm_c = min(128, tp)
    tm_r = next(t for t in (128, 64, 32, 16, 8) if n_p % t == 0 and n_s % t == 0)
    assert tp % tm_p == 0 and ts == 1 and bs % 8 == 0
    n_exp = exp_gate.shape[1]
    gate_w = exp_gate.reshape((depth * n_exp,) + exp_gate.shape[2:])
    up_w = exp_up.reshape((depth * n_exp,) + exp_up.shape[2:])
    down_w = exp_down.reshape((depth * n_exp,) + exp_down.shape[2:])

    c_all = jnp.concatenate([c_prompt, c_sample], axis=0)
    m_rows = -(-c_all.shape[0] // 8) * 8
    mod_all = _ada(jnp.pad(c_all, ((0, m_rows - c_all.shape[0]), (0, 0))), w_ada, b_ada)

    c2 = POOL_W + CONV_CH + DN_W
    lane_pad = jnp.zeros((depth, d, LANES - DN_HEADS), F32)
    w_in_p = jnp.concatenate([w_in[:, :, :c2], w_in[:, :, c2:c2 + DN_HEADS], lane_pad,
                              w_in[:, :, c2 + DN_HEADS:], lane_pad], axis=-1).astype(BF16)
    head_pad = lambda a: jnp.pad(a, ((0, 0), (0, LANES - DN_HEADS)))
    par_all = jnp.stack([head_pad(a_log), head_pad(dt_bias)] + [jnp.zeros((depth, LANES), F32)] * 6, axis=1)
    conv_w8 = jnp.pad(conv_w, ((0, 0), (0, 8 - CONV_W), (0, 0)))

    xp = x_prompt.reshape(n_p, d)
    xs = x_sample.reshape(n_s, d)
    zero_pool = jnp.zeros((bp, POOL_BUF, POOL_W), F32)
    zero_conv = jnp.zeros((bp, CONV_W - 1, CONV_CH), F32)
    zero_state = jnp.zeros((bp, DN_HEADS, DN_D, DN_D), F32)
    outs = {k: [] for k in ("pool_p", "conv_p", "delta_p", "pool_s", "conv_s", "delta_s")}
    yp = ys = None
    for l in range(depth):
        lw = {"pool_w": pool_w[l].astype(BF16), "pool_scale": pool_scale[l].reshape(1, POOL_W),
              "conv_w": conv_w8[l], "par": par_all[l], "dn_norm": dn_norm[l].reshape(1, DN_D)}
        mod_p = mod_all[l, :bp].reshape(bp, 1, 6 * d)
        mod_s = mod_all[l, bp:bp + bs].reshape(1, bs, 6 * d)
        nw_mix = norm_mix[l].reshape(1, d)
        nw_ffn = norm_ffn[l].reshape(1, d)
        w_out_b = w_out[l].astype(BF16)

        up, qkvp, zp, bap = _norm_proj(xp, mod_p, nw_mix, w_in_p[l], tm_p, tp // tm_p)
        us, qkvs, zs, bas = _norm_proj(xs, mod_s, nw_mix, w_in_p[l], n_s, 1)
        pool_p, dn_p, npool_p, nconv_p, ns_p = _mixers(up, qkvp, zp, bap, bp, tp, zero_pool, zero_conv,
                                                       zero_state, 0, lw)
        pool_s, dn_s, npool_s, nconv_s, ns_s = _mixers(us, qkvs, zs, bas, bs, ts, state_pool[l], state_conv[l],
                                                       (state_delta, l), past_len, lw)
        for key, val in (("pool_p", npool_p), ("conv_p", nconv_p), ("delta_p", ns_p),
                         ("pool_s", npool_s), ("conv_s", nconv_s), ("delta_s", ns_s)):
            outs[key].append(val)

        x1p, h2p = _out_proj(pool_p, dn_p, xp, mod_p, nw_ffn, w_out_b, tm_p, tp // tm_p)
        x1s, h2s = _out_proj(pool_s, dn_s, xs, mod_s, nw_ffn, w_out_b, n_s, 1)

        rw_b, rb = router_w[l].astype(BF16), router_bias[l].reshape(1, N_EXPERTS)
        eidx_p, wts_p, rank_p, counts = _router(h2p, rw_b, rb, jnp.zeros((1, N_EXPERTS), F32))
        eidx_s, wts_s, rank_s, counts = _router(h2s, rw_b, rb, counts)
        pstarts, block_e, nact, clear = _block_tables(counts.reshape(-1).astype(I32), n_tok)
        dest = jnp.concatenate([_dest(eidx_p, rank_p, pstarts, tm_r)[:, :TOP_K],
                                _dest(eidx_s, rank_s, pstarts, tm_r)[:, :TOP_K]])
        x_sorted = _dispatch(clear, _dest_tiles(dest, tm_r, False), h2p, h2s, tm_r)
        y_sorted = _experts(block_e, nact, x_sorted, gate_w, up_w, down_w, l)

        shg, shu, shd = sh_gate[l].astype(BF16), sh_up[l].astype(BF16), sh_down[l].astype(BF16)
        nf = norm_final.reshape(1, d)
        xp, yp = _combine(_dest_tiles(dest[:n_p], tm_c, True), y_sorted, wts_p, x1p, h2p, mod_p, shg, shu, shd,
                          nf, tm_c, tp // tm_c)
        xs, ys = _combine(_dest_tiles(dest[n_p:], n_s, True), y_sorted, wts_s, x1s, h2s, mod_s, shg, shu, shd,
                          nf, n_s, 1)

    stack = lambda key: jnp.stack(outs[key])
    return (yp.reshape(bp, tp, d), ys.reshape(bs, ts, d), stack("pool_p"), stack("conv_p"), stack("delta_p"),
            stack("pool_s"), stack("conv_s"), stack("delta_s"))
```

```python
import functools

import jax
import jax.numpy as jnp
from jax import lax
from jax.experimental import pallas as pl
from jax.experimental.pallas import tpu as pltpu

F32, BF16, I32 = jnp.float32, jnp.bfloat16, jnp.int32
HI = lax.Precision.HIGHEST
EPS = 1e-6

LANES = 128
POOL_WINDOWS = (2, 4, 8, 16)
POOL_GW = 128
POOL_W = 512
POOL_BUF = 15
POOL_HALO = 16
POOL_TILE = 512
DN_HEADS = 4
DN_D = 128
DN_W = 512
CONV_W = 4
CONV_CH = 3 * DN_W
CONV_HALO = 8
CHUNK = 64
INV_BLOCK = 16
GDN_ROWS = 8
N_EXPERTS = 256
TOP_K = 8
N_GROUPS = 8
TOPK_GROUPS = 4
GROUP_SHIFT = 5
ROUTED_SCALE = 2.5
PAST_LEN = 16384
ROW_BLOCK = 256
BA_W = 2 * LANES
D_IN_PAD = POOL_W + CONV_CH + DN_W + BA_W
VMEM_LIMIT = 48 * 2 ** 20


def _cparams(n_axes):
    return pltpu.CompilerParams(dimension_semantics=("arbitrary",) * n_axes,
                                vmem_limit_bytes=VMEM_LIMIT)


def _sigmoid(x):
    return 1.0 / (1.0 + jnp.exp(-x))


def _silu(x):
    return x * _sigmoid(x)


def _softplus(x):
    return jnp.maximum(x, 0.0) + jnp.log(1.0 + jnp.exp(-jnp.abs(x)))


def _rms(x, w):
    return x * lax.rsqrt(jnp.mean(x * x, axis=-1, keepdims=True) + EPS) * w


def _dot(a, b, precision=None):
    return jnp.dot(a, b, preferred_element_type=F32, precision=precision)


def _dot_nt(a, b, precision=None):
    return lax.dot_general(a, b, (((1,), (1,)), ((), ())), preferred_element_type=F32,
                           precision=precision)


def _ada_kernel(c_ref, w_ref, b_ref, o_ref):
    o_ref[0] = _dot(_silu(c_ref[...]), w_ref[0], HI) + b_ref[0]


def _ada(c_all, w_ada, b_ada):
    depth, d, d6 = w_ada.shape
    m = c_all.shape[0]
    tn = 1024
    return pl.pallas_call(
        _ada_kernel, grid=(depth, d6 // tn),
        in_specs=[pl.BlockSpec((m, d), lambda l, j: (0, 0)),
                  pl.BlockSpec((1, d, tn), lambda l, j: (l, 0, j)),
                  pl.BlockSpec((1, 1, tn), lambda l, j: (l, 0, j))],
        out_specs=pl.BlockSpec((1, m, tn), lambda l, j: (l, 0, j)),
        out_shape=jax.ShapeDtypeStruct((depth, m, d6), F32),
        compiler_params=_cparams(2), name="ada")(c_all, w_ada, b_ada.reshape(depth, 1, d6))


def _mod_spec(rows, d, tiles_per_batch, chunk):
    return pl.BlockSpec((1, rows, d), lambda i: (i // tiles_per_batch, 0, chunk))


def _norm_proj_kernel(x_ref, sh_ref, sc_ref, nw_ref, w_ref, u_ref, qkv_ref, z_ref, ba_ref):
    h = _rms(x_ref[...], nw_ref[...]) * (1.0 + sc_ref[0]) + sh_ref[0]
    hb = h.astype(BF16)
    c0, c1, c2 = POOL_W, POOL_W + CONV_CH, POOL_W + CONV_CH + DN_W
    u_ref[...] = _dot(hb, w_ref[:, :c0])
    qkv_ref[...] = _dot(hb, w_ref[:, c0:c1])
    z_ref[...] = _dot(hb, w_ref[:, c1:c2])
    ba_ref[...] = _dot(hb, w_ref[:, c2:])


def _norm_proj(x2d, mod3, nw, w_in_p, tm, tiles_per_batch):
    n, d = x2d.shape
    rows = mod3.shape[1]
    widths = (POOL_W, CONV_CH, DN_W, BA_W)
    return pl.pallas_call(
        _norm_proj_kernel, grid=(n // tm,),
        in_specs=[pl.BlockSpec((tm, d), lambda i: (i, 0)),
                  _mod_spec(rows, d, tiles_per_batch, 0),
                  _mod_spec(rows, d, tiles_per_batch, 1),
                  pl.BlockSpec((1, d), lambda i: (0, 0)),
                  pl.BlockSpec((d, D_IN_PAD), lambda i: (0, 0))],
        out_specs=[pl.BlockSpec((tm, w), lambda i: (i, 0)) for w in widths],
        out_shape=[jax.ShapeDtypeStruct((n, w), F32) for w in widths],
        compiler_params=_cparams(1), name="norm_proj")(x2d, mod3, mod3, nw, w_in_p)


def _pool_kernel(u_ref, buf_ref, pw_ref, ps_ref, o_ref, ext, *, tt, gb, start_pos):
    t = pl.program_id(1)

    @pl.when(t == 0)
    def _():
        ext[:, 0:POOL_HALO, :] = buf_ref[...]

    @pl.when(t > 0)
    def _():
        ext[:, 0:POOL_HALO, :] = ext[:, tt:tt + POOL_HALO, :]

    ext[:, POOL_HALO:POOL_HALO + tt, :] = u_ref[...]
    pos = start_pos + t * tt + lax.broadcasted_iota(I32, (tt, 1), 0)
    for g, win in enumerate(POOL_WINDOWS):
        sl = slice(g * POOL_GW, (g + 1) * POOL_GW)
        cnt = jnp.minimum(pos + 1, win).astype(F32)
        pooled = []
        for rb in range(gb):
            u = u_ref[rb, :, sl]
            s = u
            for j in range(1, win):
                s = s + ext[rb, POOL_HALO - j:POOL_HALO - j + tt, sl]
            pooled.append(s / cnt - u)
        mixed = _dot(jnp.concatenate(pooled, axis=0).astype(BF16), pw_ref[g]) * ps_ref[:, sl]
        for rb in range(gb):
            o_ref[rb, :, sl] = mixed[rb * tt:(rb + 1) * tt]


def _pool(u3, buf16, pool_w_b, pool_scale, start_pos):
    b, t, w = u3.shape
    tt = min(t, POOL_TILE)
    assert t % tt == 0 and tt % 8 == 0
    gb = next(g for g in (16, 8, 4, 2, 1) if b % g == 0 and g * tt <= POOL_TILE)
    return pl.pallas_call(
        functools.partial(_pool_kernel, tt=tt, gb=gb, start_pos=start_pos), grid=(b // gb, t // tt),
        in_specs=[pl.BlockSpec((gb, tt, w), lambda i, j: (i, j, 0)),
                  pl.BlockSpec((gb, POOL_HALO, w), lambda i, j: (i, 0, 0)),
                  pl.BlockSpec((len(POOL_WINDOWS), POOL_GW, POOL_GW), lambda i, j: (0, 0, 0)),
                  pl.BlockSpec((1, w), lambda i, j: (0, 0))],
        out_specs=pl.BlockSpec((gb, tt, w), lambda i, j: (i, j, 0)),
        out_shape=jax.ShapeDtypeStruct((b, t, w), F32),
        scratch_shapes=[pltpu.VMEM((gb, POOL_HALO + tt, w), F32)],
        compiler_params=_cparams(2), name="pool")(u3, buf16, pool_w_b, pool_scale)


def _bf(x):
    return x.astype(BF16)


def _dot3(a, b):
    a_hi, b_hi = _bf(a), _bf(b)
    a_lo, b_lo = _bf(a - a_hi.astype(F32)), _bf(b - b_hi.astype(F32))
    return _dot(a_hi, b_hi) + (_dot(a_hi, b_lo) + _dot(a_lo, b_hi))


def _each(fn, *lists):
    return [fn(*args) for args in zip(*lists)]


def _unit_lower_inverse(a, row, col, eye):
    same = (row // INV_BLOCK) == (col // INV_BLOCK)
    dm = _each(lambda x: jnp.where(same, -x, 0.0), a)
    lm = _each(lambda x, d: -x - d, a, dm)
    mm = lambda x, y: _dot(_bf(x), _bf(y))
    d2 = _each(mm, dm, dm)
    d4 = _each(mm, d2, d2)
    d8 = _each(mm, d4, d4)
    p = _each(lambda x, y: mm(eye + x, eye + y), dm, d2)
    p2 = _each(lambda x, y: mm(eye + x, eye + y), d4, d8)
    p = _each(mm, p, p2)
    m = _each(mm, p, lm)
    m2 = _each(mm, m, m)
    q = _each(lambda x, y: mm(eye + x, eye + y), m, m2)
    t0 = _each(mm, q, p)
    at = _each(_dot3, a, t0)
    resid = _each(lambda t, x: (eye - t) - x, t0, at)
    return _each(lambda t, r: t + mm(t, r), t0, resid)


def _gdn_kernel(qkv_ref, z_ref, ba_ref, cbuf_ref, s0_ref, cw_ref, par_ref, nw_ref,
                o_ref, sfin_ref, state, xext, *, t_len, gb):
    c = pl.program_id(1)
    seqs = range(gb)
    heads = range(DN_HEADS)
    hc = DN_HEADS * CHUNK

    @pl.when(c == 0)
    def _():
        for rb in seqs:
            state[rb * DN_HEADS:(rb + 1) * DN_HEADS] = s0_ref[rb]
            xext[rb, 0:CONV_HALO, :] = cbuf_ref[rb]

    @pl.when(c > 0)
    def _():
        for rb in seqs:
            xext[rb, 0:CONV_HALO, :] = xext[rb, CHUNK:CHUNK + CONV_HALO, :]

    valid = (c * CHUNK + lax.broadcasted_iota(I32, (CHUNK, 1), 0)) < t_len
    vf = valid.astype(F32)
    head_lane = lax.broadcasted_iota(I32, (1, LANES), 1) < DN_HEADS
    decay = jnp.where(head_lane, -jnp.exp(par_ref[0:1, :]), 0.0)
    r64 = lax.broadcasted_iota(I32, (CHUNK, CHUNK), 0)
    c64 = lax.broadcasted_iota(I32, (CHUNK, CHUNK), 1)
    tri = (r64 >= c64).astype(F32)
    pick = (lax.broadcasted_iota(I32, (8, LANES), 0) == lax.broadcasted_iota(I32, (8, LANES), 1)).astype(F32)
    row = lax.broadcasted_iota(I32, (hc, hc), 0)
    col = lax.broadcasted_iota(I32, (hc, hc), 1)
    incl = ((row // CHUNK) == (col // CHUNK)) & (row >= col)
    strict = row > col
    eye = (row == col).astype(F32)
    base = CONV_HALO - (CONV_W - 1)

    def conv(rb):
        xext[rb, CONV_HALO:CONV_HALO + CHUNK, :] = qkv_ref[rb]
        acc = xext[rb, base:base + CHUNK, :] * cw_ref[0:1, :]
        for j in range(1, CONV_W):
            acc = acc + xext[rb, base + j:base + j + CHUNK, :] * cw_ref[j:j + 1, :]
        return _silu(acc)
    qkvc = [conv(rb) for rb in seqs]
    ba = [ba_ref[rb] for rb in seqs]
    beta_all = _each(lambda x: _sigmoid(x[:, :LANES]) * vf, ba)
    g_all = _each(lambda x: decay * _softplus(x[:, LANES:] + par_ref[1:2, :]) * vf, ba)
    gam_all = _each(lambda g: _dot(tri, g, HI), g_all)
    gam_rows = _each(lambda g: _dot_nt(pick, g, HI), gam_all)

    def unit(x):
        return x * lax.rsqrt(jnp.sum(x * x, axis=-1, keepdims=True) + EPS)
    q_s = _each(lambda x: jnp.concatenate(
        [unit(x[:, h * DN_D:(h + 1) * DN_D]) * (DN_D ** -0.5) * vf for h in heads], axis=0), qkvc)
    k_s = _each(lambda x: jnp.concatenate(
        [unit(x[:, DN_W + h * DN_D:DN_W + (h + 1) * DN_D]) * vf for h in heads], axis=0), qkvc)
    v_s = _each(lambda x: jnp.concatenate(
        [x[:, 2 * DN_W + h * DN_D:2 * DN_W + (h + 1) * DN_D] * vf for h in heads], axis=0), qkvc)
    beta = _each(lambda x: jnp.concatenate([x[:, h:h + 1] for h in heads], axis=0), beta_all)
    gam = _each(lambda x: jnp.concatenate([x[:, h:h + 1] for h in heads], axis=0), gam_all)
    gam_r = _each(lambda x: jnp.concatenate([x[h:h + 1, :] for h in heads], axis=1), gam_rows)

    dec = _each(lambda g, gr: jnp.where(incl, jnp.exp(jnp.where(incl, g - gr, 0.0)), 0.0), gam, gam_r)
    kb = _each(_bf, k_s)
    qb = _each(_bf, q_s)
    kk = _each(_dot_nt, kb, kb)
    qk = _each(_dot_nt, qb, kb)
    a_mat = _each(lambda b_, x, d_: b_ * x * jnp.where(strict, d_, 0.0), beta, kk, dec)
    qk = _each(lambda x, d_: _bf(x * d_), qk, dec)
    egam = _each(jnp.exp, gam)
    rhs = _each(lambda b_, e, k, v: jnp.concatenate([(b_ * e) * k, b_ * v], axis=-1),
                beta, egam, k_s, v_s)
    t_inv = _unit_lower_inverse(a_mat, row, col, eye)
    sol = _each(_dot3, t_inv, rhs)
    wb = _each(lambda x: _bf(x[:, :DN_D]), sol)
    uv = _each(lambda x: x[:, DN_D:], sol)

    hrows = [slice(h * CHUNK, (h + 1) * CHUNK) for h in heads]
    s_f = [[state[rb * DN_HEADS + h] for h in heads] for rb in seqs]
    s_b = [[_bf(x) for x in per_seq] for per_seq in s_f]
    ub = _each(lambda u_, w_, s_: _bf(jnp.concatenate(
        [u_[hrows[h]] - _dot(w_[hrows[h]], s_[h]) for h in heads], axis=0)), uv, wb, s_b)
    o_intra = _each(_dot, qk, ub)
    for h in heads:
        rows = hrows[h]
        sl = slice(h * DN_D, (h + 1) * DN_D)
        o = _each(lambda e, q_, s_, oi: e[rows] * _dot(q_[rows], s_[h]) + oi[rows], egam, qb, s_b, o_intra)
        g_last = _each(lambda g: g[(h + 1) * CHUNK - 1:(h + 1) * CHUNK, :], gam)
        k_dec = _each(lambda k, gl, g: _bf((k[rows] * jnp.exp(gl - g[rows])).T), k_s, g_last, gam)
        upd = _each(lambda kd, u_: _dot(kd, u_[rows]), k_dec, ub)
        for rb in seqs:
            state[rb * DN_HEADS + h] = jnp.exp(g_last[rb]) * s_f[rb][h] + upd[rb]
            o_ref[rb, :, sl] = _rms(o[rb], nw_ref[...]) * _silu(z_ref[rb, :, sl])

    @pl.when(c == pl.num_programs(1) - 1)
    def _():
        for rb in seqs:
            sfin_ref[rb] = state[rb * DN_HEADS:(rb + 1) * DN_HEADS]


def _gdn(qkv3, z3, ba3, cbuf8, s0, conv_w8, par, dn_norm, t_len):
    b, tp, _ = qkv3.shape
    assert tp % CHUNK == 0
    gb = GDN_ROWS if b % GDN_ROWS == 0 else 1
    kern = functools.partial(_gdn_kernel, t_len=t_len, gb=gb)
    return pl.pallas_call(
        kern, grid=(b // gb, tp // CHUNK),
        in_specs=[pl.BlockSpec((gb, CHUNK, CONV_CH), lambda i, j: (i, j, 0)),
                  pl.BlockSpec((gb, CHUNK, DN_W), lambda i, j: (i, j, 0)),
                  pl.BlockSpec((gb, CHUNK, BA_W), lambda i, j: (i, j, 0)),
                  pl.BlockSpec((gb, CONV_HALO, CONV_CH), lambda i, j: (i, 0, 0)),
                  pl.BlockSpec((gb, DN_HEADS, DN_D, DN_D), lambda i, j: (i, 0, 0, 0)),
                  pl.BlockSpec((8, CONV_CH), lambda i, j: (0, 0)),
                  pl.BlockSpec((8, LANES), lambda i, j: (0, 0)),
                  pl.BlockSpec((1, DN_D), lambda i, j: (0, 0))],
        out_specs=[pl.BlockSpec((gb, CHUNK, DN_W), lambda i, j: (i, j, 0)),
                   pl.BlockSpec((gb, DN_HEADS, DN_D, DN_D), lambda i, j: (i, 0, 0, 0))],
        out_shape=[jax.ShapeDtypeStruct((b, tp, DN_W), F32),
                   jax.ShapeDtypeStruct((b, DN_HEADS, DN_D, DN_D), F32)],
        scratch_shapes=[pltpu.VMEM((gb * DN_HEADS, DN_D, DN_D), F32),
                        pltpu.VMEM((gb, CONV_HALO + CHUNK, CONV_CH), F32)],
        compiler_params=_cparams(2), name="gdn")(qkv3, z3, ba3, cbuf8, s0, conv_w8, par, dn_norm)


STEP_TILE = 8


def _gdn_step_kernel(qkv_ref, z_ref, ba_ref, cst_ref, s0_all_ref, cw_ref, par_ref, nw_ref, o_ref, s_ref):
    s0_ref = s0_all_ref.at[0]
    acc = qkv_ref[...] * cw_ref[CONV_W - 1:CONV_W, :]
    for j in range(CONV_W - 1):
        acc = acc + cst_ref[:, j, :] * cw_ref[j:j + 1, :]
    qkvc = _silu(acc)
    ba = ba_ref[...]
    head_lane = lax.broadcasted_iota(I32, (1, LANES), 1) < DN_HEADS
    beta_all = _sigmoid(ba[:, :LANES])
    decay = jnp.where(head_lane, -jnp.exp(par_ref[0:1, :]), 0.0)
    eg_all = jnp.exp(decay * _softplus(ba[:, LANES:] + par_ref[1:2, :]))
    pad6 = jnp.zeros((6, DN_D), F32)
    pad7 = jnp.zeros((7, DN_D), F32)
    for h in range(DN_HEADS):
        q = qkvc[:, h * DN_D:(h + 1) * DN_D]
        k = qkvc[:, DN_W + h * DN_D:DN_W + (h + 1) * DN_D]
        v = qkvc[:, 2 * DN_W + h * DN_D:2 * DN_W + (h + 1) * DN_D]
        q = q * lax.rsqrt(jnp.sum(q * q, axis=-1, keepdims=True) + EPS) * (DN_D ** -0.5)
        k = k * lax.rsqrt(jnp.sum(k * k, axis=-1, keepdims=True) + EPS)
        qk = _dot_nt(_bf(q), _bf(k))
        seqs = list(range(STEP_TILE))
        beta = [beta_all[b:b + 1, h:h + 1] for b in seqs]
        eg = [eg_all[b:b + 1, h:h + 1] for b in seqs]
        s_old = [s0_ref[b, h] for b in seqs]
        ks_qs = _each(lambda b, s_: _dot(_bf(jnp.concatenate([k[b:b + 1], q[b:b + 1], pad6], axis=0)), _bf(s_)),
                      seqs, s_old)
        u = _each(lambda b, b_, e, r: b_ * (v[b:b + 1] - e * r[0:1]), seqs, beta, eg, ks_qs)
        o_rows = _each(lambda b, e, r, u_: e * r[1:2] + qk[b:b + 1, b:b + 1] * u_, seqs, eg, ks_qs, u)
        k_col = [_bf(jnp.concatenate([k[b:b + 1], pad7], axis=0).T) for b in seqs]
        upd = _each(lambda kc, u_: _dot(kc, _bf(jnp.concatenate([u_, pad7], axis=0))), k_col, u)
        for b in seqs:
            s_ref[b, h] = eg[b] * s_old[b] + upd[b]
        o = jnp.concatenate(o_rows, axis=0)
        sl = slice(h * DN_D, (h + 1) * DN_D)
        o_ref[:, sl] = _rms(o, nw_ref[...]) * _silu(z_ref[:, sl])


def _gdn_step(qkv2, z2, ba2, conv_state, s0_all, layer, conv_w8, par, dn_norm):
    b = qkv2.shape[0]
    assert b % STEP_TILE == 0
    row = lambda w: pl.BlockSpec((STEP_TILE, w), lambda i: (i, 0))
    st = pl.BlockSpec((STEP_TILE, DN_HEADS, DN_D, DN_D), lambda i: (i, 0, 0, 0))
    st_in = pl.BlockSpec((1, STEP_TILE, DN_HEADS, DN_D, DN_D), lambda i: (layer, i, 0, 0, 0))
    return pl.pallas_call(
        _gdn_step_kernel, grid=(b // STEP_TILE,),
        in_specs=[row(CONV_CH), row(DN_W), row(BA_W),
                  pl.BlockSpec((STEP_TILE, CONV_W - 1, CONV_CH), lambda i: (i, 0, 0)), st_in,
                  pl.BlockSpec((8, CONV_CH), lambda i: (0, 0)),
                  pl.BlockSpec((8, LANES), lambda i: (0, 0)),
                  pl.BlockSpec((1, DN_D), lambda i: (0, 0))],
        out_specs=[row(DN_W), st],
        out_shape=[jax.ShapeDtypeStruct((b, DN_W), F32),
                   jax.ShapeDtypeStruct((b, DN_HEADS, DN_D, DN_D), F32)],
        compiler_params=_cparams(1), name="gdn_step")(qkv2, z2, ba2, conv_state, s0_all, conv_w8, par, dn_norm)


def _out_proj_kernel(pool_ref, dn_ref, x_ref, gt_ref, sh_ref, sc_ref, nw_ref, w_ref, x1_ref, h2_ref):
    mix = _dot(pool_ref[...].astype(BF16), w_ref[:POOL_W, :]) + _dot(dn_ref[...].astype(BF16), w_ref[POOL_W:, :])
    x1 = x_ref[...] + gt_ref[0] * mix
    x1_ref[...] = x1
    h2_ref[...] = _rms(x1, nw_ref[...]) * (1.0 + sc_ref[0]) + sh_ref[0]


def _out_proj(pool2, dn2, x2d, mod3, nw, w_out_b, tm, tiles_per_batch):
    n, d = x2d.shape
    rows = mod3.shape[1]
    tok = lambda w: pl.BlockSpec((tm, w), lambda i: (i, 0))
    return pl.pallas_call(
        _out_proj_kernel, grid=(n // tm,),
        in_specs=[tok(POOL_W), tok(DN_W), tok(d),
                  _mod_spec(rows, d, tiles_per_batch, 2),
                  _mod_spec(rows, d, tiles_per_batch, 3),
                  _mod_spec(rows, d, tiles_per_batch, 4),
                  pl.BlockSpec((1, d), lambda i: (0, 0)),
                  pl.BlockSpec((POOL_W + DN_W, d), lambda i: (0, 0))],
        out_specs=[tok(d), tok(d)],
        out_shape=[jax.ShapeDtypeStruct((n, d), F32)] * 2,
        compiler_params=_cparams(1), name="out_proj")(pool2, dn2, x2d, mod3, mod3, mod3, nw, w_out_b)


def _router_kernel(h_ref, rw_ref, rb_ref, cnt0_ref, eidx_ref, wts_ref, rank_ref, cnt_ref, run, *, tm):
    i = pl.program_id(0)

    @pl.when(i == 0)
    def _():
        run[...] = cnt0_ref[...]

    s = _sigmoid(_dot(h_ref[...].astype(BF16), rw_ref[...]))
    sel = s + rb_ref[...]
    lane_i = lax.broadcasted_iota(I32, (tm, N_EXPERTS), 1)
    lane = lane_i.astype(F32)
    grp = lane_i >> GROUP_SHIFT
    neg = -jnp.inf

    def first_max(x):
        m = jnp.max(x, axis=-1, keepdims=True)
        idx = jnp.min(jnp.where(x == m, lane, float(N_EXPERTS)), axis=-1, keepdims=True)
        return m, idx

    gs = []
    for g in range(N_GROUPS):
        xg = jnp.where(grp == g, sel, neg)
        m1, i1 = first_max(xg)
        m2 = jnp.max(jnp.where(lane == i1, neg, xg), axis=-1, keepdims=True)
        gs.append(m1 + m2)
    cur = jnp.full((tm, N_EXPERTS), neg, F32)
    for g in range(N_GROUPS):
        ahead = jnp.zeros((tm, 1), F32)
        for o in range(N_GROUPS):
            if o < g:
                ahead = ahead + jnp.where(gs[o] >= gs[g], 1.0, 0.0)
            elif o > g:
                ahead = ahead + jnp.where(gs[o] > gs[g], 1.0, 0.0)
        cur = jnp.where((grp == g) & (ahead < TOPK_GROUPS), sel, cur)

    hits, idxs, ws = [], [], []
    chosen = jnp.zeros((tm, N_EXPERTS), F32)
    for _ in range(TOP_K):
        _, idx = first_max(cur)
        hit = lane == idx
        ws.append(jnp.sum(jnp.where(hit, s, 0.0), axis=-1, keepdims=True))
        cur = jnp.where(hit, neg, cur)
        chosen = jnp.where(hit, 1.0, chosen)
        hits.append(hit)
        idxs.append(idx)
    onehot = chosen
    r = lax.broadcasted_iota(I32, (tm, tm), 0)
    c = lax.broadcasted_iota(I32, (tm, tm), 1)
    earlier = jnp.where(r > c, 1.0, 0.0).astype(BF16)
    before = _dot(earlier, onehot.astype(BF16)) + run[...]
    run[...] = run[...] + jnp.sum(onehot, axis=0, keepdims=True)

    wsum = ws[0]
    for w in ws[1:]:
        wsum = wsum + w
    out_lane = lax.broadcasted_iota(I32, (tm, LANES), 1)
    eidx = jnp.zeros((tm, LANES), F32)
    wts = jnp.zeros((tm, LANES), F32)
    rank = jnp.zeros((tm, LANES), F32)
    for k in range(TOP_K):
        rk = jnp.sum(jnp.where(hits[k], before, 0.0), axis=-1, keepdims=True)
        eidx = jnp.where(out_lane == k, idxs[k], eidx)
        wts = jnp.where(out_lane == k, ws[k] / wsum * ROUTED_SCALE, wts)
        rank = jnp.where(out_lane == k, rk, rank)
    eidx_ref[...] = eidx.astype(I32)
    wts_ref[...] = wts
    rank_ref[...] = rank.astype(I32)

    @pl.when(i == pl.num_programs(0) - 1)
    def _():
        cnt_ref[...] = run[...]


def _token_specs(n_p, tm, d_block, index_rest):
    np_tiles = n_p // tm
    return (pl.BlockSpec((tm,) + d_block, lambda i, *_: (jnp.minimum(i, np_tiles - 1),) + index_rest),
            pl.BlockSpec((tm,) + d_block, lambda i, *_: (jnp.maximum(i - np_tiles, 0),) + index_rest))


def _router(h, router_w_b, router_bias, counts0):
    n, d = h.shape
    tm = next(t for t in (512, 256, 128, 64, 32, 16, 8) if n % t == 0)
    tok = pl.BlockSpec((tm, LANES), lambda i: (i, 0))
    cnt = pl.BlockSpec((1, N_EXPERTS), lambda i: (0, 0))
    return pl.pallas_call(
        functools.partial(_router_kernel, tm=tm), grid=(n // tm,),
        in_specs=[pl.BlockSpec((tm, d), lambda i: (i, 0)),
                  pl.BlockSpec((d, N_EXPERTS), lambda i: (0, 0)), cnt, cnt],
        out_specs=[tok, tok, tok, cnt],
        out_shape=[jax.ShapeDtypeStruct((n, LANES), I32),
                   jax.ShapeDtypeStruct((n, LANES), F32),
                   jax.ShapeDtypeStruct((n, LANES), I32),
                   jax.ShapeDtypeStruct((1, N_EXPERTS), F32)],
        scratch_shapes=[pltpu.VMEM((1, N_EXPERTS), F32)],
        compiler_params=_cparams(1), name="router")(h, router_w_b, router_bias, counts0)


SUBLANES = 8


def _gather_rows(idx_ref, src_hbm, dst, sem):
    def body(g, carry):
        for j in range(SUBLANES):
            tok = idx_ref[0, 0, g * SUBLANES + j]
            pltpu.make_async_copy(src_hbm.at[pl.ds(tok, 1)], dst.at[g, pl.ds(j, 1)], sem).start(priority=j % 2)
        return carry
    lax.fori_loop(0, dst.shape[0], body, 0)


def _wait_rows(dst, sem):
    pltpu.make_async_copy(dst, dst, sem).wait()


def _dest_kernel(eidx_ref, rank_ref, ps_ref, dest_ref, *, tm):
    lane = lax.broadcasted_iota(I32, (tm, N_EXPERTS), 1)
    out_lane = lax.broadcasted_iota(I32, (tm, LANES), 1)
    eidx = eidx_ref[...]
    rank = rank_ref[...].astype(F32)
    cols = [eidx[:, k:k + 1] for k in range(TOP_K)]
    picked = [jnp.where(lane == c, ps_ref[...], 0.0) for c in cols]
    starts = [jnp.sum(p, axis=-1, keepdims=True) for p in picked]
    dest = jnp.zeros((tm, LANES), F32)
    for k in range(TOP_K):
        dest = jnp.where(out_lane == k, starts[k] + rank[:, k:k + 1], dest)
    dest_ref[...] = dest.astype(I32)


def _dest(eidx, rank, pstarts, tm):
    n = eidx.shape[0]
    tok = pl.BlockSpec((tm, LANES), lambda i: (i, 0))
    return pl.pallas_call(
        functools.partial(_dest_kernel, tm=tm), grid=(n // tm,),
        in_specs=[tok, tok, pl.BlockSpec((1, N_EXPERTS), lambda i: (0, 0))],
        out_specs=tok, out_shape=jax.ShapeDtypeStruct((n, LANES), I32),
        compiler_params=_cparams(1), name="dest")(eidx, rank, pstarts)


def _dispatch_kernel(clear_ref, dest_ref, hp_ref, hs_ref, xs_hbm, pbuf, zbuf, sems, zsem, *, tm, nb, np_tiles):
    i = pl.program_id(0)
    n = pl.num_programs(0)
    slot = i % 2

    def zero_copy(blk):
        return pltpu.make_async_copy(zbuf, xs_hbm.at[pl.ds(blk * ROW_BLOCK, ROW_BLOCK)], zsem)

    def wait_slot(s):
        for _ in range(TOP_K):
            _wait_rows(pbuf.at[s], sems.at[s])

    @pl.when(i == 0)
    def _():
        zbuf[...] = jnp.zeros_like(zbuf)

        def start(blk, carry):
            @pl.when(clear_ref[blk] > 0)
            def _():
                zero_copy(blk).start()
            return carry

        def wait(blk, carry):
            @pl.when(clear_ref[blk] > 0)
            def _():
                zero_copy(blk).wait()
            return carry
        lax.fori_loop(0, nb, start, 0)
        lax.fori_loop(0, nb, wait, 0)

    @pl.when(i >= 2)
    def _():
        wait_slot(slot)

    pbuf[slot] = jnp.where(i < np_tiles, hp_ref[...], hs_ref[...])

    def body(g, carry):
        for j in range(SUBLANES):
            for k in range(TOP_K):
                row = dest_ref[0, 0, (g * SUBLANES + j) * TOP_K + k]
                pltpu.make_async_copy(pbuf.at[slot, g, pl.ds(j, 1)], xs_hbm.at[pl.ds(row, 1)],
                                      sems.at[slot]).start(priority=k % 2)
        return carry
    lax.fori_loop(0, tm // SUBLANES, body, 0)

    @pl.when(i == n - 1)
    def _():
        wait_slot(slot)

    @pl.when(jnp.logical_and(i == n - 1, n > 1))
    def _():
        wait_slot(1 - slot)


def _dispatch(clear, dest3, h_p, h_s, tm):
    (n_p, d), n_s = h_p.shape, h_s.shape[0]
    nb = clear.shape[0]
    groups = tm // SUBLANES
    rows3 = lambda h: h.reshape(h.shape[0] // SUBLANES, SUBLANES, d)
    grid_spec = pltpu.PrefetchScalarGridSpec(
        num_scalar_prefetch=1, grid=((n_p + n_s) // tm,),
        in_specs=[pl.BlockSpec((1, 1, tm * TOP_K), lambda i, cl: (i, 0, 0), memory_space=pltpu.SMEM),
                  *_token_specs(n_p // SUBLANES, groups, (SUBLANES, d), (0, 0))],
        out_specs=pl.BlockSpec(memory_space=pl.ANY),
        scratch_shapes=[pltpu.VMEM((2, groups, SUBLANES, d), F32), pltpu.VMEM((ROW_BLOCK, d), F32),
                        pltpu.SemaphoreType.DMA((2,)), pltpu.SemaphoreType.DMA(())])
    return pl.pallas_call(
        functools.partial(_dispatch_kernel, tm=tm, nb=nb, np_tiles=n_p // tm), grid_spec=grid_spec,
        out_shape=jax.ShapeDtypeStruct((nb * ROW_BLOCK, d), F32),
        compiler_params=_cparams(1), name="dispatch")(clear, dest3, rows3(h_p), rows3(h_s))


def _experts_kernel(be_ref, nact_ref, x_ref, g_hbm, u_hbm, d_hbm, y_ref, gf, uf, df, gb, ub, db, sems, slot_ref,
                    *, base, nb):
    i = pl.program_id(0)
    nact = nact_ref[0]

    def weight_copies(e, s):
        return (pltpu.make_async_copy(g_hbm.at[base + e], gf.at[s], sems.at[s]),
                pltpu.make_async_copy(u_hbm.at[base + e], uf.at[s], sems.at[s]),
                pltpu.make_async_copy(d_hbm.at[base + e], df.at[s], sems.at[s]))

    @pl.when(jnp.logical_and(i == 0, nact > 0))
    def _():
        slot_ref[0] = 0
        for c in weight_copies(be_ref[0], 0):
            c.start()

    first = jnp.logical_or(i == 0, be_ref[i] != be_ref[jnp.maximum(i - 1, 0)])

    @pl.when(jnp.logical_and(i < nact, first))
    def _():
        s = slot_ref[0]
        here = be_ref[i]
        nxt = lax.while_loop(lambda j: jnp.logical_and(j < nact, be_ref[jnp.minimum(j, nb - 1)] == here),
                             lambda j: j + 1, i + 1)

        @pl.when(nxt < nact)
        def _():
            for c in weight_copies(be_ref[jnp.minimum(nxt, nb - 1)], 1 - s):
                c.start()
        for c in weight_copies(here, s):
            c.wait()
        gb[...] = gf[s].astype(BF16)
        ub[...] = uf[s].astype(BF16)
        db[...] = df[s].astype(BF16)
        slot_ref[0] = 1 - s

    @pl.when(i < nact)
    def _():
        x = x_ref[...].astype(BF16)
        hid = _silu(_dot(x, gb[...])) * _dot(x, ub[...])
        y_ref[...] = _dot(hid.astype(BF16), db[...])


def _experts(block_e, nact, x_sorted, exp_gate, exp_up, exp_down, layer):
    nb = x_sorted.shape[0] // ROW_BLOCK
    _, d, de = exp_gate.shape
    base = layer * N_EXPERTS
    rows = pl.BlockSpec((ROW_BLOCK, d), lambda i, be, na: (jnp.minimum(i, jnp.maximum(na[0] - 1, 0)), 0))
    grid_spec = pltpu.PrefetchScalarGridSpec(
        num_scalar_prefetch=2, grid=(nb,),
        in_specs=[rows, pl.BlockSpec(memory_space=pl.ANY), pl.BlockSpec(memory_space=pl.ANY),
                  pl.BlockSpec(memory_space=pl.ANY)],
        out_specs=rows,
        scratch_shapes=[pltpu.VMEM((2, d, de), F32), pltpu.VMEM((2, d, de), F32), pltpu.VMEM((2, de, d), F32),
                        pltpu.VMEM((d, de), BF16), pltpu.VMEM((d, de), BF16), pltpu.VMEM((de, d), BF16),
                        pltpu.SemaphoreType.DMA((2,)), pltpu.SMEM((1,), I32)])
    return pl.pallas_call(
        functools.partial(_experts_kernel, base=base, nb=nb), grid_spec=grid_spec,
        out_shape=jax.ShapeDtypeStruct((nb * ROW_BLOCK, d), F32),
        input_output_aliases={2: 0},
        compiler_params=_cparams(1), name="experts")(block_e, nact, x_sorted, exp_gate, exp_up, exp_down)


def _combine_kernel(cur_ref, nxt_ref, y_hbm, wts_ref, x1_ref, h_ref, gt_ref, sg_ref, su_ref, sd_ref,
                    nf_ref, x2_ref, yn_ref, ybuf, sems, *, tm):
    i = pl.program_id(0)
    n = pl.num_programs(0)
    slot = i % 2
    groups = tm // SUBLANES

    @pl.when(i == 0)
    def _():
        _gather_rows(cur_ref, y_hbm, ybuf.at[0], sems.at[0])

    @pl.when(i + 1 < n)
    def _():
        _gather_rows(nxt_ref, y_hbm, ybuf.at[1 - slot], sems.at[1 - slot])

    hb = h_ref[...].astype(BF16)
    hid = _silu(_dot(hb, sg_ref[...])) * _dot(hb, su_ref[...])
    acc = _dot(hid.astype(BF16), sd_ref[...])
    _wait_rows(ybuf.at[slot], sems.at[slot])
    wts = wts_ref[...]
    for k in range(TOP_K):
        y_k = ybuf[slot, k * groups:(k + 1) * groups].reshape(tm, acc.shape[1])
        acc = acc + y_k * wts[:, k:k + 1]
    x2 = x1_ref[...] + gt_ref[0] * acc
    x2_ref[...] = x2
    yn_ref[...] = _rms(x2, nf_ref[...])


def _combine(dest3, y_sorted, wts, x1, h2, mod3, sh_gate_b, sh_up_b, sh_down_b, norm_final, tm, tiles_per_batch):
    n, d = x1.shape
    nt = n // tm
    rows = mod3.shape[1]
    ds = sh_gate_b.shape[1]
    tok = lambda w: pl.BlockSpec((tm, w), lambda i: (i, 0))
    kern = functools.partial(_combine_kernel, tm=tm)
    return pl.pallas_call(
        kern, grid=(nt,),
        in_specs=[pl.BlockSpec((1, 1, tm * TOP_K), lambda i: (i, 0, 0), memory_space=pltpu.SMEM),
                  pl.BlockSpec((1, 1, tm * TOP_K), lambda i: (jnp.minimum(i + 1, nt - 1), 0, 0),
                               memory_space=pltpu.SMEM),
                  pl.BlockSpec(memory_space=pl.ANY),
                  tok(LANES), tok(d), tok(d),
                  _mod_spec(rows, d, tiles_per_batch, 5),
                  pl.BlockSpec((d, ds), lambda i: (0, 0)),
                  pl.BlockSpec((d, ds), lambda i: (0, 0)),
                  pl.BlockSpec((ds, d), lambda i: (0, 0)),
                  pl.BlockSpec((1, d), lambda i: (0, 0))],
        out_specs=[tok(d), tok(d)],
        out_shape=[jax.ShapeDtypeStruct((n, d), F32)] * 2,
        scratch_shapes=[pltpu.VMEM((2, tm * TOP_K // SUBLANES, SUBLANES, d), F32),
                        pltpu.SemaphoreType.DMA((2,))],
        compiler_params=_cparams(1), name="combine")(
            dest3, dest3, y_sorted, wts, x1, h2, mod3, sh_gate_b, sh_up_b, sh_down_b, norm_final)


def _last_rows(buf, new, keep):
    t = new.shape[1]
    if t >= keep:
        return new[:, t - keep:]
    return jnp.concatenate([buf[:, t:], new], axis=1)


def _mixers(u, qkv, z, ba, b, t, pool_buf, conv_buf, s0, start_pos, lw):
    t_pool = -(-t // 8) * 8
    t_gdn = -(-t // CHUNK) * CHUNK
    u3 = u.reshape(b, t, POOL_W)
    qkv3 = qkv.reshape(b, t, CONV_CH)
    pad_t = lambda a, tp: a if tp == t else jnp.pad(a, ((0, 0), (0, tp - t), (0, 0)))
    buf16 = jnp.pad(pool_buf, ((0, 0), (POOL_HALO - POOL_BUF, 0), (0, 0)))
    pool_out = _pool(pad_t(u3, t_pool), buf16, lw["pool_w"], lw["pool_scale"], start_pos)[:, :t]
    if t == 1:
        s0_all, layer = s0
        dn_out, s_new = _gdn_step(qkv, z, ba, conv_buf, s0_all, layer, lw["conv_w"], lw["par"], lw["dn_norm"])
    else:
        cbuf8 = jnp.pad(conv_buf, ((0, 0), (CONV_HALO - (CONV_W - 1), 0), (0, 0)))
        dn_out, s_new = _gdn(pad_t(qkv3, t_gdn), pad_t(z.reshape(b, t, DN_W), t_gdn),
                             pad_t(ba.reshape(b, t, BA_W), t_gdn), cbuf8, s0, lw["conv_w"], lw["par"],
                             lw["dn_norm"], t)
        dn_out = dn_out[:, :t].reshape(b * t, DN_W)
    new_pool = _last_rows(pool_buf, u3, POOL_BUF)
    new_conv = _last_rows(conv_buf, qkv3, CONV_W - 1)
    return pool_out.reshape(b * t, POOL_W), dn_out, new_pool, new_conv, s_new


def _block_tables(counts, n_tok):
    nk = n_tok * TOP_K
    nb = -(-(nk + N_EXPERTS * (ROW_BLOCK - 1)) // ROW_BLOCK)
    pcounts = (counts + ROW_BLOCK - 1) // ROW_BLOCK * ROW_BLOCK
    pends = jnp.cumsum(pcounts)
    pstarts = pends - pcounts
    first_row = jnp.arange(nb, dtype=I32)[:, None] * ROW_BLOCK
    block_e = jnp.minimum(jnp.sum((pends[None, :] <= first_row).astype(I32), axis=1), N_EXPERTS - 1)
    nact = (pends[-1] // ROW_BLOCK).astype(I32).reshape(1)
    blk = jnp.arange(nb, dtype=I32)
    next_e = jnp.concatenate([block_e[1:], jnp.full((1,), -1, I32)])
    clear = ((blk >= nact[0] - 1) | (next_e != block_e)).astype(I32)
    return pstarts.astype(F32).reshape(1, N_EXPERTS), block_e, nact, clear


def _dest_tiles(dest, tm, k_major):
    n = dest.shape[0]
    tiles = dest.reshape(n // tm, tm, TOP_K)
    if k_major:
        tiles = tiles.transpose(0, 2, 1)
    return tiles.reshape(n // tm, 1, tm * TOP_K)


def kernel(x_prompt, x_sample, state_pool, state_conv, state_delta, c_prompt, c_sample, norm_mix, norm_ffn, w_ada, b_ada, w_in, pool_w, pool_scale, conv_w, a_log, dt_bias, dn_norm, w_out, router_w, router_bias, exp_gate, exp_up, exp_down, sh_gate, sh_up, sh_down, norm_final):
    bp, tp, d = x_prompt.shape
    bs, ts, _ = x_sample.shape
    depth = w_ada.shape[0]
    past_len = PAST_LEN
    n_p, n_s = bp * tp, bs * ts
    n_tok = n_p + n_s
    tm_p = min(512, tp)
    tm_c = min(128, tp)
    tm_r = next(t for t in (128, 64, 32, 16, 8) if n_p % t == 0 and n_s % t == 0)
    assert tp % tm_p == 0 and ts == 1 and bs % 8 == 0
    n_exp = exp_gate.shape[1]
    gate_w = exp_gate.reshape((depth * n_exp,) + exp_gate.shape[2:])
    up_w = exp_up.reshape((depth * n_exp,) + exp_up.shape[2:])
    down_w = exp_down.reshape((depth * n_exp,) + exp_down.shape[2:])

    c_all = jnp.concatenate([c_prompt, c_sample], axis=0)
    m_rows = -(-c_all.shape[0] // 8) * 8
    mod_all = _ada(jnp.pad(c_all, ((0, m_rows - c_all.shape[0]), (0, 0))), w_ada, b_ada)

    c2 = POOL_W + CONV_CH + DN_W
    lane_pad = jnp.zeros((depth, d, LANES - DN_HEADS), F32)
    w_in_p = jnp.concatenate([w_in[:, :, :c2], w_in[:, :, c2:c2 + DN_HEADS], lane_pad,
                              w_in[:, :, c2 + DN_HEADS:], lane_pad], axis=-1).astype(BF16)
    head_pad = lambda a: jnp.pad(a, ((0, 0), (0, LANES - DN_HEADS)))
    par_all = jnp.stack([head_pad(a_log), head_pad(dt_bias)] + [jnp.zeros((depth, LANES), F32)] * 6, axis=1)
    conv_w8 = jnp.pad(conv_w, ((0, 0), (0, 8 - CONV_W), (0, 0)))

    xp = x_prompt.reshape(n_p, d)
    xs = x_sample.reshape(n_s, d)
    zero_pool = jnp.zeros((bp, POOL_BUF, POOL_W), F32)
    zero_conv = jnp.zeros((bp, CONV_W - 1, CONV_CH), F32)
    zero_state = jnp.zeros((bp, DN_HEADS, DN_D, DN_D), F32)
    outs = {k: [] for k in ("pool_p", "conv_p", "delta_p", "pool_s", "conv_s", "delta_s")}
    yp = ys = None
    for l in range(depth):
        lw = {"pool_w": pool_w[l].astype(BF16), "pool_scale": pool_scale[l].reshape(1, POOL_W),
              "conv_w": conv_w8[l], "par": par_all[l], "dn_norm": dn_norm[l].reshape(1, DN_D)}
        mod_p = mod_all[l, :bp].reshape(bp, 1, 6 * d)
        mod_s = mod_all[l, bp:bp + bs].reshape(1, bs, 6 * d)
        nw_mix = norm_mix[l].reshape(1, d)
        nw_ffn = norm_ffn[l].reshape(1, d)
        w_out_b = w_out[l].astype(BF16)

        up, qkvp, zp, bap = _norm_proj(xp, mod_p, nw_mix, w_in_p[l], tm_p, tp // tm_p)
        us, qkvs, zs, bas = _norm_proj(xs, mod_s, nw_mix, w_in_p[l], n_s, 1)
        pool_p, dn_p, npool_p, nconv_p, ns_p = _mixers(up, qkvp, zp, bap, bp, tp, zero_pool, zero_conv,
                                                       zero_state, 0, lw)
        pool_s, dn_s, npool_s, nconv_s, ns_s = _mixers(us, qkvs, zs, bas, bs, ts, state_pool[l], state_conv[l],
                                                       (state_delta, l), past_len, lw)
        for key, val in (("pool_p", npool_p), ("conv_p", nconv_p), ("delta_p", ns_p),
                         ("pool_s", npool_s), ("conv_s", nconv_s), ("delta_s", ns_s)):
            outs[key].append(val)

        x1p, h2p = _out_proj(pool_p, dn_p, xp, mod_p, nw_ffn, w_out_b, tm_p, tp // tm_p)
        x1s, h2s = _out_proj(pool_s, dn_s, xs, mod_s, nw_ffn, w_out_b, n_s, 1)

        rw_b, rb = router_w[l].astype(BF16), router_bias[l].reshape(1, N_EXPERTS)
        eidx_p, wts_p, rank_p, counts = _router(h2p, rw_b, rb, jnp.zeros((1, N_EXPERTS), F32))
        eidx_s, wts_s, rank_s, counts = _router(h2s, rw_b, rb, counts)
        pstarts, block_e, nact, clear = _block_tables(counts.reshape(-1).astype(I32), n_tok)
        dest = jnp.concatenate([_dest(eidx_p, rank_p, pstarts, tm_r)[:, :TOP_K],
                                _dest(eidx_s, rank_s, pstarts, tm_r)[:, :TOP_K]])
        x_sorted = _dispatch(clear, _dest_tiles(dest, tm_r, False), h2p, h2s, tm_r)
        y_sorted = _experts(block_e, nact, x_sorted, gate_w, up_w, down_w, l)

        shg, shu, shd = sh_gate[l].astype(BF16), sh_up[l].astype(BF16), sh_down[l].astype(BF16)
        nf = norm_final.reshape(1, d)
        xp, yp = _combine(_dest_tiles(dest[:n_p], tm_c, True), y_sorted, wts_p, x1p, h2p, mod_p, shg, shu, shd,
                          nf, tm_c, tp // tm_c)
        xs, ys = _combine(_dest_tiles(dest[n_p:], n_s, True), y_sorted, wts_s, x1s, h2s, mod_s, shg, shu, shd,
                          nf, n_s, 1)

    stack = lambda key: jnp.stack(outs[key])
    return (yp.reshape(bp, tp, d), ys.reshape(bs, ts, d), stack("pool_p"), stack("conv_p"), stack("delta_p"),
            stack("pool_s"), stack("conv_s"), stack("delta_s"))
```

```python
import functools

import jax
import jax.numpy as jnp
from jax import lax
from jax.experimental import pallas as pl
from jax.experimental.pallas import tpu as pltpu

F32, BF16, I32 = jnp.float32, jnp.bfloat16, jnp.int32
HI = lax.Precision.HIGHEST
EPS = 1e-6

LANES = 128
POOL_WINDOWS = (2, 4, 8, 16)
POOL_GW = 128
POOL_W = 512
POOL_BUF = 15
POOL_HALO = 16
POOL_TILE = 512
DN_HEADS = 4
DN_D = 128
DN_W = 512
CONV_W = 4
CONV_CH = 3 * DN_W
CONV_HALO = 8
CHUNK = 64
INV_BLOCK = 16
GDN_ROWS = 8
N_EXPERTS = 256
TOP_K = 8
N_GROUPS = 8
TOPK_GROUPS = 4
GROUP_SHIFT = 5
ROUTED_SCALE = 2.5
PAST_LEN = 16384
ROW_BLOCK = 256
BA_W = 2 * LANES
D_IN_PAD = POOL_W + CONV_CH + DN_W + BA_W
VMEM_LIMIT = 48 * 2 ** 20


def _cparams(n_axes):
    return pltpu.CompilerParams(dimension_semantics=("arbitrary",) * n_axes,
                                vmem_limit_bytes=VMEM_LIMIT)


def _sigmoid(x):
    return 1.0 / (1.0 + jnp.exp(-x))


def _silu(x):
    return x * _sigmoid(x)


def _softplus(x):
    return jnp.maximum(x, 0.0) + jnp.log(1.0 + jnp.exp(-jnp.abs(x)))


def _rms(x, w):
    return x * lax.rsqrt(jnp.mean(x * x, axis=-1, keepdims=True) + EPS) * w


def _dot(a, b, precision=None):
    return jnp.dot(a, b, preferred_element_type=F32, precision=precision)


def _dot_nt(a, b, precision=None):
    return lax.dot_general(a, b, (((1,), (1,)), ((), ())), preferred_element_type=F32,
                           precision=precision)


def _ada_kernel(c_ref, w_ref, b_ref, o_ref):
    o_ref[0] = _dot(_silu(c_ref[...]), w_ref[0], HI) + b_ref[0]


def _ada(c_all, w_ada, b_ada):
    depth, d, d6 = w_ada.shape
    m = c_all.shape[0]
    tn = 1024
    return pl.pallas_call(
        _ada_kernel, grid=(depth, d6 // tn),
        in_specs=[pl.BlockSpec((m, d), lambda l, j: (0, 0)),
                  pl.BlockSpec((1, d, tn), lambda l, j: (l, 0, j)),
                  pl.BlockSpec((1, 1, tn), lambda l, j: (l, 0, j))],
        out_specs=pl.BlockSpec((1, m, tn), lambda l, j: (l, 0, j)),
        out_shape=jax.ShapeDtypeStruct((depth, m, d6), F32),
        compiler_params=_cparams(2), name="ada")(c_all, w_ada, b_ada.reshape(depth, 1, d6))


def _mod_spec(rows, d, tiles_per_batch, chunk):
    return pl.BlockSpec((1, rows, d), lambda i: (i // tiles_per_batch, 0, chunk))


def _norm_proj_kernel(x_ref, sh_ref, sc_ref, nw_ref, w_ref, u_ref, qkv_ref, z_ref, ba_ref):
    h = _rms(x_ref[...], nw_ref[...]) * (1.0 + sc_ref[0]) + sh_ref[0]
    hb = h.astype(BF16)
    c0, c1, c2 = POOL_W, POOL_W + CONV_CH, POOL_W + CONV_CH + DN_W
    u_ref[...] = _dot(hb, w_ref[:, :c0])
    qkv_ref[...] = _dot(hb, w_ref[:, c0:c1])
    z_ref[...] = _dot(hb, w_ref[:, c1:c2])
    ba_ref[...] = _dot(hb, w_ref[:, c2:])


def _norm_proj(x2d, mod3, nw, w_in_p, tm, tiles_per_batch):
    n, d = x2d.shape
    rows = mod3.shape[1]
    widths = (POOL_W, CONV_CH, DN_W, BA_W)
    return pl.pallas_call(
        _norm_proj_kernel, grid=(n // tm,),
        in_specs=[pl.BlockSpec((tm, d), lambda i: (i, 0)),
                  _mod_spec(rows, d, tiles_per_batch, 0),
                  _mod_spec(rows, d, tiles_per_batch, 1),
                  pl.BlockSpec((1, d), lambda i: (0, 0)),
                  pl.BlockSpec((d, D_IN_PAD), lambda i: (0, 0))],
        out_specs=[pl.BlockSpec((tm, w), lambda i: (i, 0)) for w in widths],
        out_shape=[jax.ShapeDtypeStruct((n, w), F32) for w in widths],
        compiler_params=_cparams(1), name="norm_proj")(x2d, mod3, mod3, nw, w_in_p)


def _pool_kernel(u_ref, buf_ref, pw_ref, ps_ref, o_ref, ext, *, tt, gb, start_pos):
    t = pl.program_id(1)

    @pl.when(t == 0)
    def _():
        ext[:, 0:POOL_HALO, :] = buf_ref[...]

    @pl.when(t > 0)
    def _():
        ext[:, 0:POOL_HALO, :] = ext[:, tt:tt + POOL_HALO, :]

    ext[:, POOL_HALO:POOL_HALO + tt, :] = u_ref[...]
    pos = start_pos + t * tt + lax.broadcasted_iota(I32, (tt, 1), 0)
    for g, win in enumerate(POOL_WINDOWS):
        sl = slice(g * POOL_GW, (g + 1) * POOL_GW)
        cnt = jnp.minimum(pos + 1, win).astype(F32)
        pooled = []
        for rb in range(gb):
            u = u_ref[rb, :, sl]
            s = u
            for j in range(1, win):
                s = s + ext[rb, POOL_HALO - j:POOL_HALO - j + tt, sl]
            pooled.append(s / cnt - u)
        mixed = _dot(jnp.concatenate(pooled, axis=0).astype(BF16), pw_ref[g]) * ps_ref[:, sl]
        for rb in range(gb):
            o_ref[rb, :, sl] = mixed[rb * tt:(rb + 1) * tt]


def _pool(u3, buf16, pool_w_b, pool_scale, start_pos):
    b, t, w = u3.shape
    tt = min(t, POOL_TILE)
    assert t % tt == 0 and tt % 8 == 0
    gb = next(g for g in (16, 8, 4, 2, 1) if b % g == 0 and g * tt <= POOL_TILE)
    return pl.pallas_call(
        functools.partial(_pool_kernel, tt=tt, gb=gb, start_pos=start_pos), grid=(b // gb, t // tt),
        in_specs=[pl.BlockSpec((gb, tt, w), lambda i, j: (i, j, 0)),
                  pl.BlockSpec((gb, POOL_HALO, w), lambda i, j: (i, 0, 0)),
                  pl.BlockSpec((len(POOL_WINDOWS), POOL_GW, POOL_GW), lambda i, j: (0, 0, 0)),
                  pl.BlockSpec((1, w), lambda i, j: (0, 0))],
        out_specs=pl.BlockSpec((gb, tt, w), lambda i, j: (i, j, 0)),
        out_shape=jax.ShapeDtypeStruct((b, t, w), F32),
        scratch_shapes=[pltpu.VMEM((gb, POOL_HALO + tt, w), F32)],
        compiler_params=_cparams(2), name="pool")(u3, buf16, pool_w_b, pool_scale)


def _bf(x):
    return x.astype(BF16)


def _dot3(a, b):
    a_hi, b_hi = _bf(a), _bf(b)
    a_lo, b_lo = _bf(a - a_hi.astype(F32)), _bf(b - b_hi.astype(F32))
    return _dot(a_hi, b_hi) + (_dot(a_hi, b_lo) + _dot(a_lo, b_hi))


def _each(fn, *lists):
    return [fn(*args) for args in zip(*lists)]


def _unit_lower_inverse(a, row, col, eye):
    same = (row // INV_BLOCK) == (col // INV_BLOCK)
    dm = _each(lambda x: jnp.where(same, -x, 0.0), a)
    lm = _each(lambda x, d: -x - d, a, dm)
    mm = lambda x, y: _dot(_bf(x), _bf(y))
    d2 = _each(mm, dm, dm)
    d4 = _each(mm, d2, d2)
    d8 = _each(mm, d4, d4)
    p = _each(lambda x, y: mm(eye + x, eye + y), dm, d2)
    p2 = _each(lambda x, y: mm(eye + x, eye + y), d4, d8)
    p = _each(mm, p, p2)
    m = _each(mm, p, lm)
    m2 = _each(mm, m, m)
    q = _each(lambda x, y: mm(eye + x, eye + y), m, m2)
    t0 = _each(mm, q, p)
    at = _each(_dot3, a, t0)
    resid = _each(lambda t, x: (eye - t) - x, t0, at)
    return _each(lambda t, r: t + mm(t, r), t0, resid)


def _gdn_kernel(qkv_ref, z_ref, ba_ref, cbuf_ref, s0_ref, cw_ref, par_ref, nw_ref,
                o_ref, sfin_ref, state, xext, *, t_len, gb):
    c = pl.program_id(1)
    seqs = range(gb)
    heads = range(DN_HEADS)
    hc = DN_HEADS * CHUNK

    @pl.when(c == 0)
    def _():
        for rb in seqs:
            state[rb * DN_HEADS:(rb + 1) * DN_HEADS] = s0_ref[rb]
            xext[rb, 0:CONV_HALO, :] = cbuf_ref[rb]

    @pl.when(c > 0)
    def _():
        for rb in seqs:
            xext[rb, 0:CONV_HALO, :] = xext[rb, CHUNK:CHUNK + CONV_HALO, :]

    valid = (c * CHUNK + lax.broadcasted_iota(I32, (CHUNK, 1), 0)) < t_len
    vf = valid.astype(F32)
    head_lane = lax.broadcasted_iota(I32, (1, LANES), 1) < DN_HEADS
    decay = jnp.where(head_lane, -jnp.exp(par_ref[0:1, :]), 0.0)
    r64 = lax.broadcasted_iota(I32, (CHUNK, CHUNK), 0)
    c64 = lax.broadcasted_iota(I32, (CHUNK, CHUNK), 1)
    tri = (r64 >= c64).astype(F32)
    pick = (lax.broadcasted_iota(I32, (8, LANES), 0) == lax.broadcasted_iota(I32, (8, LANES), 1)).astype(F32)
    row = lax.broadcasted_iota(I32, (hc, hc), 0)
    col = lax.broadcasted_iota(I32, (hc, hc), 1)
    incl = ((row // CHUNK) == (col // CHUNK)) & (row >= col)
    strict = row > col
    eye = (row == col).astype(F32)
    base = CONV_HALO - (CONV_W - 1)

    def conv(rb):
        xext[rb, CONV_HALO:CONV_HALO + CHUNK, :] = qkv_ref[rb]
        acc = xext[rb, base:base + CHUNK, :] * cw_ref[0:1, :]
        for j in range(1, CONV_W):
            acc = acc + xext[rb, base + j:base + j + CHUNK, :] * cw_ref[j:j + 1, :]
        return _silu(acc)
    qkvc = [conv(rb) for rb in seqs]
    ba = [ba_ref[rb] for rb in seqs]
    beta_all = _each(lambda x: _sigmoid(x[:, :LANES]) * vf, ba)
    g_all = _each(lambda x: decay * _softplus(x[:, LANES:] + par_ref[1:2, :]) * vf, ba)
    gam_all = _each(lambda g: _dot(tri, g, HI), g_all)
    gam_rows = _each(lambda g: _dot_nt(pick, g, HI), gam_all)

    def unit(x):
        return x * lax.rsqrt(jnp.sum(x * x, axis=-1, keepdims=True) + EPS)
    q_s = _each(lambda x: jnp.concatenate(
        [unit(x[:, h * DN_D:(h + 1) * DN_D]) * (DN_D ** -0.5) * vf for h in heads], axis=0), qkvc)
    k_s = _each(lambda x: jnp.concatenate(
        [unit(x[:, DN_W + h * DN_D:DN_W + (h + 1) * DN_D]) * vf for h in heads], axis=0), qkvc)
    v_s = _each(lambda x: jnp.concatenate(
        [x[:, 2 * DN_W + h * DN_D:2 * DN_W + (h + 1) * DN_D] * vf for h in heads], axis=0), qkvc)
    beta = _each(lambda x: jnp.concatenate([x[:, h:h + 1] for h in heads], axis=0), beta_all)
    gam = _each(lambda x: jnp.concatenate([x[:, h:h + 1] for h in heads], axis=0), gam_all)
    gam_r = _each(lambda x: jnp.concatenate([x[h:h + 1, :] for h in heads], axis=1), gam_rows)

    dec = _each(lambda g, gr: jnp.where(incl, jnp.exp(jnp.where(incl, g - gr, 0.0)), 0.0), gam, gam_r)
    kb = _each(_bf, k_s)
    qb = _each(_bf, q_s)
    kk = _each(_dot_nt, kb, kb)
    qk = _each(_dot_nt, qb, kb)
    a_mat = _each(lambda b_, x, d_: b_ * x * jnp.where(strict, d_, 0.0), beta, kk, dec)
    qk = _each(lambda x, d_: _bf(x * d_), qk, dec)
    egam = _each(jnp.exp, gam)
    rhs = _each(lambda b_, e, k, v: jnp.concatenate([(b_ * e) * k, b_ * v], axis=-1),
                beta, egam, k_s, v_s)
    t_inv = _unit_lower_inverse(a_mat, row, col, eye)
    sol = _each(_dot3, t_inv, rhs)
    wb = _each(lambda x: _bf(x[:, :DN_D]), sol)
    uv = _each(lambda x: x[:, DN_D:], sol)

    hrows = [slice(h * CHUNK, (h + 1) * CHUNK) for h in heads]
    s_f = [[state[rb * DN_HEADS + h] for h in heads] for rb in seqs]
    s_b = [[_bf(x) for x in per_seq] for per_seq in s_f]
    ub = _each(lambda u_, w_, s_: _bf(jnp.concatenate(
        [u_[hrows[h]] - _dot(w_[hrows[h]], s_[h]) for h in heads], axis=0)), uv, wb, s_b)
    o_intra = _each(_dot, qk, ub)
    for h in heads:
        rows = hrows[h]
        sl = slice(h * DN_D, (h + 1) * DN_D)
        o = _each(lambda e, q_, s_, oi: e[rows] * _dot(q_[rows], s_[h]) + oi[rows], egam, qb, s_b, o_intra)
        g_last = _each(lambda g: g[(h + 1) * CHUNK - 1:(h + 1) * CHUNK, :], gam)
        k_dec = _each(lambda k, gl, g: _bf((k[rows] * jnp.exp(gl - g[rows])).T), k_s, g_last, gam)
        upd = _each(lambda kd, u_: _dot(kd, u_[rows]), k_dec, ub)
        for rb in seqs:
            state[rb * DN_HEADS + h] = jnp.exp(g_last[rb]) * s_f[rb][h] + upd[rb]
            o_ref[rb, :, sl] = _rms(o[rb], nw_ref[...]) * _silu(z_ref[rb, :, sl])

    @pl.when(c == pl.num_programs(1) - 1)
    def _():
        for rb in seqs:
            sfin_ref[rb] = state[rb * DN_HEADS:(rb + 1) * DN_HEADS]


def _gdn(qkv3, z3, ba3, cbuf8, s0, conv_w8, par, dn_norm, t_len):
    b, tp, _ = qkv3.shape
    assert tp % CHUNK == 0
    gb = GDN_ROWS if b % GDN_ROWS == 0 else 1
    kern = functools.partial(_gdn_kernel, t_len=t_len, gb=gb)
    return pl.pallas_call(
        kern, grid=(b // gb, tp // CHUNK),
        in_specs=[pl.BlockSpec((gb, CHUNK, CONV_CH), lambda i, j: (i, j, 0)),
                  pl.BlockSpec((gb, CHUNK, DN_W), lambda i, j: (i, j, 0)),
                  pl.BlockSpec((gb, CHUNK, BA_W), lambda i, j: (i, j, 0)),
                  pl.BlockSpec((gb, CONV_HALO, CONV_CH), lambda i, j: (i, 0, 0)),
                  pl.BlockSpec((gb, DN_HEADS, DN_D, DN_D), lambda i, j: (i, 0, 0, 0)),
                  pl.BlockSpec((8, CONV_CH), lambda i, j: (0, 0)),
                  pl.BlockSpec((8, LANES), lambda i, j: (0, 0)),
                  pl.BlockSpec((1, DN_D), lambda i, j: (0, 0))],
        out_specs=[pl.BlockSpec((gb, CHUNK, DN_W), lambda i, j: (i, j, 0)),
                   pl.BlockSpec((gb, DN_HEADS, DN_D, DN_D), lambda i, j: (i, 0, 0, 0))],
        out_shape=[jax.ShapeDtypeStruct((b, tp, DN_W), F32),
                   jax.ShapeDtypeStruct((b, DN_HEADS, DN_D, DN_D), F32)],
        scratch_shapes=[pltpu.VMEM((gb * DN_HEADS, DN_D, DN_D), F32),
                        pltpu.VMEM((gb, CONV_HALO + CHUNK, CONV_CH), F32)],
        compiler_params=_cparams(2), name="gdn")(qkv3, z3, ba3, cbuf8, s0, conv_w8, par, dn_norm)


STEP_TILE = 8


def _gdn_step_kernel(qkv_ref, z_ref, ba_ref, cst_ref, s0_all_ref, cw_ref, par_ref, nw_ref, o_ref, s_ref):
    s0_ref = s0_all_ref.at[0]
    acc = qkv_ref[...] * cw_ref[CONV_W - 1:CONV_W, :]
    for j in range(CONV_W - 1):
        acc = acc + cst_ref[:, j, :] * cw_ref[j:j + 1, :]
    qkvc = _silu(acc)
    ba = ba_ref[...]
    head_lane = lax.broadcasted_iota(I32, (1, LANES), 1) < DN_HEADS
    beta_all = _sigmoid(ba[:, :LANES])
    decay = jnp.where(head_lane, -jnp.exp(par_ref[0:1, :]), 0.0)
    eg_all = jnp.exp(decay * _softplus(ba[:, LANES:] + par_ref[1:2, :]))
    pad6 = jnp.zeros((6, DN_D), F32)
    pad7 = jnp.zeros((7, DN_D), F32)
    for h in range(DN_HEADS):
        q = qkvc[:, h * DN_D:(h + 1) * DN_D]
        k = qkvc[:, DN_W + h * DN_D:DN_W + (h + 1) * DN_D]
        v = qkvc[:, 2 * DN_W + h * DN_D:2 * DN_W + (h + 1) * DN_D]
        q = q * lax.rsqrt(jnp.sum(q * q, axis=-1, keepdims=True) + EPS) * (DN_D ** -0.5)
        k = k * lax.rsqrt(jnp.sum(k * k, axis=-1, keepdims=True) + EPS)
        qk = _dot_nt(_bf(q), _bf(k))
        seqs = list(range(STEP_TILE))
        beta = [beta_all[b:b + 1, h:h + 1] for b in seqs]
        eg = [eg_all[b:b + 1, h:h + 1] for b in seqs]
        s_old = [s0_ref[b, h] for b in seqs]
        ks_qs = _each(lambda b, s_: _dot(_bf(jnp.concatenate([k[b:b + 1], q[b:b + 1], pad6], axis=0)), _bf(s_)),
                      seqs, s_old)
        u = _each(lambda b, b_, e, r: b_ * (v[b:b + 1] - e * r[0:1]), seqs, beta, eg, ks_qs)
        o_rows = _each(lambda b, e, r, u_: e * r[1:2] + qk[b:b + 1, b:b + 1] * u_, seqs, eg, ks_qs, u)
        k_col = [_bf(jnp.concatenate([k[b:b + 1], pad7], axis=0).T) for b in seqs]
        upd = _each(lambda kc, u_: _dot(kc, _bf(jnp.concatenate([u_, pad7], axis=0))), k_col, u)
        for b in seqs:
            s_ref[b, h] = eg[b] * s_old[b] + upd[b]
        o = jnp.concatenate(o_rows, axis=0)
        sl = slice(h * DN_D, (h + 1) * DN_D)
        o_ref[:, sl] = _rms(o, nw_ref[...]) * _silu(z_ref[:, sl])


def _gdn_step(qkv2, z2, ba2, conv_state, s0_all, layer, conv_w8, par, dn_norm):
    b = qkv2.shape[0]
    assert b % STEP_TILE == 0
    row = lambda w: pl.BlockSpec((STEP_TILE, w), lambda i: (i, 0))
    st = pl.BlockSpec((STEP_TILE, DN_HEADS, DN_D, DN_D), lambda i: (i, 0, 0, 0))
    st_in = pl.BlockSpec((1, STEP_TILE, DN_HEADS, DN_D, DN_D), lambda i: (layer, i, 0, 0, 0))
    return pl.pallas_call(
        _gdn_step_kernel, grid=(b // STEP_TILE,),
        in_specs=[row(CONV_CH), row(DN_W), row(BA_W),
                  pl.BlockSpec((STEP_TILE, CONV_W - 1, CONV_CH), lambda i: (i, 0, 0)), st_in,
                  pl.BlockSpec((8, CONV_CH), lambda i: (0, 0)),
                  pl.BlockSpec((8, LANES), lambda i: (0, 0)),
                  pl.BlockSpec((1, DN_D), lambda i: (0, 0))],
        out_specs=[row(DN_W), st],
        out_shape=[jax.ShapeDtypeStruct((b, DN_W), F32),
                   jax.ShapeDtypeStruct((b, DN_HEADS, DN_D, DN_D), F32)],
        compiler_params=_cparams(1), name="gdn_step")(qkv2, z2, ba2, conv_state, s0_all, conv_w8, par, dn_norm)


def _out_proj_kernel(pool_ref, dn_ref, x_ref, gt_ref, sh_ref, sc_ref, nw_ref, w_ref, x1_ref, h2_ref):
    mix = _dot(pool_ref[...].astype(BF16), w_ref[:POOL_W, :]) + _dot(dn_ref[...].astype(BF16), w_ref[POOL_W:, :])
    x1 = x_ref[...] + gt_ref[0] * mix
    x1_ref[...] = x1
    h2_ref[...] = _rms(x1, nw_ref[...]) * (1.0 + sc_ref[0]) + sh_ref[0]


def _out_proj(pool2, dn2, x2d, mod3, nw, w_out_b, tm, tiles_per_batch):
    n, d = x2d.shape
    rows = mod3.shape[1]
    tok = lambda w: pl.BlockSpec((tm, w), lambda i: (i, 0))
    return pl.pallas_call(
        _out_proj_kernel, grid=(n // tm,),
        in_specs=[tok(POOL_W), tok(DN_W), tok(d),
                  _mod_spec(rows, d, tiles_per_batch, 2),
                  _mod_spec(rows, d, tiles_per_batch, 3),
                  _mod_spec(rows, d, tiles_per_batch, 4),
                  pl.BlockSpec((1, d), lambda i: (0, 0)),
                  pl.BlockSpec((POOL_W + DN_W, d), lambda i: (0, 0))],
        out_specs=[tok(d), tok(d)],
        out_shape=[jax.ShapeDtypeStruct((n, d), F32)] * 2,
        compiler_params=_cparams(1), name="out_proj")(pool2, dn2, x2d, mod3, mod3, mod3, nw, w_out_b)


def _router_kernel(h_ref, rw_ref, rb_ref, cnt0_ref, eidx_ref, wts_ref, rank_ref, cnt_ref, *more, tm, zero_blocks,
                   n_steps):
    i = pl.program_id(0)
    if zero_blocks:
        xs_hbm, run, zbuf, zsem = more
        per_step = -(-zero_blocks // n_steps)

        def zero_copy(blk):
            return pltpu.make_async_copy(zbuf, xs_hbm.at[pl.ds(blk * ROW_BLOCK, ROW_BLOCK)], zsem)

        def for_blocks(step, fn):
            for j in range(per_step):
                blk = step * per_step + j

                @pl.when(blk < zero_blocks)
                def _():
                    fn(zero_copy(blk))

        @pl.when(i == 0)
        def _():
            zbuf[...] = jnp.zeros_like(zbuf)

        @pl.when(i > 0)
        def _():
            for_blocks(i - 1, lambda c: c.wait())
        for_blocks(i, lambda c: c.start())

        @pl.when(i == n_steps - 1)
        def _():
            for_blocks(i, lambda c: c.wait())
    else:
        (run,) = more

    @pl.when(i == 0)
    def _():
        run[...] = cnt0_ref[...]

    s = _sigmoid(_dot(h_ref[...].astype(BF16), rw_ref[...]))
    sel = s + rb_ref[...]
    lane_i = lax.broadcasted_iota(I32, (tm, N_EXPERTS), 1)
    lane = lane_i.astype(F32)
    grp = lane_i >> GROUP_SHIFT
    neg = -jnp.inf

    def first_max(x):
        m = jnp.max(x, axis=-1, keepdims=True)
        idx = jnp.min(jnp.where(x == m, lane, float(N_EXPERTS)), axis=-1, keepdims=True)
        return m, idx

    gs = []
    for g in range(N_GROUPS):
        xg = jnp.where(grp == g, sel, neg)
        m1, i1 = first_max(xg)
        m2 = jnp.max(jnp.where(lane == i1, neg, xg), axis=-1, keepdims=True)
        gs.append(m1 + m2)
    cur = jnp.full((tm, N_EXPERTS), neg, F32)
    for g in range(N_GROUPS):
        ahead = jnp.zeros((tm, 1), F32)
        for o in range(N_GROUPS):
            if o < g:
                ahead = ahead + jnp.where(gs[o] >= gs[g], 1.0, 0.0)
            elif o > g:
                ahead = ahead + jnp.where(gs[o] > gs[g], 1.0, 0.0)
        cur = jnp.where((grp == g) & (ahead < TOPK_GROUPS), sel, cur)

    hits, idxs, ws = [], [], []
    chosen = jnp.zeros((tm, N_EXPERTS), F32)
    for _ in range(TOP_K):
        _, idx = first_max(cur)
        hit = lane == idx
        ws.append(jnp.sum(jnp.where(hit, s, 0.0), axis=-1, keepdims=True))
        cur = jnp.where(hit, neg, cur)
        chosen = jnp.where(hit, 1.0, chosen)
        hits.append(hit)
        idxs.append(idx)
    onehot = chosen
    r = lax.broadcasted_iota(I32, (tm, tm), 0)
    c = lax.broadcasted_iota(I32, (tm, tm), 1)
    earlier = jnp.where(r > c, 1.0, 0.0).astype(BF16)
    before = _dot(earlier, onehot.astype(BF16)) + run[...]
    run[...] = run[...] + jnp.sum(onehot, axis=0, keepdims=True)

    wsum = ws[0]
    for w in ws[1:]:
        wsum = wsum + w
    out_lane = lax.broadcasted_iota(I32, (tm, LANES), 1)
    eidx = jnp.zeros((tm, LANES), F32)
    wts = jnp.zeros((tm, LANES), F32)
    rank = jnp.zeros((tm, LANES), F32)
    for k in range(TOP_K):
        rk = jnp.sum(jnp.where(hits[k], before, 0.0), axis=-1, keepdims=True)
        eidx = jnp.where(out_lane == k, idxs[k], eidx)
        wts = jnp.where(out_lane == k, ws[k] / wsum * ROUTED_SCALE, wts)
        rank = jnp.where(out_lane == k, rk, rank)
    eidx_ref[...] = eidx.astype(I32)
    wts_ref[...] = wts
    rank_ref[...] = rank.astype(I32)

    @pl.when(i == pl.num_programs(0) - 1)
    def _():
        cnt_ref[...] = run[...]


def _token_specs(n_p, tm, d_block, index_rest):
    np_tiles = n_p // tm
    return (pl.BlockSpec((tm,) + d_block, lambda i, *_: (jnp.minimum(i, np_tiles - 1),) + index_rest),
            pl.BlockSpec((tm,) + d_block, lambda i, *_: (jnp.maximum(i - np_tiles, 0),) + index_rest))


def _router(h, router_w_b, router_bias, counts0, zero_blocks=0):
    n, d = h.shape
    tm = next(t for t in (512, 256, 128, 64, 32, 16, 8) if n % t == 0)
    tok = pl.BlockSpec((tm, LANES), lambda i: (i, 0))
    cnt = pl.BlockSpec((1, N_EXPERTS), lambda i: (0, 0))
    out_specs = [tok, tok, tok, cnt]
    out_shape = [jax.ShapeDtypeStruct((n, LANES), I32), jax.ShapeDtypeStruct((n, LANES), F32),
                 jax.ShapeDtypeStruct((n, LANES), I32), jax.ShapeDtypeStruct((1, N_EXPERTS), F32)]
    scratch = [pltpu.VMEM((1, N_EXPERTS), F32)]
    if zero_blocks:
        out_specs.append(pl.BlockSpec(memory_space=pl.ANY))
        out_shape.append(jax.ShapeDtypeStruct((zero_blocks * ROW_BLOCK, d), F32))
        scratch += [pltpu.VMEM((ROW_BLOCK, d), F32), pltpu.SemaphoreType.DMA(())]
    return pl.pallas_call(
        functools.partial(_router_kernel, tm=tm, zero_blocks=zero_blocks, n_steps=n // tm), grid=(n // tm,),
        in_specs=[pl.BlockSpec((tm, d), lambda i: (i, 0)),
                  pl.BlockSpec((d, N_EXPERTS), lambda i: (0, 0)), cnt, cnt],
        out_specs=out_specs, out_shape=out_shape, scratch_shapes=scratch,
        compiler_params=_cparams(1), name="router")(h, router_w_b, router_bias, counts0)


SUBLANES = 8


def _gather_rows(idx_ref, src_hbm, dst, sem):
    def body(g, carry):
        for j in range(SUBLANES):
            tok = idx_ref[0, 0, g * SUBLANES + j]
            pltpu.make_async_copy(src_hbm.at[pl.ds(tok, 1)], dst.at[g, pl.ds(j, 1)], sem).start(priority=j % 2)
        return carry
    lax.fori_loop(0, dst.shape[0], body, 0)


def _wait_rows(dst, sem):
    pltpu.make_async_copy(dst, dst, sem).wait()


def _dest_kernel(eidx_ref, rank_ref, ps_ref, dest_ref, *, tm):
    lane = lax.broadcasted_iota(I32, (tm, N_EXPERTS), 1)
    out_lane = lax.broadcasted_iota(I32, (tm, LANES), 1)
    eidx = eidx_ref[...]
    rank = rank_ref[...].astype(F32)
    cols = [eidx[:, k:k + 1] for k in range(TOP_K)]
    picked = [jnp.where(lane == c, ps_ref[...], 0.0) for c in cols]
    starts = [jnp.sum(p, axis=-1, keepdims=True) for p in picked]
    dest = jnp.zeros((tm, LANES), F32)
    for k in range(TOP_K):
        dest = jnp.where(out_lane == k, starts[k] + rank[:, k:k + 1], dest)
    dest_ref[...] = dest.astype(I32)


def _dest(eidx, rank, pstarts, tm):
    n = eidx.shape[0]
    tok = pl.BlockSpec((tm, LANES), lambda i: (i, 0))
    return pl.pallas_call(
        functools.partial(_dest_kernel, tm=tm), grid=(n // tm,),
        in_specs=[tok, tok, pl.BlockSpec((1, N_EXPERTS), lambda i: (0, 0))],
        out_specs=tok, out_shape=jax.ShapeDtypeStruct((n, LANES), I32),
        compiler_params=_cparams(1), name="dest")(eidx, rank, pstarts)


def _dispatch_kernel(dest_ref, hp_ref, hs_ref, xs_in, xs_hbm, pbuf, sems, *, tm, np_tiles):
    del xs_in
    i = pl.program_id(0)
    n = pl.num_programs(0)
    slot = i % 2

    def wait_slot(s):
        for _ in range(TOP_K):
            _wait_rows(pbuf.at[s], sems.at[s])

    @pl.when(i >= 2)
    def _():
        wait_slot(slot)

    pbuf[slot] = jnp.where(i < np_tiles, hp_ref[...], hs_ref[...])

    def body(g, carry):
        for j in range(SUBLANES):
            for k in range(TOP_K):
                row = dest_ref[0, 0, (g * SUBLANES + j) * TOP_K + k]
                pltpu.make_async_copy(pbuf.at[slot, g, pl.ds(j, 1)], xs_hbm.at[pl.ds(row, 1)],
                                      sems.at[slot]).start(priority=k % 2)
        return carry
    lax.fori_loop(0, tm // SUBLANES, body, 0)

    @pl.when(i == n - 1)
    def _():
        wait_slot(slot)

    @pl.when(jnp.logical_and(i == n - 1, n > 1))
    def _():
        wait_slot(1 - slot)


def _dispatch(dest3, h_p, h_s, x_zero, tm):
    (n_p, d), n_s = h_p.shape, h_s.shape[0]
    groups = tm // SUBLANES
    rows3 = lambda h: h.reshape(h.shape[0] // SUBLANES, SUBLANES, d)
    return pl.pallas_call(
        functools.partial(_dispatch_kernel, tm=tm, np_tiles=n_p // tm), grid=((n_p + n_s) // tm,),
        in_specs=[pl.BlockSpec((1, 1, tm * TOP_K), lambda i: (i, 0, 0), memory_space=pltpu.SMEM),
                  *_token_specs(n_p // SUBLANES, groups, (SUBLANES, d), (0, 0)),
                  pl.BlockSpec(memory_space=pl.ANY)],
        out_specs=pl.BlockSpec(memory_space=pl.ANY),
        out_shape=jax.ShapeDtypeStruct(x_zero.shape, F32),
        scratch_shapes=[pltpu.VMEM((2, groups, SUBLANES, d), F32), pltpu.SemaphoreType.DMA((2,))],
        input_output_aliases={3: 0},
        compiler_params=_cparams(1), name="dispatch")(dest3, rows3(h_p), rows3(h_s), x_zero)


def _experts_kernel(be_ref, nact_ref, x_ref, g_hbm, u_hbm, d_hbm, y_ref, gf, uf, df, gb, ub, db, sems, slot_ref,
                    *, base, nb):
    i = pl.program_id(0)
    nact = nact_ref[0]

    def weight_copies(e, s):
        return (pltpu.make_async_copy(g_hbm.at[base + e], gf.at[s], sems.at[s]),
                pltpu.make_async_copy(u_hbm.at[base + e], uf.at[s], sems.at[s]),
                pltpu.make_async_copy(d_hbm.at[base + e], df.at[s], sems.at[s]))

    @pl.when(jnp.logical_and(i == 0, nact > 0))
    def _():
        slot_ref[0] = 0
        for c in weight_copies(be_ref[0], 0):
            c.start()

    first = jnp.logical_or(i == 0, be_ref[i] != be_ref[jnp.maximum(i - 1, 0)])

    @pl.when(jnp.logical_and(i < nact, first))
    def _():
        s = slot_ref[0]
        here = be_ref[i]
        nxt = lax.while_loop(lambda j: jnp.logical_and(j < nact, be_ref[jnp.minimum(j, nb - 1)] == here),
                             lambda j: j + 1, i + 1)

        @pl.when(nxt < nact)
        def _():
            for c in weight_copies(be_ref[jnp.minimum(nxt, nb - 1)], 1 - s):
                c.start()
        for c in weight_copies(here, s):
            c.wait()
        gb[...] = gf[s].astype(BF16)
        ub[...] = uf[s].astype(BF16)
        db[...] = df[s].astype(BF16)
        slot_ref[0] = 1 - s

    @pl.when(i < nact)
    def _():
        x = x_ref[...].astype(BF16)
        hid = _silu(_dot(x, gb[...])) * _dot(x, ub[...])
        y_ref[...] = _dot(hid.astype(BF16), db[...])


def _experts(block_e, nact, x_sorted, exp_gate, exp_up, exp_down, layer):
    nb = x_sorted.shape[0] // ROW_BLOCK
    _, d, de = exp_gate.shape
    base = layer * N_EXPERTS
    rows = pl.BlockSpec((ROW_BLOCK, d), lambda i, be, na: (jnp.minimum(i, jnp.maximum(na[0] - 1, 0)), 0))
    grid_spec = pltpu.PrefetchScalarGridSpec(
        num_scalar_prefetch=2, grid=(nb,),
        in_specs=[rows, pl.BlockSpec(memory_space=pl.ANY), pl.BlockSpec(memory_space=pl.ANY),
                  pl.BlockSpec(memory_space=pl.ANY)],
        out_specs=rows,
        scratch_shapes=[pltpu.VMEM((2, d, de), F32), pltpu.VMEM((2, d, de), F32), pltpu.VMEM((2, de, d), F32),
                        pltpu.VMEM((d, de), BF16), pltpu.VMEM((d, de), BF16), pltpu.VMEM((de, d), BF16),
                        pltpu.SemaphoreType.DMA((2,)), pltpu.SMEM((1,), I32)])
    return pl.pallas_call(
        functools.partial(_experts_kernel, base=base, nb=nb), grid_spec=grid_spec,
        out_shape=jax.ShapeDtypeStruct((nb * ROW_BLOCK, d), F32),
        input_output_aliases={2: 0},
        compiler_params=_cparams(1), name="experts")(block_e, nact, x_sorted, exp_gate, exp_up, exp_down)


def _combine_kernel(cur_ref, nxt_ref, y_hbm, wts_ref, x1_ref, h_ref, gt_ref, sg_ref, su_ref, sd_ref,
                    nf_ref, x2_ref, yn_ref, ybuf, sems, *, tm):
    i = pl.program_id(0)
    n = pl.num_programs(0)
    slot = i % 2
    groups = tm // SUBLANES

    @pl.when(i == 0)
    def _():
        _gather_rows(cur_ref, y_hbm, ybuf.at[0], sems.at[0])

    @pl.when(i + 1 < n)
    def _():
        _gather_rows(nxt_ref, y_hbm, ybuf.at[1 - slot], sems.at[1 - slot])

    hb = h_ref[...].astype(BF16)
    hid = _silu(_dot(hb, sg_ref[...])) * _dot(hb, su_ref[...])
    acc = _dot(hid.astype(BF16), sd_ref[...])
    _wait_rows(ybuf.at[slot], sems.at[slot])
    wts = wts_ref[...]
    for k in range(TOP_K):
        y_k = ybuf[slot, k * groups:(k + 1) * groups].reshape(tm, acc.shape[1])
        acc = acc + y_k * wts[:, k:k + 1]
    x2 = x1_ref[...] + gt_ref[0] * acc
    x2_ref[...] = x2
    yn_ref[...] = _rms(x2, nf_ref[...])


def _combine(dest3, y_sorted, wts, x1, h2, mod3, sh_gate_b, sh_up_b, sh_down_b, norm_final, tm, tiles_per_batch):
    n, d = x1.shape
    nt = n // tm
    rows = mod3.shape[1]
    ds = sh_gate_b.shape[1]
    tok = lambda w: pl.BlockSpec((tm, w), lambda i: (i, 0))
    kern = functools.partial(_combine_kernel, tm=tm)
    return pl.pallas_call(
        kern, grid=(nt,),
        in_specs=[pl.BlockSpec((1, 1, tm * TOP_K), lambda i: (i, 0, 0), memory_space=pltpu.SMEM),
                  pl.BlockSpec((1, 1, tm * TOP_K), lambda i: (jnp.minimum(i + 1, nt - 1), 0, 0),
                               memory_space=pltpu.SMEM),
                  pl.BlockSpec(memory_space=pl.ANY),
                  tok(LANES), tok(d), tok(d),
                  _mod_spec(rows, d, tiles_per_batch, 5),
                  pl.BlockSpec((d, ds), lambda i: (0, 0)),
                  pl.BlockSpec((d, ds), lambda i: (0, 0)),
                  pl.BlockSpec((ds, d), lambda i: (0, 0)),
                  pl.BlockSpec((1, d), lambda i: (0, 0))],
        out_specs=[tok(d), tok(d)],
        out_shape=[jax.ShapeDtypeStruct((n, d), F32)] * 2,
        scratch_shapes=[pltpu.VMEM((2, tm * TOP_K // SUBLANES, SUBLANES, d), F32),
                        pltpu.SemaphoreType.DMA((2,))],
        compiler_params=_cparams(1), name="combine")(
            dest3, dest3, y_sorted, wts, x1, h2, mod3, sh_gate_b, sh_up_b, sh_down_b, norm_final)


def _last_rows(buf, new, keep):
    t = new.shape[1]
    if t >= keep:
        return new[:, t - keep:]
    return jnp.concatenate([buf[:, t:], new], axis=1)


def _mixers(u, qkv, z, ba, b, t, pool_buf, conv_buf, s0, start_pos, lw):
    t_pool = -(-t // 8) * 8
    t_gdn = -(-t // CHUNK) * CHUNK
    u3 = u.reshape(b, t, POOL_W)
    qkv3 = qkv.reshape(b, t, CONV_CH)
    pad_t = lambda a, tp: a if tp == t else jnp.pad(a, ((0, 0), (0, tp - t), (0, 0)))
    buf16 = jnp.pad(pool_buf, ((0, 0), (POOL_HALO - POOL_BUF, 0), (0, 0)))
    pool_out = _pool(pad_t(u3, t_pool), buf16, lw["pool_w"], lw["pool_scale"], start_pos)[:, :t]
    if t == 1:
        s0_all, layer = s0
        dn_out, s_new = _gdn_step(qkv, z, ba, conv_buf, s0_all, layer, lw["conv_w"], lw["par"], lw["dn_norm"])
    else:
        cbuf8 = jnp.pad(conv_buf, ((0, 0), (CONV_HALO - (CONV_W - 1), 0), (0, 0)))
        dn_out, s_new = _gdn(pad_t(qkv3, t_gdn), pad_t(z.reshape(b, t, DN_W), t_gdn),
                             pad_t(ba.reshape(b, t, BA_W), t_gdn), cbuf8, s0, lw["conv_w"], lw["par"],
                             lw["dn_norm"], t)
        dn_out = dn_out[:, :t].reshape(b * t, DN_W)
    new_pool = _last_rows(pool_buf, u3, POOL_BUF)
    new_conv = _last_rows(conv_buf, qkv3, CONV_W - 1)
    return pool_out.reshape(b * t, POOL_W), dn_out, new_pool, new_conv, s_new


def _block_tables(counts, n_tok):
    nb = _n_row_blocks(n_tok)
    pcounts = (counts + ROW_BLOCK - 1) // ROW_BLOCK * ROW_BLOCK
    pends = jnp.cumsum(pcounts)
    pstarts = pends - pcounts
    first_row = jnp.arange(nb, dtype=I32)[:, None] * ROW_BLOCK
    block_e = jnp.minimum(jnp.sum((pends[None, :] <= first_row).astype(I32), axis=1), N_EXPERTS - 1)
    nact = (pends[-1] // ROW_BLOCK).astype(I32).reshape(1)
    return pstarts.astype(F32).reshape(1, N_EXPERTS), block_e, nact


def _n_row_blocks(n_tok):
    return -(-(n_tok * TOP_K + N_EXPERTS * (ROW_BLOCK - 1)) // ROW_BLOCK)


def _dest_tiles(dest, tm, k_major):
    n = dest.shape[0]
    tiles = dest.reshape(n // tm, tm, TOP_K)
    if k_major:
        tiles = tiles.transpose(0, 2, 1)
    return tiles.reshape(n // tm, 1, tm * TOP_K)


def kernel(x_prompt, x_sample, state_pool, state_conv, state_delta, c_prompt, c_sample, norm_mix, norm_ffn, w_ada, b_ada, w_in, pool_w, pool_scale, conv_w, a_log, dt_bias, dn_norm, w_out, router_w, router_bias, exp_gate, exp_up, exp_down, sh_gate, sh_up, sh_down, norm_final):
    bp, tp, d = x_prompt.shape
    bs, ts, _ = x_sample.shape
    depth = w_ada.shape[0]
    past_len = PAST_LEN
    n_p, n_s = bp * tp, bs * ts
    n_tok = n_p + n_s
    tm_p = min(512, tp)
    tm_c = min(128, tp)
    tm_r = next(t for t in (128, 64, 32, 16, 8) if n_p % t == 0 and n_s % t == 0)
    assert tp % tm_p == 0 and ts == 1 and bs % 8 == 0
    n_exp = exp_gate.shape[1]
    gate_w = exp_gate.reshape((depth * n_exp,) + exp_gate.shape[2:])
    up_w = exp_up.reshape((depth * n_exp,) + exp_up.shape[2:])
    down_w = exp_down.reshape((depth * n_exp,) + exp_down.shape[2:])

    c_all = jnp.concatenate([c_prompt, c_sample], axis=0)
    m_rows = -(-c_all.shape[0] // 8) * 8
    mod_all = _ada(jnp.pad(c_all, ((0, m_rows - c_all.shape[0]), (0, 0))), w_ada, b_ada)

    c2 = POOL_W + CONV_CH + DN_W
    lane_pad = jnp.zeros((depth, d, LANES - DN_HEADS), F32)
    w_in_p = jnp.concatenate([w_in[:, :, :c2], w_in[:, :, c2:c2 + DN_HEADS], lane_pad,
                              w_in[:, :, c2 + DN_HEADS:], lane_pad], axis=-1).astype(BF16)
    head_pad = lambda a: jnp.pad(a, ((0, 0), (0, LANES - DN_HEADS)))
    par_all = jnp.stack([head_pad(a_log), head_pad(dt_bias)] + [jnp.zeros((depth, LANES), F32)] * 6, axis=1)
    conv_w8 = jnp.pad(conv_w, ((0, 0), (0, 8 - CONV_W), (0, 0)))

    xp = x_prompt.reshape(n_p, d)
    xs = x_sample.reshape(n_s, d)
    zero_pool = jnp.zeros((bp, POOL_BUF, POOL_W), F32)
    zero_conv = jnp.zeros((bp, CONV_W - 1, CONV_CH), F32)
    zero_state = jnp.zeros((bp, DN_HEADS, DN_D, DN_D), F32)
    outs = {k: [] for k in ("pool_p", "conv_p", "delta_p", "pool_s", "conv_s", "delta_s")}
    yp = ys = None
    for l in range(depth):
        lw = {"pool_w": pool_w[l].astype(BF16), "pool_scale": pool_scale[l].reshape(1, POOL_W),
              "conv_w": conv_w8[l], "par": par_all[l], "dn_norm": dn_norm[l].reshape(1, DN_D)}
        mod_p = mod_all[l, :bp].reshape(bp, 1, 6 * d)
        mod_s = mod_all[l, bp:bp + bs].reshape(1, bs, 6 * d)
        nw_mix = norm_mix[l].reshape(1, d)
        nw_ffn = norm_ffn[l].reshape(1, d)
        w_out_b = w_out[l].astype(BF16)

        up, qkvp, zp, bap = _norm_proj(xp, mod_p, nw_mix, w_in_p[l], tm_p, tp // tm_p)
        us, qkvs, zs, bas = _norm_proj(xs, mod_s, nw_mix, w_in_p[l], n_s, 1)
        pool_p, dn_p, npool_p, nconv_p, ns_p = _mixers(up, qkvp, zp, bap, bp, tp, zero_pool, zero_conv,
                                                       zero_state, 0, lw)
        pool_s, dn_s, npool_s, nconv_s, ns_s = _mixers(us, qkvs, zs, bas, bs, ts, state_pool[l], state_conv[l],
                                                       (state_delta, l), past_len, lw)
        for key, val in (("pool_p", npool_p), ("conv_p", nconv_p), ("delta_p", ns_p),
                         ("pool_s", npool_s), ("conv_s", nconv_s), ("delta_s", ns_s)):
            outs[key].append(val)

        x1p, h2p = _out_proj(pool_p, dn_p, xp, mod_p, nw_ffn, w_out_b, tm_p, tp // tm_p)
        x1s, h2s = _out_proj(pool_s, dn_s, xs, mod_s, nw_ffn, w_out_b, n_s, 1)

        rw_b, rb = router_w[l].astype(BF16), router_bias[l].reshape(1, N_EXPERTS)
        eidx_p, wts_p, rank_p, counts, x_zero = _router(h2p, rw_b, rb, jnp.zeros((1, N_EXPERTS), F32),
                                                        zero_blocks=_n_row_blocks(n_tok))
        eidx_s, wts_s, rank_s, counts = _router(h2s, rw_b, rb, counts)
        pstarts, block_e, nact = _block_tables(counts.reshape(-1).astype(I32), n_tok)
        dest = jnp.concatenate([_dest(eidx_p, rank_p, pstarts, tm_r)[:, :TOP_K],
                                _dest(eidx_s, rank_s, pstarts, tm_r)[:, :TOP_K]])
        x_sorted = _dispatch(_dest_tiles(dest, tm_r, False), h2p, h2s, x_zero, tm_r)
        y_sorted = _experts(block_e, nact, x_sorted, gate_w, up_w, down_w, l)

        shg, shu, shd = sh_gate[l].astype(BF16), sh_up[l].astype(BF16), sh_down[l].astype(BF16)
        nf = norm_final.reshape(1, d)
        xp, yp = _combine(_dest_tiles(dest[:n_p], tm_c, True), y_sorted, wts_p, x1p, h2p, mod_p, shg, shu, shd,
                          nf, tm_c, tp // tm_c)
        xs, ys = _combine(_dest_tiles(dest[n_p:], n_s, True), y_sorted, wts_s, x1s, h2s, mod_s, shg, shu, shd,
                          nf, n_s, 1)

    stack = lambda key: jnp.stack(outs[key])
    return (yp.reshape(bp, tp, d), ys.reshape(bs, ts, d), stack("pool_p"), stack("conv_p"), stack("delta_p"),
            stack("pool_s"), stack("conv_s"), stack("delta_s"))
```

```python
import functools

import jax
import jax.numpy as jnp
from jax import lax
from jax.experimental import pallas as pl
from jax.experimental.pallas import tpu as pltpu

F32, BF16, I32 = jnp.float32, jnp.bfloat16, jnp.int32
HI = lax.Precision.HIGHEST
EPS = 1e-6

LANES = 128
POOL_WINDOWS = (2, 4, 8, 16)
POOL_GW = 128
POOL_W = 512
POOL_BUF = 15
POOL_HALO = 16
POOL_TILE = 512
DN_HEADS = 4
DN_D = 128
DN_W = 512
CONV_W = 4
CONV_CH = 3 * DN_W
CONV_HALO = 8
CHUNK = 64
INV_BLOCK = 16
GDN_ROWS = 8
N_EXPERTS = 256
TOP_K = 8
N_GROUPS = 8
TOPK_GROUPS = 4
GROUP_SHIFT = 5
ROUTED_SCALE = 2.5
PAST_LEN = 16384
ROW_BLOCK = 256
BA_W = 2 * LANES
D_IN_PAD = POOL_W + CONV_CH + DN_W + BA_W
VMEM_LIMIT = 48 * 2 ** 20


def _cparams(n_axes):
    return pltpu.CompilerParams(dimension_semantics=("arbitrary",) * n_axes,
                                vmem_limit_bytes=VMEM_LIMIT)


def _sigmoid(x):
    return 1.0 / (1.0 + jnp.exp(-x))


def _silu(x):
    return x * _sigmoid(x)


def _softplus(x):
    return jnp.maximum(x, 0.0) + jnp.log(1.0 + jnp.exp(-jnp.abs(x)))


def _rms(x, w):
    return x * lax.rsqrt(jnp.mean(x * x, axis=-1, keepdims=True) + EPS) * w


def _dot(a, b, precision=None):
    return jnp.dot(a, b, preferred_element_type=F32, precision=precision)


def _dot_nt(a, b, precision=None):
    return lax.dot_general(a, b, (((1,), (1,)), ((), ())), preferred_element_type=F32,
                           precision=precision)


def _ada_kernel(c_ref, w_ref, b_ref, o_ref):
    o_ref[0] = _dot(_silu(c_ref[...]), w_ref[0], HI) + b_ref[0]


def _ada(c_all, w_ada, b_ada):
    depth, d, d6 = w_ada.shape
    m = c_all.shape[0]
    tn = 1024
    return pl.pallas_call(
        _ada_kernel, grid=(depth, d6 // tn),
        in_specs=[pl.BlockSpec((m, d), lambda l, j: (0, 0)),
                  pl.BlockSpec((1, d, tn), lambda l, j: (l, 0, j)),
                  pl.BlockSpec((1, 1, tn), lambda l, j: (l, 0, j))],
        out_specs=pl.BlockSpec((1, m, tn), lambda l, j: (l, 0, j)),
        out_shape=jax.ShapeDtypeStruct((depth, m, d6), F32),
        compiler_params=_cparams(2), name="ada")(c_all, w_ada, b_ada.reshape(depth, 1, d6))


def _mod_spec(rows, d, tiles_per_batch, chunk):
    return pl.BlockSpec((1, rows, d), lambda i: (i // tiles_per_batch, 0, chunk))


def _norm_proj_kernel(x_ref, sh_ref, sc_ref, nw_ref, w_ref, u_ref, qkv_ref, z_ref, ba_ref):
    h = _rms(x_ref[...], nw_ref[...]) * (1.0 + sc_ref[0]) + sh_ref[0]
    hb = h.astype(BF16)
    c0, c1, c2 = POOL_W, POOL_W + CONV_CH, POOL_W + CONV_CH + DN_W
    u_ref[...] = _dot(hb, w_ref[:, :c0])
    qkv_ref[...] = _dot(hb, w_ref[:, c0:c1])
    z_ref[...] = _dot(hb, w_ref[:, c1:c2])
    ba_ref[...] = _dot(hb, w_ref[:, c2:])


def _norm_proj(x2d, mod3, nw, w_in_p, tm, tiles_per_batch):
    n, d = x2d.shape
    rows = mod3.shape[1]
    widths = (POOL_W, CONV_CH, DN_W, BA_W)
    return pl.pallas_call(
        _norm_proj_kernel, grid=(n // tm,),
        in_specs=[pl.BlockSpec((tm, d), lambda i: (i, 0)),
                  _mod_spec(rows, d, tiles_per_batch, 0),
                  _mod_spec(rows, d, tiles_per_batch, 1),
                  pl.BlockSpec((1, d), lambda i: (0, 0)),
                  pl.BlockSpec((d, D_IN_PAD), lambda i: (0, 0))],
        out_specs=[pl.BlockSpec((tm, w), lambda i: (i, 0)) for w in widths],
        out_shape=[jax.ShapeDtypeStruct((n, w), F32) for w in widths],
        compiler_params=_cparams(1), name="norm_proj")(x2d, mod3, mod3, nw, w_in_p)


def _pool_kernel(u_ref, buf_ref, pw_ref, ps_ref, o_ref, ext, *, tt, gb, start_pos):
    t = pl.program_id(1)

    @pl.when(t == 0)
    def _():
        ext[:, 0:POOL_HALO, :] = buf_ref[...]

    @pl.when(t > 0)
    def _():
        ext[:, 0:POOL_HALO, :] = ext[:, tt:tt + POOL_HALO, :]

    ext[:, POOL_HALO:POOL_HALO + tt, :] = u_ref[...]
    pos = start_pos + t * tt + lax.broadcasted_iota(I32, (tt, 1), 0)
    for g, win in enumerate(POOL_WINDOWS):
        sl = slice(g * POOL_GW, (g + 1) * POOL_GW)
        cnt = jnp.minimum(pos + 1, win).astype(F32)
        pooled = []
        for rb in range(gb):
            u = u_ref[rb, :, sl]
            s = u
            for j in range(1, win):
                s = s + ext[rb, POOL_HALO - j:POOL_HALO - j + tt, sl]
            pooled.append(s / cnt - u)
        mixed = _dot(jnp.concatenate(pooled, axis=0).astype(BF16), pw_ref[g]) * ps_ref[:, sl]
        for rb in range(gb):
            o_ref[rb, :, sl] = mixed[rb * tt:(rb + 1) * tt]


def _pool(u3, buf16, pool_w_b, pool_scale, start_pos):
    b, t, w = u3.shape
    tt = min(t, POOL_TILE)
    assert t % tt == 0 and tt % 8 == 0
    gb = next(g for g in (16, 8, 4, 2, 1) if b % g == 0 and g * tt <= POOL_TILE)
    return pl.pallas_call(
        functools.partial(_pool_kernel, tt=tt, gb=gb, start_pos=start_pos), grid=(b // gb, t // tt),
        in_specs=[pl.BlockSpec((gb, tt, w), lambda i, j: (i, j, 0)),
                  pl.BlockSpec((gb, POOL_HALO, w), lambda i, j: (i, 0, 0)),
                  pl.BlockSpec((len(POOL_WINDOWS), POOL_GW, POOL_GW), lambda i, j: (0, 0, 0)),
                  pl.BlockSpec((1, w), lambda i, j: (0, 0))],
        out_specs=pl.BlockSpec((gb, tt, w), lambda i, j: (i, j, 0)),
        out_shape=jax.ShapeDtypeStruct((b, t, w), F32),
        scratch_shapes=[pltpu.VMEM((gb, POOL_HALO + tt, w), F32)],
        compiler_params=_cparams(2), name="pool")(u3, buf16, pool_w_b, pool_scale)


def _zero_fill(step, n_steps, zero_blocks, xs_hbm, zbuf, zsem):
    per_step = -(-zero_blocks // n_steps)

    def zero_copy(blk):
        return pltpu.make_async_copy(zbuf, xs_hbm.at[pl.ds(blk * ROW_BLOCK, ROW_BLOCK)], zsem)

    def for_blocks(s, fn):
        for j in range(per_step):
            blk = s * per_step + j

            @pl.when(blk < zero_blocks)
            def _():
                fn(zero_copy(blk))

    @pl.when(step == 0)
    def _():
        zbuf[...] = jnp.zeros_like(zbuf)

    @pl.when(step > 0)
    def _():
        for_blocks(step - 1, lambda c: c.wait())
    for_blocks(step, lambda c: c.start())

    @pl.when(step == n_steps - 1)
    def _():
        for_blocks(step, lambda c: c.wait())


def _bf(x):
    return x.astype(BF16)


def _dot3(a, b):
    a_hi, b_hi = _bf(a), _bf(b)
    a_lo, b_lo = _bf(a - a_hi.astype(F32)), _bf(b - b_hi.astype(F32))
    return _dot(a_hi, b_hi) + (_dot(a_hi, b_lo) + _dot(a_lo, b_hi))


def _each(fn, *lists):
    return [fn(*args) for args in zip(*lists)]


def _unit_lower_inverse(a, row, col, eye):
    same = (row // INV_BLOCK) == (col // INV_BLOCK)
    dm = _each(lambda x: jnp.where(same, -x, 0.0), a)
    lm = _each(lambda x, d: -x - d, a, dm)
    mm = lambda x, y: _dot(_bf(x), _bf(y))
    d2 = _each(mm, dm, dm)
    d4 = _each(mm, d2, d2)
    d8 = _each(mm, d4, d4)
    p = _each(lambda x, y: mm(eye + x, eye + y), dm, d2)
    p2 = _each(lambda x, y: mm(eye + x, eye + y), d4, d8)
    p = _each(mm, p, p2)
    m = _each(mm, p, lm)
    m2 = _each(mm, m, m)
    q = _each(lambda x, y: mm(eye + x, eye + y), m, m2)
    t0 = _each(mm, q, p)
    at = _each(_dot3, a, t0)
    resid = _each(lambda t, x: (eye - t) - x, t0, at)
    return _each(lambda t, r: t + mm(t, r), t0, resid)


def _gdn_kernel(qkv_ref, z_ref, ba_ref, cbuf_ref, s0_ref, cw_ref, par_ref, nw_ref,
                o_ref, sfin_ref, *more, t_len, gb, zero_blocks, grid):
    c = pl.program_id(1)
    if zero_blocks:
        xs_hbm, state, xext, zbuf, zsem = more
        _zero_fill(pl.program_id(0) * grid[1] + c, grid[0] * grid[1], zero_blocks, xs_hbm, zbuf, zsem)
    else:
        state, xext = more
    seqs = range(gb)
    heads = range(DN_HEADS)
    hc = DN_HEADS * CHUNK

    @pl.when(c == 0)
    def _():
        for rb in seqs:
            state[rb * DN_HEADS:(rb + 1) * DN_HEADS] = s0_ref[rb]
            xext[rb, 0:CONV_HALO, :] = cbuf_ref[rb]

    @pl.when(c > 0)
    def _():
        for rb in seqs:
            xext[rb, 0:CONV_HALO, :] = xext[rb, CHUNK:CHUNK + CONV_HALO, :]

    valid = (c * CHUNK + lax.broadcasted_iota(I32, (CHUNK, 1), 0)) < t_len
    vf = valid.astype(F32)
    head_lane = lax.broadcasted_iota(I32, (1, LANES), 1) < DN_HEADS
    decay = jnp.where(head_lane, -jnp.exp(par_ref[0:1, :]), 0.0)
    r64 = lax.broadcasted_iota(I32, (CHUNK, CHUNK), 0)
    c64 = lax.broadcasted_iota(I32, (CHUNK, CHUNK), 1)
    tri = (r64 >= c64).astype(F32)
    pick = (lax.broadcasted_iota(I32, (8, LANES), 0) == lax.broadcasted_iota(I32, (8, LANES), 1)).astype(F32)
    row = lax.broadcasted_iota(I32, (hc, hc), 0)
    col = lax.broadcasted_iota(I32, (hc, hc), 1)
    incl = ((row // CHUNK) == (col // CHUNK)) & (row >= col)
    strict = row > col
    eye = (row == col).astype(F32)
    base = CONV_HALO - (CONV_W - 1)

    def conv(rb):
        xext[rb, CONV_HALO:CONV_HALO + CHUNK, :] = qkv_ref[rb]
        acc = xext[rb, base:base + CHUNK, :] * cw_ref[0:1, :]
        for j in range(1, CONV_W):
            acc = acc + xext[rb, base + j:base + j + CHUNK, :] * cw_ref[j:j + 1, :]
        return _silu(acc)
    qkvc = [conv(rb) for rb in seqs]
    ba = [ba_ref[rb] for rb in seqs]
    beta_all = _each(lambda x: _sigmoid(x[:, :LANES]) * vf, ba)
    g_all = _each(lambda x: decay * _softplus(x[:, LANES:] + par_ref[1:2, :]) * vf, ba)
    gam_all = _each(lambda g: _dot(tri, g, HI), g_all)
    gam_rows = _each(lambda g: _dot_nt(pick, g, HI), gam_all)

    def unit(x):
        return x * lax.rsqrt(jnp.sum(x * x, axis=-1, keepdims=True) + EPS)
    q_s = _each(lambda x: jnp.concatenate(
        [unit(x[:, h * DN_D:(h + 1) * DN_D]) * (DN_D ** -0.5) * vf for h in heads], axis=0), qkvc)
    k_s = _each(lambda x: jnp.concatenate(
        [unit(x[:, DN_W + h * DN_D:DN_W + (h + 1) * DN_D]) * vf for h in heads], axis=0), qkvc)
    v_s = _each(lambda x: jnp.concatenate(
        [x[:, 2 * DN_W + h * DN_D:2 * DN_W + (h + 1) * DN_D] * vf for h in heads], axis=0), qkvc)
    beta = _each(lambda x: jnp.concatenate([x[:, h:h + 1] for h in heads], axis=0), beta_all)
    gam = _each(lambda x: jnp.concatenate([x[:, h:h + 1] for h in heads], axis=0), gam_all)
    gam_r = _each(lambda x: jnp.concatenate([x[h:h + 1, :] for h in heads], axis=1), gam_rows)

    dec = _each(lambda g, gr: jnp.where(incl, jnp.exp(jnp.where(incl, g - gr, 0.0)), 0.0), gam, gam_r)
    kb = _each(_bf, k_s)
    qb = _each(_bf, q_s)
    kk = _each(_dot_nt, kb, kb)
    qk = _each(_dot_nt, qb, kb)
    a_mat = _each(lambda b_, x, d_: b_ * x * jnp.where(strict, d_, 0.0), beta, kk, dec)
    qk = _each(lambda x, d_: _bf(x * d_), qk, dec)
    egam = _each(jnp.exp, gam)
    rhs = _each(lambda b_, e, k, v: jnp.concatenate([(b_ * e) * k, b_ * v], axis=-1),
                beta, egam, k_s, v_s)
    t_inv = _unit_lower_inverse(a_mat, row, col, eye)
    sol = _each(_dot3, t_inv, rhs)
    wb = _each(lambda x: _bf(x[:, :DN_D]), sol)
    uv = _each(lambda x: x[:, DN_D:], sol)

    hrows = [slice(h * CHUNK, (h + 1) * CHUNK) for h in heads]
    s_f = [[state[rb * DN_HEADS + h] for h in heads] for rb in seqs]
    s_b = [[_bf(x) for x in per_seq] for per_seq in s_f]
    ub = _each(lambda u_, w_, s_: _bf(jnp.concatenate(
        [u_[hrows[h]] - _dot(w_[hrows[h]], s_[h]) for h in heads], axis=0)), uv, wb, s_b)
    o_intra = _each(_dot, qk, ub)
    for h in heads:
        rows = hrows[h]
        sl = slice(h * DN_D, (h + 1) * DN_D)
        o = _each(lambda e, q_, s_, oi: e[rows] * _dot(q_[rows], s_[h]) + oi[rows], egam, qb, s_b, o_intra)
        g_last = _each(lambda g: g[(h + 1) * CHUNK - 1:(h + 1) * CHUNK, :], gam)
        k_dec = _each(lambda k, gl, g: _bf((k[rows] * jnp.exp(gl - g[rows])).T), k_s, g_last, gam)
        upd = _each(lambda kd, u_: _dot(kd, u_[rows]), k_dec, ub)
        for rb in seqs:
            state[rb * DN_HEADS + h] = jnp.exp(g_last[rb]) * s_f[rb][h] + upd[rb]
            o_ref[rb, :, sl] = _rms(o[rb], nw_ref[...]) * _silu(z_ref[rb, :, sl])

    @pl.when(c == pl.num_programs(1) - 1)
    def _():
        for rb in seqs:
            sfin_ref[rb] = state[rb * DN_HEADS:(rb + 1) * DN_HEADS]


def _gdn(qkv3, z3, ba3, cbuf8, s0, conv_w8, par, dn_norm, t_len, zero_blocks=0, d_model=0):
    b, tp, _ = qkv3.shape
    assert tp % CHUNK == 0
    gb = GDN_ROWS if b % GDN_ROWS == 0 else 1
    grid = (b // gb, tp // CHUNK)
    kern = functools.partial(_gdn_kernel, t_len=t_len, gb=gb, zero_blocks=zero_blocks, grid=grid)
    extra_out = [pl.BlockSpec(memory_space=pl.ANY)] if zero_blocks else []
    extra_shape = [jax.ShapeDtypeStruct((zero_blocks * ROW_BLOCK, d_model), F32)] if zero_blocks else []
    extra_scratch = [pltpu.VMEM((ROW_BLOCK, d_model), F32), pltpu.SemaphoreType.DMA(())] if zero_blocks else []
    return pl.pallas_call(
        kern, grid=grid,
        in_specs=[pl.BlockSpec((gb, CHUNK, CONV_CH), lambda i, j: (i, j, 0)),
                  pl.BlockSpec((gb, CHUNK, DN_W), lambda i, j: (i, j, 0)),
                  pl.BlockSpec((gb, CHUNK, BA_W), lambda i, j: (i, j, 0)),
                  pl.BlockSpec((gb, CONV_HALO, CONV_CH), lambda i, j: (i, 0, 0)),
                  pl.BlockSpec((gb, DN_HEADS, DN_D, DN_D), lambda i, j: (i, 0, 0, 0)),
                  pl.BlockSpec((8, CONV_CH), lambda i, j: (0, 0)),
                  pl.BlockSpec((8, LANES), lambda i, j: (0, 0)),
                  pl.BlockSpec((1, DN_D), lambda i, j: (0, 0))],
        out_specs=[pl.BlockSpec((gb, CHUNK, DN_W), lambda i, j: (i, j, 0)),
                   pl.BlockSpec((gb, DN_HEADS, DN_D, DN_D), lambda i, j: (i, 0, 0, 0))] + extra_out,
        out_shape=[jax.ShapeDtypeStruct((b, tp, DN_W), F32),
                   jax.ShapeDtypeStruct((b, DN_HEADS, DN_D, DN_D), F32)] + extra_shape,
        scratch_shapes=[pltpu.VMEM((gb * DN_HEADS, DN_D, DN_D), F32),
                        pltpu.VMEM((gb, CONV_HALO + CHUNK, CONV_CH), F32)] + extra_scratch,
        compiler_params=_cparams(2), name="gdn")(qkv3, z3, ba3, cbuf8, s0, conv_w8, par, dn_norm)


STEP_TILE = 8


def _gdn_step_kernel(qkv_ref, z_ref, ba_ref, cst_ref, s0_all_ref, cw_ref, par_ref, nw_ref, o_ref, s_ref):
    s0_ref = s0_all_ref.at[0]
    acc = qkv_ref[...] * cw_ref[CONV_W - 1:CONV_W, :]
    for j in range(CONV_W - 1):
        acc = acc + cst_ref[:, j, :] * cw_ref[j:j + 1, :]
    qkvc = _silu(acc)
    ba = ba_ref[...]
    head_lane = lax.broadcasted_iota(I32, (1, LANES), 1) < DN_HEADS
    beta_all = _sigmoid(ba[:, :LANES])
    decay = jnp.where(head_lane, -jnp.exp(par_ref[0:1, :]), 0.0)
    eg_all = jnp.exp(decay * _softplus(ba[:, LANES:] + par_ref[1:2, :]))
    pad6 = jnp.zeros((6, DN_D), F32)
    pad7 = jnp.zeros((7, DN_D), F32)
    for h in range(DN_HEADS):
        q = qkvc[:, h * DN_D:(h + 1) * DN_D]
        k = qkvc[:, DN_W + h * DN_D:DN_W + (h + 1) * DN_D]
        v = qkvc[:, 2 * DN_W + h * DN_D:2 * DN_W + (h + 1) * DN_D]
        q = q * lax.rsqrt(jnp.sum(q * q, axis=-1, keepdims=True) + EPS) * (DN_D ** -0.5)
        k = k * lax.rsqrt(jnp.sum(k * k, axis=-1, keepdims=True) + EPS)
        qk = _dot_nt(_bf(q), _bf(k))
        seqs = list(range(STEP_TILE))
        beta = [beta_all[b:b + 1, h:h + 1] for b in seqs]
        eg = [eg_all[b:b + 1, h:h + 1] for b in seqs]
        s_old = [s0_ref[b, h] for b in seqs]
        ks_qs = _each(lambda b, s_: _dot(_bf(jnp.concatenate([k[b:b + 1], q[b:b + 1], pad6], axis=0)), _bf(s_)),
                      seqs, s_old)
        u = _each(lambda b, b_, e, r: b_ * (v[b:b + 1] - e * r[0:1]), seqs, beta, eg, ks_qs)
        o_rows = _each(lambda b, e, r, u_: e * r[1:2] + qk[b:b + 1, b:b + 1] * u_, seqs, eg, ks_qs, u)
        k_col = [_bf(jnp.concatenate([k[b:b + 1], pad7], axis=0).T) for b in seqs]
        upd = _each(lambda kc, u_: _dot(kc, _bf(jnp.concatenate([u_, pad7], axis=0))), k_col, u)
        for b in seqs:
            s_ref[b, h] = eg[b] * s_old[b] + upd[b]
        o = jnp.concatenate(o_rows, axis=0)
        sl = slice(h * DN_D, (h + 1) * DN_D)
        o_ref[:, sl] = _rms(o, nw_ref[...]) * _silu(z_ref[:, sl])


def _gdn_step(qkv2, z2, ba2, conv_state, s0_all, layer, conv_w8, par, dn_norm):
    b = qkv2.shape[0]
    assert b % STEP_TILE == 0
    row = lambda w: pl.BlockSpec((STEP_TILE, w), lambda i: (i, 0))
    st = pl.BlockSpec((STEP_TILE, DN_HEADS, DN_D, DN_D), lambda i: (i, 0, 0, 0))
    st_in = pl.BlockSpec((1, STEP_TILE, DN_HEADS, DN_D, DN_D), lambda i: (layer, i, 0, 0, 0))
    return pl.pallas_call(
        _gdn_step_kernel, grid=(b // STEP_TILE,),
        in_specs=[row(CONV_CH), row(DN_W), row(BA_W),
                  pl.BlockSpec((STEP_TILE, CONV_W - 1, CONV_CH), lambda i: (i, 0, 0)), st_in,
                  pl.BlockSpec((8, CONV_CH), lambda i: (0, 0)),
                  pl.BlockSpec((8, LANES), lambda i: (0, 0)),
                  pl.BlockSpec((1, DN_D), lambda i: (0, 0))],
        out_specs=[row(DN_W), st],
        out_shape=[jax.ShapeDtypeStruct((b, DN_W), F32),
                   jax.ShapeDtypeStruct((b, DN_HEADS, DN_D, DN_D), F32)],
        compiler_params=_cparams(1), name="gdn_step")(qkv2, z2, ba2, conv_state, s0_all, conv_w8, par, dn_norm)


def _out_proj_kernel(pool_ref, dn_ref, x_ref, gt_ref, sh_ref, sc_ref, nw_ref, w_ref, x1_ref, h2_ref):
    mix = _dot(pool_ref[...].astype(BF16), w_ref[:POOL_W, :]) + _dot(dn_ref[...].astype(BF16), w_ref[POOL_W:, :])
    x1 = x_ref[...] + gt_ref[0] * mix
    x1_ref[...] = x1
    h2_ref[...] = _rms(x1, nw_ref[...]) * (1.0 + sc_ref[0]) + sh_ref[0]


def _out_proj(pool2, dn2, x2d, mod3, nw, w_out_b, tm, tiles_per_batch):
    n, d = x2d.shape
    rows = mod3.shape[1]
    tok = lambda w: pl.BlockSpec((tm, w), lambda i: (i, 0))
    return pl.pallas_call(
        _out_proj_kernel, grid=(n // tm,),
        in_specs=[tok(POOL_W), tok(DN_W), tok(d),
                  _mod_spec(rows, d, tiles_per_batch, 2),
                  _mod_spec(rows, d, tiles_per_batch, 3),
                  _mod_spec(rows, d, tiles_per_batch, 4),
                  pl.BlockSpec((1, d), lambda i: (0, 0)),
                  pl.BlockSpec((POOL_W + DN_W, d), lambda i: (0, 0))],
        out_specs=[tok(d), tok(d)],
        out_shape=[jax.ShapeDtypeStruct((n, d), F32)] * 2,
        compiler_params=_cparams(1), name="out_proj")(pool2, dn2, x2d, mod3, mod3, mod3, nw, w_out_b)


def _router_kernel(h_ref, rw_ref, rb_ref, cnt0_ref, eidx_ref, wts_ref, rank_ref, cnt_ref, *more, tm, zero_blocks,
                   n_steps):
    i = pl.program_id(0)
    if zero_blocks:
        xs_hbm, run, zbuf, zsem = more
        _zero_fill(i, n_steps, zero_blocks, xs_hbm, zbuf, zsem)
    else:
        (run,) = more

    @pl.when(i == 0)
    def _():
        run[...] = cnt0_ref[...]

    s = _sigmoid(_dot(h_ref[...].astype(BF16), rw_ref[...]))
    sel = s + rb_ref[...]
    lane_i = lax.broadcasted_iota(I32, (tm, N_EXPERTS), 1)
    lane = lane_i.astype(F32)
    grp = lane_i >> GROUP_SHIFT
    neg = -jnp.inf

    def first_max(x):
        m = jnp.max(x, axis=-1, keepdims=True)
        idx = jnp.min(jnp.where(x == m, lane, float(N_EXPERTS)), axis=-1, keepdims=True)
        return m, idx

    gs = []
    for g in range(N_GROUPS):
        xg = jnp.where(grp == g, sel, neg)
        m1, i1 = first_max(xg)
        m2 = jnp.max(jnp.where(lane == i1, neg, xg), axis=-1, keepdims=True)
        gs.append(m1 + m2)
    cur = jnp.full((tm, N_EXPERTS), neg, F32)
    for g in range(N_GROUPS):
        ahead = jnp.zeros((tm, 1), F32)
        for o in range(N_GROUPS):
            if o < g:
                ahead = ahead + jnp.where(gs[o] >= gs[g], 1.0, 0.0)
            elif o > g:
                ahead = ahead + jnp.where(gs[o] > gs[g], 1.0, 0.0)
        cur = jnp.where((grp == g) & (ahead < TOPK_GROUPS), sel, cur)

    hits, idxs, ws = [], [], []
    chosen = jnp.zeros((tm, N_EXPERTS), F32)
    for _ in range(TOP_K):
        _, idx = first_max(cur)
        hit = lane == idx
        ws.append(jnp.sum(jnp.where(hit, s, 0.0), axis=-1, keepdims=True))
        cur = jnp.where(hit, neg, cur)
        chosen = jnp.where(hit, 1.0, chosen)
        hits.append(hit)
        idxs.append(idx)
    onehot = chosen
    r = lax.broadcasted_iota(I32, (tm, tm), 0)
    c = lax.broadcasted_iota(I32, (tm, tm), 1)
    earlier = jnp.where(r > c, 1.0, 0.0).astype(BF16)
    before = _dot(earlier, onehot.astype(BF16)) + run[...]
    run[...] = run[...] + jnp.sum(onehot, axis=0, keepdims=True)

    wsum = ws[0]
    for w in ws[1:]:
        wsum = wsum + w
    out_lane = lax.broadcasted_iota(I32, (tm, LANES), 1)
    eidx = jnp.zeros((tm, LANES), F32)
    wts = jnp.zeros((tm, LANES), F32)
    rank = jnp.zeros((tm, LANES), F32)
    for k in range(TOP_K):
        rk = jnp.sum(jnp.where(hits[k], before, 0.0), axis=-1, keepdims=True)
        eidx = jnp.where(out_lane == k, idxs[k], eidx)
        wts = jnp.where(out_lane == k, ws[k] / wsum * ROUTED_SCALE, wts)
        rank = jnp.where(out_lane == k, rk, rank)
    eidx_ref[...] = eidx.astype(I32)
    wts_ref[...] = wts
    rank_ref[...] = rank.astype(I32)

    @pl.when(i == pl.num_programs(0) - 1)
    def _():
        cnt_ref[...] = run[...]


def _token_specs(n_p, tm, d_block, index_rest):
    np_tiles = n_p // tm
    return (pl.BlockSpec((tm,) + d_block, lambda i, *_: (jnp.minimum(i, np_tiles - 1),) + index_rest),
            pl.BlockSpec((tm,) + d_block, lambda i, *_: (jnp.maximum(i - np_tiles, 0),) + index_rest))


def _router(h, router_w_b, router_bias, counts0, zero_blocks=0):
    n, d = h.shape
    tm = next(t for t in (512, 256, 128, 64, 32, 16, 8) if n % t == 0)
    tok = pl.BlockSpec((tm, LANES), lambda i: (i, 0))
    cnt = pl.BlockSpec((1, N_EXPERTS), lambda i: (0, 0))
    out_specs = [tok, tok, tok, cnt]
    out_shape = [jax.ShapeDtypeStruct((n, LANES), I32), jax.ShapeDtypeStruct((n, LANES), F32),
                 jax.ShapeDtypeStruct((n, LANES), I32), jax.ShapeDtypeStruct((1, N_EXPERTS), F32)]
    scratch = [pltpu.VMEM((1, N_EXPERTS), F32)]
    if zero_blocks:
        out_specs.append(pl.BlockSpec(memory_space=pl.ANY))
        out_shape.append(jax.ShapeDtypeStruct((zero_blocks * ROW_BLOCK, d), F32))
        scratch += [pltpu.VMEM((ROW_BLOCK, d), F32), pltpu.SemaphoreType.DMA(())]
    return pl.pallas_call(
        functools.partial(_router_kernel, tm=tm, zero_blocks=zero_blocks, n_steps=n // tm), grid=(n // tm,),
        in_specs=[pl.BlockSpec((tm, d), lambda i: (i, 0)),
                  pl.BlockSpec((d, N_EXPERTS), lambda i: (0, 0)), cnt, cnt],
        out_specs=out_specs, out_shape=out_shape, scratch_shapes=scratch,
        compiler_params=_cparams(1), name="router")(h, router_w_b, router_bias, counts0)


SUBLANES = 8


def _gather_rows(idx_ref, src_hbm, dst, sem):
    def body(g, carry):
        for j in range(SUBLANES):
            tok = idx_ref[0, 0, g * SUBLANES + j]
            pltpu.make_async_copy(src_hbm.at[pl.ds(tok, 1)], dst.at[g, pl.ds(j, 1)], sem).start(priority=j % 2)
        return carry
    lax.fori_loop(0, dst.shape[0], body, 0)


def _wait_rows(dst, sem):
    pltpu.make_async_copy(dst, dst, sem).wait()


def _dest_kernel(eidx_ref, rank_ref, ps_ref, dest_ref, *, tm):
    lane = lax.broadcasted_iota(I32, (tm, N_EXPERTS), 1)
    out_lane = lax.broadcasted_iota(I32, (tm, LANES), 1)
    eidx = eidx_ref[...]
    rank = rank_ref[...].astype(F32)
    cols = [eidx[:, k:k + 1] for k in range(TOP_K)]
    picked = [jnp.where(lane == c, ps_ref[...], 0.0) for c in cols]
    starts = [jnp.sum(p, axis=-1, keepdims=True) for p in picked]
    dest = jnp.zeros((tm, LANES), F32)
    for k in range(TOP_K):
        dest = jnp.where(out_lane == k, starts[k] + rank[:, k:k + 1], dest)
    dest_ref[...] = dest.astype(I32)


def _dest(eidx, rank, pstarts, tm):
    n = eidx.shape[0]
    tok = pl.BlockSpec((tm, LANES), lambda i: (i, 0))
    return pl.pallas_call(
        functools.partial(_dest_kernel, tm=tm), grid=(n // tm,),
        in_specs=[tok, tok, pl.BlockSpec((1, N_EXPERTS), lambda i: (0, 0))],
        out_specs=tok, out_shape=jax.ShapeDtypeStruct((n, LANES), I32),
        compiler_params=_cparams(1), name="dest")(eidx, rank, pstarts)


def _dispatch_kernel(dest_ref, hp_ref, hs_ref, xs_in, xs_hbm, pbuf, sems, *, tm, np_tiles):
    del xs_in
    i = pl.program_id(0)
    n = pl.num_programs(0)
    slot = i % 2

    def wait_slot(s):
        for _ in range(TOP_K):
            _wait_rows(pbuf.at[s], sems.at[s])

    @pl.when(i >= 2)
    def _():
        wait_slot(slot)

    pbuf[slot] = jnp.where(i < np_tiles, hp_ref[...], hs_ref[...])

    def body(g, carry):
        for j in range(SUBLANES):
            for k in range(TOP_K):
                row = dest_ref[0, 0, (g * SUBLANES + j) * TOP_K + k]
                pltpu.make_async_copy(pbuf.at[slot, g, pl.ds(j, 1)], xs_hbm.at[pl.ds(row, 1)],
                                      sems.at[slot]).start(priority=k % 2)
        return carry
    lax.fori_loop(0, tm // SUBLANES, body, 0)

    @pl.when(i == n - 1)
    def _():
        wait_slot(slot)

    @pl.when(jnp.logical_and(i == n - 1, n > 1))
    def _():
        wait_slot(1 - slot)


def _dispatch(dest3, h_p, h_s, x_zero, tm):
    (n_p, d), n_s = h_p.shape, h_s.shape[0]
    groups = tm // SUBLANES
    rows3 = lambda h: h.reshape(h.shape[0] // SUBLANES, SUBLANES, d)
    return pl.pallas_call(
        functools.partial(_dispatch_kernel, tm=tm, np_tiles=n_p // tm), grid=((n_p + n_s) // tm,),
        in_specs=[pl.BlockSpec((1, 1, tm * TOP_K), lambda i: (i, 0, 0), memory_space=pltpu.SMEM),
                  *_token_specs(n_p // SUBLANES, groups, (SUBLANES, d), (0, 0)),
                  pl.BlockSpec(memory_space=pl.ANY)],
        out_specs=pl.BlockSpec(memory_space=pl.ANY),
        out_shape=jax.ShapeDtypeStruct(x_zero.shape, F32),
        scratch_shapes=[pltpu.VMEM((2, groups, SUBLANES, d), F32), pltpu.SemaphoreType.DMA((2,))],
        input_output_aliases={3: 0},
        compiler_params=_cparams(1), name="dispatch")(dest3, rows3(h_p), rows3(h_s), x_zero)


def _experts_kernel(be_ref, nact_ref, x_ref, g_hbm, u_hbm, d_hbm, y_ref, gf, uf, df, gb, ub, db, sems, slot_ref,
                    *, base, nb):
    i = pl.program_id(0)
    nact = nact_ref[0]

    def weight_copies(e, s):
        return (pltpu.make_async_copy(g_hbm.at[base + e], gf.at[s], sems.at[s]),
                pltpu.make_async_copy(u_hbm.at[base + e], uf.at[s], sems.at[s]),
                pltpu.make_async_copy(d_hbm.at[base + e], df.at[s], sems.at[s]))

    @pl.when(jnp.logical_and(i == 0, nact > 0))
    def _():
        slot_ref[0] = 0
        for c in weight_copies(be_ref[0], 0):
            c.start()

    first = jnp.logical_or(i == 0, be_ref[i] != be_ref[jnp.maximum(i - 1, 0)])

    @pl.when(jnp.logical_and(i < nact, first))
    def _():
        s = slot_ref[0]
        here = be_ref[i]
        nxt = lax.while_loop(lambda j: jnp.logical_and(j < nact, be_ref[jnp.minimum(j, nb - 1)] == here),
                             lambda j: j + 1, i + 1)

        @pl.when(nxt < nact)
        def _():
            for c in weight_copies(be_ref[jnp.minimum(nxt, nb - 1)], 1 - s):
                c.start()
        for c in weight_copies(here, s):
            c.wait()
        gb[...] = gf[s].astype(BF16)
        ub[...] = uf[s].astype(BF16)
        db[...] = df[s].astype(BF16)
        slot_ref[0] = 1 - s

    @pl.when(i < nact)
    def _():
        x = x_ref[...].astype(BF16)
        hid = _silu(_dot(x, gb[...])) * _dot(x, ub[...])
        y_ref[...] = _dot(hid.astype(BF16), db[...])


def _experts(block_e, nact, x_sorted, exp_gate, exp_up, exp_down, layer):
    nb = x_sorted.shape[0] // ROW_BLOCK
    _, d, de = exp_gate.shape
    base = layer * N_EXPERTS
    rows = pl.BlockSpec((ROW_BLOCK, d), lambda i, be, na: (jnp.minimum(i, jnp.maximum(na[0] - 1, 0)), 0))
    grid_spec = pltpu.PrefetchScalarGridSpec(
        num_scalar_prefetch=2, grid=(nb,),
        in_specs=[rows, pl.BlockSpec(memory_space=pl.ANY), pl.BlockSpec(memory_space=pl.ANY),
                  pl.BlockSpec(memory_space=pl.ANY)],
        out_specs=rows,
        scratch_shapes=[pltpu.VMEM((2, d, de), F32), pltpu.VMEM((2, d, de), F32), pltpu.VMEM((2, de, d), F32),
                        pltpu.VMEM((d, de), BF16), pltpu.VMEM((d, de), BF16), pltpu.VMEM((de, d), BF16),
                        pltpu.SemaphoreType.DMA((2,)), pltpu.SMEM((1,), I32)])
    return pl.pallas_call(
        functools.partial(_experts_kernel, base=base, nb=nb), grid_spec=grid_spec,
        out_shape=jax.ShapeDtypeStruct((nb * ROW_BLOCK, d), F32),
        input_output_aliases={2: 0},
        compiler_params=_cparams(1), name="experts")(block_e, nact, x_sorted, exp_gate, exp_up, exp_down)


def _combine_kernel(cur_ref, nxt_ref, y_hbm, wts_ref, x1_ref, h_ref, gt_ref, sg_ref, su_ref, sd_ref,
                    nf_ref, x2_ref, yn_ref, ybuf, sems, *, tm):
    i = pl.program_id(0)
    n = pl.num_programs(0)
    slot = i % 2
    groups = tm // SUBLANES

    @pl.when(i == 0)
    def _():
        _gather_rows(cur_ref, y_hbm, ybuf.at[0], sems.at[0])

    @pl.when(i + 1 < n)
    def _():
        _gather_rows(nxt_ref, y_hbm, ybuf.at[1 - slot], sems.at[1 - slot])

    hb = h_ref[...].astype(BF16)
    hid = _silu(_dot(hb, sg_ref[...])) * _dot(hb, su_ref[...])
    acc = _dot(hid.astype(BF16), sd_ref[...])
    _wait_rows(ybuf.at[slot], sems.at[slot])
    wts = wts_ref[...]
    for k in range(TOP_K):
        y_k = ybuf[slot, k * groups:(k + 1) * groups].reshape(tm, acc.shape[1])
        acc = acc + y_k * wts[:, k:k + 1]
    x2 = x1_ref[...] + gt_ref[0] * acc
    x2_ref[...] = x2
    yn_ref[...] = _rms(x2, nf_ref[...])


def _combine(dest3, y_sorted, wts, x1, h2, mod3, sh_gate_b, sh_up_b, sh_down_b, norm_final, tm, tiles_per_batch):
    n, d = x1.shape
    nt = n // tm
    rows = mod3.shape[1]
    ds = sh_gate_b.shape[1]
    tok = lambda w: pl.BlockSpec((tm, w), lambda i: (i, 0))
    kern = functools.partial(_combine_kernel, tm=tm)
    return pl.pallas_call(
        kern, grid=(nt,),
        in_specs=[pl.BlockSpec((1, 1, tm * TOP_K), lambda i: (i, 0, 0), memory_space=pltpu.SMEM),
                  pl.BlockSpec((1, 1, tm * TOP_K), lambda i: (jnp.minimum(i + 1, nt - 1), 0, 0),
                               memory_space=pltpu.SMEM),
                  pl.BlockSpec(memory_space=pl.ANY),
                  tok(LANES), tok(d), tok(d),
                  _mod_spec(rows, d, tiles_per_batch, 5),
                  pl.BlockSpec((d, ds), lambda i: (0, 0)),
                  pl.BlockSpec((d, ds), lambda i: (0, 0)),
                  pl.BlockSpec((ds, d), lambda i: (0, 0)),
                  pl.BlockSpec((1, d), lambda i: (0, 0))],
        out_specs=[tok(d), tok(d)],
        out_shape=[jax.ShapeDtypeStruct((n, d), F32)] * 2,
        scratch_shapes=[pltpu.VMEM((2, tm * TOP_K // SUBLANES, SUBLANES, d), F32),
                        pltpu.SemaphoreType.DMA((2,))],
        compiler_params=_cparams(1), name="combine")(
            dest3, dest3, y_sorted, wts, x1, h2, mod3, sh_gate_b, sh_up_b, sh_down_b, norm_final)


def _last_rows(buf, new, keep):
    t = new.shape[1]
    if t >= keep:
        return new[:, t - keep:]
    return jnp.concatenate([buf[:, t:], new], axis=1)


def _mixers(u, qkv, z, ba, b, t, pool_buf, conv_buf, s0, start_pos, lw):
    t_pool = -(-t // 8) * 8
    t_gdn = -(-t // CHUNK) * CHUNK
    u3 = u.reshape(b, t, POOL_W)
    qkv3 = qkv.reshape(b, t, CONV_CH)
    pad_t = lambda a, tp: a if tp == t else jnp.pad(a, ((0, 0), (0, tp - t), (0, 0)))
    buf16 = jnp.pad(pool_buf, ((0, 0), (POOL_HALO - POOL_BUF, 0), (0, 0)))
    pool_out = _pool(pad_t(u3, t_pool), buf16, lw["pool_w"], lw["pool_scale"], start_pos)[:, :t]
    if t == 1:
        s0_all, layer = s0
        dn_out, s_new = _gdn_step(qkv, z, ba, conv_buf, s0_all, layer, lw["conv_w"], lw["par"], lw["dn_norm"])
        x_zero = None
    else:
        cbuf8 = jnp.pad(conv_buf, ((0, 0), (CONV_HALO - (CONV_W - 1), 0), (0, 0)))
        dn_out, s_new, x_zero = _gdn(pad_t(qkv3, t_gdn), pad_t(z.reshape(b, t, DN_W), t_gdn),
                                     pad_t(ba.reshape(b, t, BA_W), t_gdn), cbuf8, s0, lw["conv_w"], lw["par"],
                                     lw["dn_norm"], t, zero_blocks=lw["zero_blocks"], d_model=lw["d_model"])
        dn_out = dn_out[:, :t].reshape(b * t, DN_W)
    new_pool = _last_rows(pool_buf, u3, POOL_BUF)
    new_conv = _last_rows(conv_buf, qkv3, CONV_W - 1)
    return pool_out.reshape(b * t, POOL_W), dn_out, new_pool, new_conv, s_new, x_zero


def _block_tables(counts, n_tok):
    nb = _n_row_blocks(n_tok)
    pcounts = (counts + ROW_BLOCK - 1) // ROW_BLOCK * ROW_BLOCK
    pends = jnp.cumsum(pcounts)
    pstarts = pends - pcounts
    first_row = jnp.arange(nb, dtype=I32)[:, None] * ROW_BLOCK
    block_e = jnp.minimum(jnp.sum((pends[None, :] <= first_row).astype(I32), axis=1), N_EXPERTS - 1)
    nact = (pends[-1] // ROW_BLOCK).astype(I32).reshape(1)
    return pstarts.astype(F32).reshape(1, N_EXPERTS), block_e, nact


def _n_row_blocks(n_tok):
    return -(-(n_tok * TOP_K + N_EXPERTS * (ROW_BLOCK - 1)) // ROW_BLOCK)


def _dest_tiles(dest, tm, k_major):
    n = dest.shape[0]
    tiles = dest.reshape(n // tm, tm, TOP_K)
    if k_major:
        tiles = tiles.transpose(0, 2, 1)
    return tiles.reshape(n // tm, 1, tm * TOP_K)


def kernel(x_prompt, x_sample, state_pool, state_conv, state_delta, c_prompt, c_sample, norm_mix, norm_ffn, w_ada, b_ada, w_in, pool_w, pool_scale, conv_w, a_log, dt_bias, dn_norm, w_out, router_w, router_bias, exp_gate, exp_up, exp_down, sh_gate, sh_up, sh_down, norm_final):
    bp, tp, d = x_prompt.shape
    bs, ts, _ = x_sample.shape
    depth = w_ada.shape[0]
    past_len = PAST_LEN
    n_p, n_s = bp * tp, bs * ts
    n_tok = n_p + n_s
    tm_p = min(512, tp)
    tm_c = min(128, tp)
    tm_r = next(t for t in (128, 64, 32, 16, 8) if n_p % t == 0 and n_s % t == 0)
    assert tp % tm_p == 0 and ts == 1 and bs % 8 == 0
    n_exp = exp_gate.shape[1]
    gate_w = exp_gate.reshape((depth * n_exp,) + exp_gate.shape[2:])
    up_w = exp_up.reshape((depth * n_exp,) + exp_up.shape[2:])
    down_w = exp_down.reshape((depth * n_exp,) + exp_down.shape[2:])

    c_all = jnp.concatenate([c_prompt, c_sample], axis=0)
    m_rows = -(-c_all.shape[0] // 8) * 8
    mod_all = _ada(jnp.pad(c_all, ((0, m_rows - c_all.shape[0]), (0, 0))), w_ada, b_ada)

    c2 = POOL_W + CONV_CH + DN_W
    lane_pad = jnp.zeros((depth, d, LANES - DN_HEADS), F32)
    w_in_p = jnp.concatenate([w_in[:, :, :c2], w_in[:, :, c2:c2 + DN_HEADS], lane_pad,
                              w_in[:, :, c2 + DN_HEADS:], lane_pad], axis=-1).astype(BF16)
    head_pad = lambda a: jnp.pad(a, ((0, 0), (0, LANES - DN_HEADS)))
    par_all = jnp.stack([head_pad(a_log), head_pad(dt_bias)] + [jnp.zeros((depth, LANES), F32)] * 6, axis=1)
    conv_w8 = jnp.pad(conv_w, ((0, 0), (0, 8 - CONV_W), (0, 0)))

    xp = x_prompt.reshape(n_p, d)
    xs = x_sample.reshape(n_s, d)
    zero_pool = jnp.zeros((bp, POOL_BUF, POOL_W), F32)
    zero_conv = jnp.zeros((bp, CONV_W - 1, CONV_CH), F32)
    zero_state = jnp.zeros((bp, DN_HEADS, DN_D, DN_D), F32)
    outs = {k: [] for k in ("pool_p", "conv_p", "delta_p", "pool_s", "conv_s", "delta_s")}
    yp = ys = None
    for l in range(depth):
        lw = {"pool_w": pool_w[l].astype(BF16), "pool_scale": pool_scale[l].reshape(1, POOL_W),
              "conv_w": conv_w8[l], "par": par_all[l], "dn_norm": dn_norm[l].reshape(1, DN_D),
              "zero_blocks": _n_row_blocks(n_tok), "d_model": d}
        mod_p = mod_all[l, :bp].reshape(bp, 1, 6 * d)
        mod_s = mod_all[l, bp:bp + bs].reshape(1, bs, 6 * d)
        nw_mix = norm_mix[l].reshape(1, d)
        nw_ffn = norm_ffn[l].reshape(1, d)
        w_out_b = w_out[l].astype(BF16)

        up, qkvp, zp, bap = _norm_proj(xp, mod_p, nw_mix, w_in_p[l], tm_p, tp // tm_p)
        us, qkvs, zs, bas = _norm_proj(xs, mod_s, nw_mix, w_in_p[l], n_s, 1)
        pool_p, dn_p, npool_p, nconv_p, ns_p, x_zero = _mixers(up, qkvp, zp, bap, bp, tp, zero_pool, zero_conv,
                                                               zero_state, 0, lw)
        pool_s, dn_s, npool_s, nconv_s, ns_s, _ = _mixers(us, qkvs, zs, bas, bs, ts, state_pool[l], state_conv[l],
                                                          (state_delta, l), past_len, lw)
        for key, val in (("pool_p", npool_p), ("conv_p", nconv_p), ("delta_p", ns_p),
                         ("pool_s", npool_s), ("conv_s", nconv_s), ("delta_s", ns_s)):
            outs[key].append(val)

        x1p, h2p = _out_proj(pool_p, dn_p, xp, mod_p, nw_ffn, w_out_b, tm_p, tp // tm_p)
        x1s, h2s = _out_proj(pool_s, dn_s, xs, mod_s, nw_ffn, w_out_b, n_s, 1)

        rw_b, rb = router_w[l].astype(BF16), router_bias[l].reshape(1, N_EXPERTS)
        eidx_p, wts_p, rank_p, counts = _router(h2p, rw_b, rb, jnp.zeros((1, N_EXPERTS), F32))
        eidx_s, wts_s, rank_s, counts = _router(h2s, rw_b, rb, counts)
        pstarts, block_e, nact = _block_tables(counts.reshape(-1).astype(I32), n_tok)
        dest = jnp.concatenate([_dest(eidx_p, rank_p, pstarts, tm_r)[:, :TOP_K],
                                _dest(eidx_s, rank_s, pstarts, tm_r)[:, :TOP_K]])
        x_sorted = _dispatch(_dest_tiles(dest, tm_r, False), h2p, h2s, x_zero, tm_r)
        y_sorted = _experts(block_e, nact, x_sorted, gate_w, up_w, down_w, l)

        shg, shu, shd = sh_gate[l].astype(BF16), sh_up[l].astype(BF16), sh_down[l].astype(BF16)
        nf = norm_final.reshape(1, d)
        xp, yp = _combine(_dest_tiles(dest[:n_p], tm_c, True), y_sorted, wts_p, x1p, h2p, mod_p, shg, shu, shd,
                          nf, tm_c, tp // tm_c)
        xs, ys = _combine(_dest_tiles(dest[n_p:], n_s, True), y_sorted, wts_s, x1s, h2s, mod_s, shg, shu, shd,
                          nf, n_s, 1)

    stack = lambda key: jnp.stack(outs[key])
    return (yp.reshape(bp, tp, d), ys.reshape(bs, ts, d), stack("pool_p"), stack("conv_p"), stack("delta_p"),
            stack("pool_s"), stack("conv_s"), stack("delta_s"))
```
